```python
import jax
import jax.numpy as jnp
from jax import lax
import numpy as np

D_MODEL = 4096
BATCH = 4
SEQ = 2048
DEPTH = 2
DEC_BATCH = 8
DEC_SEQ = 4
PAST_LEN = 16384
PAGE_SIZE = 128

N_MIXERS = 2
N_CONV_LAYERS = (DEPTH + 1) // 2
N_NSA_LAYERS = DEPTH // 2
CONV_W = 3
N_HEADS = 32
HEAD_DIM = D_MODEL // N_HEADS
N_KV_HEADS = 4
GROUP = N_HEADS // N_KV_HEADS
CMP_BLOCK = 32
CMP_STRIDE = 16
CMP_PARTS = CMP_BLOCK // CMP_STRIDE
CMP_HIDDEN = 2 * HEAD_DIM
SEL_BLOCK = 64
N_SEL = 16
WINDOW = 512
Q_BLOCK = 32
Q_DIM = N_HEADS * HEAD_DIM
KV_DIM = 2 * N_KV_HEADS * HEAD_DIM
NSA_IN = Q_DIM + 3 * KV_DIM + 3 * N_HEADS
N_EXPERTS = 32
TOP_K = 4
D_FF = D_MODEL // 2
SWIGLU_LIMIT = 7.0
SWIGLU_ALPHA = 1.702
MOE_BLOCK = 128
MOE_BLOCK_SMALL = 8
EPS = 1e-6
FORCE_SCORE = 1e4

kernel_name = 'hybrid_shortconv_nsa_moe_adaln_step'


def alibi_slopes():
    return jnp.exp2(-8.0 * jnp.arange(1, N_HEADS + 1, dtype=jnp.float32) / N_HEADS)


def rmsnorm(x, g):
    xf = x.astype(jnp.float32)
    y = xf * lax.rsqrt(jnp.mean(xf * xf, axis=-1, keepdims=True) + EPS)
    return (y * g.astype(jnp.float32)).astype(x.dtype)


def adaln(c, w, b):
    mod = jax.nn.silu(c) @ w + b
    return [m[:, None, :] for m in jnp.split(mod, 6, axis=-1)]


def modulate(x, g, shift, scale):
    return rmsnorm(x, g) * (1.0 + scale) + shift


def masked_softmax(s, mask):
    s = jnp.where(mask, s, -jnp.inf)
    m = jnp.max(s, axis=-1, keepdims=True)
    m = jnp.where(jnp.isfinite(m), m, 0.0)
    e = jnp.exp(s - m)
    return e / jnp.maximum(jnp.sum(e, axis=-1, keepdims=True), 1e-30)


def conv_mixer(h, prev, w_in, w_conv, w_out):
    T = h.shape[1]
    b_gate, c_gate, u = jnp.split(h @ w_in, 3, axis=-1)
    v = c_gate * u
    ext = jnp.concatenate([prev, v], axis=1)
    conv = ext[:, 0:T] * w_conv[0]
    for k in range(1, CONV_W):
        conv = conv + ext[:, k:k + T] * w_conv[k]
    return (b_gate * conv) @ w_out, ext[:, -(CONV_W - 1):]


def nsa_project(h, w_in, q_g, k_g):
    B, T, _ = h.shape
    z = h @ w_in
    q = rmsnorm(z[..., :Q_DIM].reshape(B, T, N_HEADS, HEAD_DIM), q_g)
    kv = z[..., Q_DIM:Q_DIM + 3 * KV_DIM].reshape(B, T, 3, 2, N_KV_HEADS, HEAD_DIM)
    kv_cmp = kv[:, :, 0]
    kv_sel = jnp.stack([rmsnorm(kv[:, :, 1, 0], k_g[1]), kv[:, :, 1, 1]], axis=2)
    kv_win = jnp.stack([rmsnorm(kv[:, :, 2, 0], k_g[2]), kv[:, :, 2, 1]], axis=2)
    gates = jax.nn.sigmoid(z[..., Q_DIM + 3 * KV_DIM:]).reshape(B, T, 3, N_KV_HEADS, GROUP, 1)
    return q, kv_cmp, kv_sel, kv_win, gates


def compress(kv_rows, pe, w1, b1, w2, b2, k_g):
    B, Tk = kv_rows.shape[:2]
    n_ch = Tk // CMP_STRIDE
    n_cmp = n_ch - CMP_PARTS + 1
    ch = kv_rows[:, :n_ch * CMP_STRIDE].reshape(B, n_ch, CMP_STRIDE, 2, N_KV_HEADS, HEAD_DIM)
    part = jnp.einsum('bnsjhd,jpsdf->bpnjhf', ch, w1)
    hid = b1[:, None, :] + jnp.einsum('jpsd,jpsdf->jf', pe, w1)[:, None, :]
    for p in range(CMP_PARTS):
        hid = hid + part[:, p, p:p + n_cmp]
    out = jnp.einsum('bnjhf,jfd->bnjhd', jax.nn.silu(hid), w2) + b2[:, None, :]
    return rmsnorm(out[:, :, 0], k_g), out[:, :, 1]


def nsa_attend(q, t_pos, gates, k_c, v_c, k_s, v_s, k_w, v_w, w_pos):
    B, Tq = q.shape[:2]
    Tk = k_s.shape[1]
    slope = alibi_slopes().reshape(N_KV_HEADS, GROUP)
    qg = q.reshape(B, Tq, N_KV_HEADS, GROUP, HEAD_DIM) * (HEAD_DIM ** -0.5)

    n_cmp = k_c.shape[1]
    c_start = jnp.arange(n_cmp) * CMP_STRIDE
    d_c = t_pos[:, None] - (c_start + CMP_BLOCK - 1)[None, :]
    s_c = jnp.einsum('bthgd,bchd->bhgtc', qg, k_c, preferred_element_type=jnp.float32)
    p_c = masked_softmax(s_c - slope[:, :, None, None] * d_c, d_c >= 0)
    o_c = jnp.einsum('bhgtc,bchd->bthgd', p_c.astype(v_c.dtype), v_c)

    n_blk = -(-Tk // SEL_BLOCK)
    n_top = min(N_SEL, n_blk)
    b_start = jnp.arange(n_blk) * SEL_BLOCK
    overlap = ((c_start[:, None] < b_start[None, :] + SEL_BLOCK)
               & (c_start[:, None] + CMP_BLOCK > b_start[None, :])).astype(jnp.float32)
    imp = jnp.einsum('bhgtc,cn->bhtn', p_c, overlap)
    blk = jnp.arange(n_blk)[None, :]
    cur = (t_pos // SEL_BLOCK)[:, None]
    imp = jnp.where((blk == 0) | (blk == cur) | (blk == cur - 1), FORCE_SCORE, imp)
    imp = jnp.where(blk > cur, -jnp.inf, imp)
    _, sel = lax.top_k(imp, n_top)

    pos = (sel[..., None] * SEL_BLOCK + jnp.arange(SEL_BLOCK)).reshape(B, N_KV_HEADS, Tq, n_top * SEL_BLOCK)
    pos_c = jnp.minimum(pos, Tk - 1)
    b_ix = jnp.arange(B)[:, None, None, None]
    h_ix = jnp.arange(N_KV_HEADS)[None, :, None, None]
    k_g = k_s[b_ix, pos_c, h_ix]
    v_g = v_s[b_ix, pos_c, h_ix]
    d_s = (t_pos[None, None, :, None] - pos)[:, :, None]
    s_s = jnp.einsum('bthgd,bhtsd->bhgts', qg, k_g, preferred_element_type=jnp.float32)
    p_s = masked_softmax(s_s - slope[None, :, :, None, None] * d_s, d_s >= 0)
    o_s = jnp.einsum('bhgts,bhtsd->bthgd', p_s.astype(v_g.dtype), v_g)

    d_w = t_pos[:, None] - w_pos[None, :]
    m_w = (d_w >= 0) & (d_w < WINDOW) & (w_pos[None, :] >= 0)
    s_w = jnp.einsum('bthgd,blhd->bhgtl', qg, k_w, preferred_element_type=jnp.float32)
    p_w = masked_softmax(s_w - slope[:, :, None, None] * d_w, m_w)
    o_w = jnp.einsum('bhgtl,blhd->bthgd', p_w.astype(v_w.dtype), v_w)

    o = gates[:, :, 0] * o_c + gates[:, :, 1] * o_s + gates[:, :, 2] * o_w
    return o.reshape(B, Tq, Q_DIM)


def nsa_prompt(h, w_in, w_out, q_g, k_g, pe, w1, b1, w2, b2):
    B, T, _ = h.shape
    q, kv_c, kv_s, kv_w, gates = nsa_project(h, w_in, q_g, k_g)
    k_c, v_c = compress(kv_c, pe, w1, b1, w2, b2, k_g[0])
    k_s, v_s = kv_s[:, :, 0], kv_s[:, :, 1]
    pad = WINDOW - 1
    kv_w_pad = jnp.pad(kv_w, ((0, 0), (pad, 0), (0, 0), (0, 0), (0, 0)))

    def block(i):
        start = i * Q_BLOCK
        t_pos = start + jnp.arange(Q_BLOCK)
        qb = lax.dynamic_slice_in_dim(q, start, Q_BLOCK, axis=1)
        gb = lax.dynamic_slice_in_dim(gates, start, Q_BLOCK, axis=1)
        wb = lax.dynamic_slice_in_dim(kv_w_pad, start, Q_BLOCK + pad, axis=1)
        w_pos = start - pad + jnp.arange(Q_BLOCK + pad)
        return nsa_attend(qb, t_pos, gb, k_c, v_c, k_s, v_s, wb[:, :, 0], wb[:, :, 1], w_pos)

    o = lax.map(block, jnp.arange(T // Q_BLOCK))
    o = jnp.moveaxis(o, 0, 1).reshape(B, T, Q_DIM)
    return o @ w_out, kv_c, kv_s, kv_w[:, -min(WINDOW, T):]


def nsa_sample(h, cache_c, cache_s, cache_w, page_table, w_in, w_out, q_g, k_g, pe, w1, b1, w2, b2):
    B, T, _ = h.shape
    q, kv_c, kv_s, kv_w, gates = nsa_project(h, w_in, q_g, k_g)

    def gather(cache):
        return cache[page_table].reshape(B, -1, 2, N_KV_HEADS, HEAD_DIM)

    past_c = gather(cache_c)
    past_len = past_c.shape[1]
    all_c = jnp.concatenate([past_c, kv_c], axis=1)
    all_s = jnp.concatenate([gather(cache_s), kv_s], axis=1)
    k_c, v_c = compress(all_c, pe, w1, b1, w2, b2, k_g[0])
    n_buf = cache_w.shape[1]
    win = jnp.concatenate([cache_w, kv_w], axis=1)
    w_pos = past_len - n_buf + jnp.arange(n_buf + T)
    t_pos = past_len + jnp.arange(T)
    o = nsa_attend(q, t_pos, gates, k_c, v_c, all_s[:, :, 0], all_s[:, :, 1], win[:, :, 0], win[:, :, 1], w_pos)
    return o @ w_out, kv_c, kv_s, win[:, -n_buf:]


def moe(h, w_router, b_router, w_gu, b_gu, w_down, b_down):
    B, T, D = h.shape
    xt = h.reshape(-1, D)
    n = xt.shape[0]
    n_pairs = n * TOP_K
    blk = MOE_BLOCK if n_pairs >= N_EXPERTS * MOE_BLOCK else MOE_BLOCK_SMALL
    logits = (xt @ w_router).astype(jnp.float32) + b_router.astype(jnp.float32)
    top_v, top_i = lax.top_k(logits, TOP_K)
    gate = jax.nn.softmax(top_v, axis=-1)
    flat_e = top_i.reshape(-1)
    order = jnp.argsort(flat_e)
    sorted_e = flat_e[order]
    counts = jnp.zeros((N_EXPERTS,), jnp.int32).at[flat_e].add(1)
    padded = (counts + blk - 1) // blk * blk
    pad_end = jnp.cumsum(padded)
    pad_start = pad_end - padded
    start = jnp.cumsum(counts) - counts
    dest_sorted = pad_start[sorted_e] + jnp.arange(n_pairs) - start[sorted_e]
    dest = jnp.zeros((n_pairs,), jnp.int32).at[order].set(dest_sorted.astype(jnp.int32))
    n_blocks = -(-(n_pairs + N_EXPERTS * (blk - 1)) // blk)
    row_tok = jnp.zeros((n_blocks * blk,), jnp.int32).at[dest].set(jnp.arange(n_pairs, dtype=jnp.int32) // TOP_K)
    blk_exp = jnp.minimum(jnp.searchsorted(pad_end, jnp.arange(n_blocks) * blk, side='right'), N_EXPERTS - 1)
    xs = xt[row_tok].reshape(n_blocks, blk, D)

    def expert_block(args):
        xb, e = args
        gu = xb @ w_gu[e] + b_gu[e]
        g = jnp.minimum(gu[:, :D_FF], SWIGLU_LIMIT)
        u = jnp.clip(gu[:, D_FF:], -SWIGLU_LIMIT, SWIGLU_LIMIT)
        act = g * jax.nn.sigmoid(SWIGLU_ALPHA * g) * (u + 1.0)
        return act @ w_down[e] + b_down[e]

    ys = lax.map(expert_block, (xs, blk_exp)).reshape(n_blocks * blk, D)
    y_pairs = ys[dest].reshape(n, TOP_K, D)
    y = jnp.einsum('tk,tkd->td', gate.astype(y_pairs.dtype), y_pairs)
    return y.reshape(B, T, D)


def setup_inputs(seed: int = 0) -> dict:
    key = jax.random.key(seed)
    ks = jax.random.split(key, 32)
    f32 = jnp.float32
    n_pages = PAST_LEN // PAGE_SIZE
    n_phys = (5 * DEC_BATCH * n_pages + 3) // 4
    win_buf = min(WINDOW, PAST_LEN)

    def nrm(k, shape, scale=1.0):
        return jax.random.normal(k, shape, f32) * scale

    page_table = jax.random.permutation(ks[8], n_phys)[:DEC_BATCH * n_pages].reshape(DEC_BATCH, n_pages).astype(jnp.int32)
    return {
        'x_prompt': nrm(ks[0], (BATCH, SEQ, D_MODEL)),
        'x_sample': nrm(ks[1], (DEC_BATCH, DEC_SEQ, D_MODEL)),
        'c_prompt': nrm(ks[2], (BATCH, D_MODEL)),
        'c_sample': nrm(ks[3], (DEC_BATCH, D_MODEL)),
        'state_conv': nrm(ks[4], (N_CONV_LAYERS, DEC_BATCH, CONV_W - 1, D_MODEL)),
        'cache_cmp_kv': nrm(ks[5], (N_NSA_LAYERS, n_phys, PAGE_SIZE, 2, N_KV_HEADS, HEAD_DIM)),
        'cache_sel_kv': nrm(ks[6], (N_NSA_LAYERS, n_phys, PAGE_SIZE, 2, N_KV_HEADS, HEAD_DIM)),
        'cache_win_kv': nrm(ks[7], (N_NSA_LAYERS, DEC_BATCH, win_buf, 2, N_KV_HEADS, HEAD_DIM)),
        'page_table': page_table,
        'w_mod': nrm(ks[9], (DEPTH, D_MODEL, 6 * D_MODEL), 0.5 * D_MODEL ** -0.5),
        'b_mod': nrm(ks[10], (DEPTH, 6 * D_MODEL), 0.02),
        'norm_g': 1.0 + nrm(ks[11], (DEPTH, 2, D_MODEL), 0.05),
        'conv_w_in': nrm(ks[12], (N_CONV_LAYERS, D_MODEL, 3 * D_MODEL), D_MODEL ** -0.5),
        'conv_w': nrm(ks[13], (N_CONV_LAYERS, CONV_W, D_MODEL), CONV_W ** -0.5),
        'conv_w_out': nrm(ks[14], (N_CONV_LAYERS, D_MODEL, D_MODEL), D_MODEL ** -0.5),
        'nsa_w_in': nrm(ks[15], (N_NSA_LAYERS, D_MODEL, NSA_IN), D_MODEL ** -0.5),
        'nsa_w_out': nrm(ks[16], (N_NSA_LAYERS, Q_DIM, D_MODEL), Q_DIM ** -0.5),
        'q_norm_g': 1.0 + nrm(ks[17], (N_NSA_LAYERS, HEAD_DIM), 0.05),
        'k_norm_g': 1.0 + nrm(ks[18], (N_NSA_LAYERS, 3, HEAD_DIM), 0.05),
        'cmp_pe': nrm(ks[19], (N_NSA_LAYERS, 2, CMP_PARTS, CMP_STRIDE, HEAD_DIM), 0.02),
        'cmp_w1': nrm(ks[20], (N_NSA_LAYERS, 2, CMP_PARTS, CMP_STRIDE, HEAD_DIM, CMP_HIDDEN), (CMP_BLOCK * HEAD_DIM) ** -0.5),
        'cmp_b1': nrm(ks[21], (N_NSA_LAYERS, 2, CMP_HIDDEN), 0.02),
        'cmp_w2': nrm(ks[22], (N_NSA_LAYERS, 2, CMP_HIDDEN, HEAD_DIM), CMP_HIDDEN ** -0.5),
        'cmp_b2': nrm(ks[23], (N_NSA_LAYERS, 2, HEAD_DIM), 0.02),
        'router_w': nrm(ks[24], (DEPTH, D_MODEL, N_EXPERTS), D_MODEL ** -0.5),
        'router_b': nrm(ks[25], (DEPTH, N_EXPERTS), 0.01),
        'moe_w_gu': nrm(ks[26], (DEPTH, N_EXPERTS, D_MODEL, 2 * D_FF), D_MODEL ** -0.5),
        'moe_b_gu': nrm(ks[27], (DEPTH, N_EXPERTS, 2 * D_FF), 0.02),
        'moe_w_down': nrm(ks[28], (DEPTH, N_EXPERTS, D_FF, D_MODEL), D_FF ** -0.5),
        'moe_b_down': nrm(ks[29], (DEPTH, N_EXPERTS, D_MODEL), 0.02),
    }


def reference(x_prompt, x_sample, c_prompt, c_sample, state_conv, cache_cmp_kv, cache_sel_kv,
              cache_win_kv, page_table, w_mod, b_mod, norm_g, conv_w_in, conv_w, conv_w_out,
              nsa_w_in, nsa_w_out, q_norm_g, k_norm_g, cmp_pe, cmp_w1, cmp_b1, cmp_w2, cmp_b2,
              router_w, router_b, moe_w_gu, moe_b_gu, moe_w_down, moe_b_down):
    xp, xs = x_prompt, x_sample
    conv_p, conv_s, cmp_p, cmp_s, sel_p, sel_s, win_p, win_s = [], [], [], [], [], [], [], []
    for i in range(DEPTH):
        j = i // N_MIXERS
        sh1p, sc1p, g1p, sh2p, sc2p, g2p = adaln(c_prompt, w_mod[i], b_mod[i])
        sh1s, sc1s, g1s, sh2s, sc2s, g2s = adaln(c_sample, w_mod[i], b_mod[i])
        hp = modulate(xp, norm_g[i, 0], sh1p, sc1p)
        hs = modulate(xs, norm_g[i, 0], sh1s, sc1s)
        if i % N_MIXERS == 0:
            zero_prev = jnp.zeros((hp.shape[0], CONV_W - 1, D_MODEL), hp.dtype)
            yp, stp = conv_mixer(hp, zero_prev, conv_w_in[j], conv_w[j], conv_w_out[j])
            ys, sts = conv_mixer(hs, state_conv[j], conv_w_in[j], conv_w[j], conv_w_out[j])
            conv_p.append(stp)
            conv_s.append(sts)
        else:
            nsa_w = (nsa_w_in[j], nsa_w_out[j], q_norm_g[j], k_norm_g[j], cmp_pe[j],
                     cmp_w1[j], cmp_b1[j], cmp_w2[j], cmp_b2[j])
            yp, kcp, ksp, kwp = nsa_prompt(hp, *nsa_w)
            ys, kcs, kss, kws = nsa_sample(hs, cache_cmp_kv[j], cache_sel_kv[j], cache_win_kv[j], page_table, *nsa_w)
            cmp_p.append(kcp)
            cmp_s.append(kcs)
            sel_p.append(ksp)
            sel_s.append(kss)
            win_p.append(kwp)
            win_s.append(kws)
        xp = xp + g1p * yp
        xs = xs + g1s * ys
        moe_w = (router_w[i], router_b[i], moe_w_gu[i], moe_b_gu[i], moe_w_down[i], moe_b_down[i])
        xp = xp + g2p * moe(modulate(xp, norm_g[i, 1], sh2p, sc2p), *moe_w)
        xs = xs + g2s * moe(modulate(xs, norm_g[i, 1], sh2s, sc2s), *moe_w)
    return (xp, xs, jnp.stack(conv_p), jnp.stack(conv_s), jnp.stack(cmp_p), jnp.stack(cmp_s),
            jnp.stack(sel_p), jnp.stack(sel_s), jnp.stack(win_p), jnp.stack(win_s))
```

```python
import functools

import jax
import jax.numpy as jnp
from jax import lax
from jax.experimental import pallas as pl
from jax.experimental.pallas import tpu as pltpu

F32 = jnp.float32
BF16 = jnp.bfloat16
I32 = jnp.int32
U32 = jnp.uint32

N_HEADS = 32
N_KV_HEADS = 4
CMP_BLOCK = 32
CMP_STRIDE = 16
SEL_BLOCK = 64
N_SEL = 16
WINDOW = 512
PAGE_SIZE = 128
TOP_K = 4
SWIGLU_LIMIT = 7.0
SWIGLU_ALPHA = 1.702
EPS = 1e-6
FORCE_SCORE = 1e4
MASKED = -1e30

LANES = 128
SUBLANES = 8
VMEM_PHYSICAL_BYTES = 64 * 1024 * 1024
VMEM_CAP_BYTES = VMEM_PHYSICAL_BYTES - 6 * 1024 * 1024


def _vmem_limit(block_bytes):
    return int(min(VMEM_CAP_BYTES, block_bytes * 5 // 4 + (4 << 20)))


def _params(sem, block_bytes):
    return pltpu.CompilerParams(dimension_semantics=sem, vmem_limit_bytes=_vmem_limit(block_bytes))


def _tile(n, pref):
    if n <= pref:
        return n
    t = pref
    while n % t:
        t //= 2
    return t


def _dot(a, b):
    return jnp.dot(a, b, preferred_element_type=F32)


def _dot_nt(a, b):
    return lax.dot_general(a, b, (((1,), (1,)), ((), ())), preferred_element_type=F32)


def _split3(x):
    hi = x.astype(BF16)
    r = x - hi.astype(F32)
    mid = r.astype(BF16)
    lo = (r - mid.astype(F32)).astype(BF16)
    return hi, mid, lo


def _iota(shape, dim):
    return lax.broadcasted_iota(I32, shape, dim)


def _rms(a):
    return a * lax.rsqrt(jnp.mean(a * a, axis=-1, keepdims=True) + EPS)


def _adaln_kernel(c_ref, w_ref, b_ref, o_ref, *, kc):
    c = c_ref[...]
    a = (c * jax.nn.sigmoid(c)).astype(BF16)
    acc = jnp.zeros(o_ref.shape, F32)
    for k0 in range(0, a.shape[1], kc):
        acc = acc + _dot(a[:, k0:k0 + kc], w_ref[k0:k0 + kc, :].astype(BF16))
    o_ref[...] = acc + b_ref[...]


def _adaln(c_all, w_mod, b_mod):
    n_layers, d, n6 = w_mod.shape
    r = c_all.shape[0]
    tn = _tile(n6, 1024)
    kc = _tile(d, 1024)
    blk = 2 * d * tn * 4 + d * tn * 2 + r * d * 4
    return pl.pallas_call(
        functools.partial(_adaln_kernel, kc=kc),
        out_shape=jax.ShapeDtypeStruct((n_layers, r, n6), F32),
        grid=(n_layers, n6 // tn),
        in_specs=[
            pl.BlockSpec((r, d), lambda l, j: (0, 0)),
            pl.BlockSpec((None, d, tn), lambda l, j: (l, 0, j)),
            pl.BlockSpec((None, 1, tn), lambda l, j: (l, 0, j)),
        ],
        out_specs=pl.BlockSpec((None, r, tn), lambda l, j: (l, 0, j)),
        compiler_params=_params(("arbitrary", "arbitrary"), blk),
        name="adaln_mod",
    )(c_all, w_mod, b_mod.reshape(n_layers, 1, n6))


def _modulated(x_ref, g_ref, sc_ref, sh_ref):
    return _rms(x_ref[...]) * g_ref[...] * (1.0 + sc_ref[...]) + sh_ref[...]


def _norm_mod_kernel(x_ref, g_ref, sc_ref, sh_ref, h_ref):
    h_ref[...] = _modulated(x_ref, g_ref, sc_ref, sh_ref).astype(h_ref.dtype)


def _mod_spec(mod, tm, d):
    if mod.shape[1] == 1:
        return pl.BlockSpec((None, 1, d), lambda b, i: (b, 0, 0))
    return pl.BlockSpec((None, tm, d), lambda b, i: (b, i, 0))


def _norm_mod(x, g, scale, shift):
    b, t, d = x.shape
    tm = _tile(t, 512)
    blk = 2 * tm * d * (4 + 2) + 6 * d * 4 + 2 * tm * d * 4
    return pl.pallas_call(
        _norm_mod_kernel,
        out_shape=jax.ShapeDtypeStruct((b, t, d), BF16),
        grid=(b, t // tm),
        in_specs=[
            pl.BlockSpec((None, tm, d), lambda b_, i: (b_, i, 0)),
            pl.BlockSpec((1, d), lambda b_, i: (0, 0)),
            _mod_spec(scale, tm, d),
            _mod_spec(shift, tm, d),
        ],
        out_specs=pl.BlockSpec((None, tm, d), lambda b_, i: (b_, i, 0)),
        compiler_params=_params(("arbitrary", "arbitrary"), blk),
        name="norm_mod",
    )(x, g.reshape(1, d), scale, shift)


def _norm_router_kernel(x_ref, g_ref, sc_ref, sh_ref, wr_ref, br_ref, hp_ref, ti_ref, gt_ref):
    h = _modulated(x_ref, g_ref, sc_ref, sh_ref)
    tm, d = h.shape
    dh = d // 2
    lo = pltpu.bitcast(h[:, :dh].astype(BF16).astype(F32), U32)
    hi = pltpu.bitcast(h[:, dh:].astype(BF16).astype(F32), U32)
    hp_ref[...] = (lo >> 16) | hi

    h1, h2, h3 = _split3(h)
    w1, w2, w3 = _split3(wr_ref[...])
    logits = (_dot(h1, w1) + (_dot(h1, w2) + _dot(h2, w1))
              + (_dot(h2, w2) + _dot(h1, w3) + _dot(h3, w1))) + br_ref[...]
    n_exp = logits.shape[1]
    lane = _iota(logits.shape, 1).astype(F32)
    work = logits
    vals, idxs = [], []
    for _ in range(TOP_K):
        m = jnp.max(work, axis=-1, keepdims=True)
        idx = jnp.min(jnp.where(work == m, lane, float(n_exp)), axis=-1, keepdims=True)
        vals.append(m)
        idxs.append(idx)
        work = jnp.where(lane == idx, -jnp.inf, work)
    es = [jnp.exp(v - vals[0]) for v in vals]
    den = es[0]
    for e in es[1:]:
        den = den + e
    lane_o = _iota((tm, LANES), 1)
    ti = jnp.zeros((tm, LANES), F32)
    gt = jnp.zeros((tm, LANES), F32)
    for k in range(TOP_K):
        ti = jnp.where(lane_o == k, idxs[k], ti)
        gt = jnp.where(lane_o == k, es[k] / den, gt)
    ti_ref[...] = ti.astype(I32)
    gt_ref[...] = gt


def _norm_router(x, g, scale, shift, w_router, b_router):
    b, t, d = x.shape
    n_exp = w_router.shape[1]
    tm = _tile(t, 256)
    blk = 2 * tm * d * 4 + 2 * tm * d * 2 + 8 * tm * d * 4 + 2 * d * n_exp * 4
    row = lambda b_, i: (b_, i, 0)
    return pl.pallas_call(
        _norm_router_kernel,
        out_shape=(jax.ShapeDtypeStruct((b, t, d // 2), U32),
                   jax.ShapeDtypeStruct((b, t, LANES), I32),
                   jax.ShapeDtypeStruct((b, t, LANES), F32)),
        grid=(b, t // tm),
        in_specs=[
            pl.BlockSpec((None, tm, d), row),
            pl.BlockSpec((1, d), lambda b_, i: (0, 0)),
            _mod_spec(scale, tm, d),
            _mod_spec(shift, tm, d),
            pl.BlockSpec((d, n_exp), lambda b_, i: (0, 0)),
            pl.BlockSpec((1, n_exp), lambda b_, i: (0, 0)),
        ],
        out_specs=(pl.BlockSpec((None, tm, d // 2), row),
                   pl.BlockSpec((None, tm, LANES), row),
                   pl.BlockSpec((None, tm, LANES), row)),
        compiler_params=_params(("arbitrary", "arbitrary"), blk),
        name="norm_router",
    )(x, g.reshape(1, d), scale, shift, w_router, b_router.reshape(1, n_exp))


def _mm_kernel(a_ref, w_ref, *rest, epi, head_dim, scale):
    *ins, o_ref, wb = rest

    @pl.when(pl.program_id(1) == 0)
    def _():
        wb[...] = w_ref[...].astype(BF16)

    acc = _dot(a_ref[...], wb[...])
    tn = acc.shape[1]
    if epi == "res":
        x_ref, g_ref = ins
        o_ref[...] = x_ref[...] + g_ref[...] * acc
    elif epi == "qnorm":
        (gq_ref,) = ins
        for c in range(tn // head_dim):
            a = acc[:, c * head_dim:(c + 1) * head_dim]
            o_ref[:, c * head_dim:(c + 1) * head_dim] = (_rms(a) * gq_ref[...] * scale).astype(o_ref.dtype)
    elif epi == "kvnorm":
        fl_ref, gk_ref = ins
        for c in range(tn // head_dim):
            sl = slice(c * head_dim, (c + 1) * head_dim)
            a = acc[:, sl]
            o_ref[:, sl] = jnp.where(fl_ref[:, sl] > 0.5, _rms(a) * gk_ref[:, sl], a)
    elif epi == "sigmoid":
        o_ref[...] = jax.nn.sigmoid(acc)
    else:
        raise ValueError(epi)


def _mm(a, w, *, col0, n, epi, extra=(), rows_per_batch=None, out_dtype=F32, head_dim=LANES, scale=1.0,
        name="mm"):
    m, k = a.shape
    tm = _tile(m, 512)
    tn = _tile(n, 512)
    assert col0 % tn == 0
    j0 = col0 // tn
    in_specs = [pl.BlockSpec((tm, k), lambda j, i: (i, 0)),
                pl.BlockSpec((k, tn), lambda j, i: (0, j + j0))]
    operands = [a, w]
    if epi == "res":
        x, g = extra
        in_specs.append(pl.BlockSpec((tm, tn), lambda j, i: (i, j)))
        if g.ndim == 3:
            assert rows_per_batch % tm == 0
            in_specs.append(pl.BlockSpec((None, 1, tn), lambda j, i: (i * tm // rows_per_batch, 0, j)))
        else:
            in_specs.append(pl.BlockSpec((tm, tn), lambda j, i: (i, j)))
        operands += [x, g]
    elif epi == "qnorm":
        in_specs.append(pl.BlockSpec((1, head_dim), lambda j, i: (0, 0)))
        operands += list(extra)
    elif epi == "kvnorm":
        in_specs += [pl.BlockSpec((1, tn), lambda j, i: (0, j))] * 2
        operands += list(extra)
    blk = 2 * tm * k * 2 + 2 * k * tn * 4 + k * tn * 2 + 6 * tm * tn * 4
    return pl.pallas_call(
        functools.partial(_mm_kernel, epi=epi, head_dim=head_dim, scale=scale),
        out_shape=jax.ShapeDtypeStruct((m, n), out_dtype),
        grid=(n // tn, m // tm),
        in_specs=in_specs,
        out_specs=pl.BlockSpec((tm, tn), lambda j, i: (i, j)),
        scratch_shapes=[pltpu.VMEM((k, tn), BF16)],
        compiler_params=_params(("arbitrary", "arbitrary"), blk),
        name=name,
    )(*operands)


def _conv_in_kernel(a_ref, wb_ref, wc_ref, wu_ref, cw_ref, *rest, tiles_per_batch, seq, per_token_prev):
    if per_token_prev:
        p1_ref, p2_ref, o_ref, v_ref, wbuf, ext = rest
    else:
        o_ref, st_ref, wbuf, ext = rest
    i = pl.program_id(1)

    @pl.when(i == 0)
    def _():
        wbuf[0] = wb_ref[...].astype(BF16)
        wbuf[1] = wc_ref[...].astype(BF16)
        wbuf[2] = wu_ref[...].astype(BF16)

    a = a_ref[...]
    b_gate = _dot(a, wbuf[0])
    v = _dot(a, wbuf[1]) * _dot(a, wbuf[2])
    tm, tn = v.shape

    if per_token_prev:
        ext[0:SUBLANES, :] = jnp.zeros((SUBLANES, tn), F32)
    else:
        @pl.when(i % tiles_per_batch == 0)
        def _():
            ext[0:SUBLANES, :] = jnp.zeros((SUBLANES, tn), F32)

    ext[SUBLANES:SUBLANES + tm, :] = v
    s1 = ext[SUBLANES - 1:SUBLANES - 1 + tm, :]
    s2 = ext[SUBLANES - 2:SUBLANES - 2 + tm, :]
    if per_token_prev:
        tpos = _iota((tm, 1), 0) % seq
        s1 = jnp.where(tpos >= 1, s1, p1_ref[...])
        s2 = jnp.where(tpos >= 2, s2, p2_ref[...])
    cw = cw_ref[...]
    conv = s2 * cw[0:1, :] + s1 * cw[1:2, :] + v * cw[2:3, :]
    o_ref[...] = (b_gate * conv).astype(o_ref.dtype)

    if per_token_prev:
        v_ref[...] = v
    else:
        ext[0:SUBLANES, :] = ext[tm:tm + SUBLANES, :]

        @pl.when(i % tiles_per_batch == tiles_per_batch - 1)
        def _():
            st_ref[...] = ext[tm + SUBLANES - 2:tm + SUBLANES, :]


def _conv_in(h, w_in, conv_w, *, batch, seq, prev=None):
    m, d = h.shape
    tn = _tile(d, 256)
    nd = d // tn
    per_token_prev = prev is not None
    tm = m if per_token_prev else _tile(seq, 512)
    tiles_per_batch = max(seq // tm, 1)
    in_specs = [pl.BlockSpec((tm, d), lambda j, i: (i, 0)),
                pl.BlockSpec((d, tn), lambda j, i: (0, j)),
                pl.BlockSpec((d, tn), lambda j, i: (0, j + nd)),
                pl.BlockSpec((d, tn), lambda j, i: (0, j + 2 * nd)),
                pl.BlockSpec((3, tn), lambda j, i: (0, j))]
    operands = [h, w_in, w_in, w_in, conv_w]
    tile_spec = pl.BlockSpec((tm, tn), lambda j, i: (i, j))
    if per_token_prev:
        in_specs += [tile_spec, tile_spec]
        operands += list(prev)
        out_shape = (jax.ShapeDtypeStruct((m, d), BF16), jax.ShapeDtypeStruct((m, d), F32))
        out_specs = (tile_spec, tile_spec)
    else:
        out_shape = (jax.ShapeDtypeStruct((m, d), BF16), jax.ShapeDtypeStruct((batch, 2, d), F32))
        out_specs = (tile_spec, pl.BlockSpec((None, 2, tn), lambda j, i: (i // tiles_per_batch, 0, j)))
    blk = 2 * tm * d * 2 + 6 * d * tn * 4 + 3 * d * tn * 2 + 10 * tm * tn * 4
    return pl.pallas_call(
        functools.partial(_conv_in_kernel, tiles_per_batch=tiles_per_batch, seq=seq,
                          per_token_prev=per_token_prev),
        out_shape=out_shape,
        grid=(nd, m // tm),
        in_specs=in_specs,
        out_specs=out_specs,
        scratch_shapes=[pltpu.VMEM((3, d, tn), BF16), pltpu.VMEM((tm + 2 * SUBLANES, tn), F32)],
        compiler_params=_params(("arbitrary", "arbitrary"), blk),
        name="conv_in",
    )(*operands)


def _conv_prev_rows(state, seq):
    b, _, d = state.shape
    zeros = jnp.zeros((b, seq, d), state.dtype)
    p1 = zeros.at[:, 0].set(state[:, 1])
    p2 = zeros.at[:, 0].set(state[:, 0]).at[:, 1].set(state[:, 1])
    return p1.reshape(b * seq, d), p2.reshape(b * seq, d)


def _plan_kernel(ti_ref, pos_ref, cnt_ref, carry):
    @pl.when(pl.program_id(0) == 0)
    def _():
        carry[...] = jnp.zeros(carry.shape, F32)

    ti = ti_ref[...]
    tm = ti.shape[0]
    e_iota = _iota((tm, LANES), 1)
    onehots = [jnp.where(ti[:, k:k + 1] == e_iota, 1.0, 0.0) for k in range(TOP_K)]
    hits = onehots[0]
    for oh in onehots[1:]:
        hits = hits + oh
    strictly_lower = jnp.where(_iota((tm, tm), 0) > _iota((tm, tm), 1), 1.0, 0.0).astype(BF16)
    before = _dot(strictly_lower, hits.astype(BF16)) + carry[...]
    out = jnp.zeros((tm, LANES), F32)
    for k in range(TOP_K):
        out = jnp.where(e_iota == k, jnp.sum(onehots[k] * before, axis=-1, keepdims=True), out)
    pos_ref[...] = out.astype(I32)
    carry[...] = carry[...] + jnp.sum(hits, axis=0, keepdims=True)
    cnt_ref[...] = carry[...]


def _plan(topi):
    n = topi.shape[0]
    tm = _tile(n, 256)
    return pl.pallas_call(
        _plan_kernel,
        out_shape=(jax.ShapeDtypeStruct((n, LANES), I32), jax.ShapeDtypeStruct((1, LANES), F32)),
        grid=(n // tm,),
        in_specs=[pl.BlockSpec((tm, LANES), lambda i: (i, 0))],
        out_specs=(pl.BlockSpec((tm, LANES), lambda i: (i, 0)), pl.BlockSpec((1, LANES), lambda i: (0, 0))),
        scratch_shapes=[pltpu.VMEM((1, LANES), F32)],
        compiler_params=_params(("arbitrary",), 8 * tm * LANES * 4 + tm * tm * 8),
        name="moe_plan",
    )(topi)


def _dispatch_tables(topi, n_exp, sub, rmax):
    n = topi.shape[0]
    ids = topi[:, :TOP_K]
    n_pairs = n * TOP_K
    n_pad = -(-n // 256) * 256
    pos, counts = _plan(jnp.pad(topi, ((0, n_pad - n), (0, 0)), constant_values=-1))
    counts = counts[0, :n_exp].astype(I32)
    padded = (counts + sub - 1) // sub * sub
    pad_start = jnp.cumsum(padded) - padded
    dest = pad_start[ids] + pos[:n, :TOP_K]
    n_rows = -(-(n_pairs + n_exp * (sub - 1)) // sub) * sub
    row_tok = jnp.zeros((n_rows,), I32).at[dest.reshape(-1)].set(jnp.arange(n_pairs, dtype=I32) // TOP_K)
    n_sb = (counts + rmax - 1) // rmax
    cum = jnp.cumsum(n_sb)
    total = cum[-1]
    n_sb_max = n_exp + n_pairs // rmax
    s = jnp.arange(n_sb_max, dtype=I32)
    e_of = jnp.minimum(jnp.searchsorted(cum, s, side="right"), n_exp - 1).astype(I32)
    local = s - (cum - n_sb)[e_of]
    active = s < total
    rows = jnp.where(active, jnp.minimum(counts[e_of] - local * rmax, rmax), 0).astype(I32)
    start = jnp.where(active, pad_start[e_of] + local * rmax, 0).astype(I32)
    sb_exp = jnp.where(active, e_of, e_of[jnp.maximum(total - 1, 0)]).astype(I32)
    n_used = jnp.sum(padded).astype(I32).reshape(1)
    return dest.astype(I32), row_tok, sb_exp, start, rows, n_used, n_rows, n_sb_max


def _expert_kernel(exp_ref, start_ref, rows_ref, tok_ref, used_ref,
                   h_hbm, wg_ref, wu_ref, bg_ref, bu_ref, wd_ref, bd_ref, ys_hbm,
                   xbuf, act, wgu, wdb, ostage, gsem, osem, *, n1, sub, tf, dt):
    sb = pl.program_id(0)
    s = pl.program_id(1)
    rows = rows_ref[sb]
    start = start_ref[sb]
    n_sub = (rows + sub - 1) // sub
    dh = xbuf.shape[1]

    @pl.when(jnp.logical_and(sb == 0, s == 0))
    def _():
        ostage[0] = jnp.zeros(ostage.shape[1:], F32)
        n_tail = (ys_hbm.shape[0] - used_ref[0]) // sub

        def tail_copy(t, col):
            row0 = pl.multiple_of(used_ref[0] + t * sub, sub)
            return pltpu.make_async_copy(ostage.at[0], ys_hbm.at[pl.ds(row0, sub), pl.ds(col * dt, dt)],
                                         osem.at[0])

        def issue(t, c):
            for col in range(ys_hbm.shape[1] // dt):
                tail_copy(t, col).start()
            return c

        def drain(t, c):
            for col in range(ys_hbm.shape[1] // dt):
                tail_copy(t, col).wait()
            return c

        lax.fori_loop(0, n_tail, issue, 0)
        lax.fori_loop(0, n_tail, drain, 0)

    def gather_copy(r):
        tok = tok_ref[start + r]
        return pltpu.make_async_copy(h_hbm.at[pl.ds(tok, 1)], xbuf.at[pl.ds(r, 1)], gsem)

    @pl.when(jnp.logical_and(s == 0, rows > 0))
    def _():
        def issue(r, c):
            gather_copy(r).start()
            return c

        def drain(r, c):
            gather_copy(r).wait()
            return c

        lax.fori_loop(0, n_sub * sub, issue, 0)
        lax.fori_loop(0, n_sub * sub, drain, 0)

    def unpack(i):
        words = xbuf[pl.ds(pl.multiple_of(i * sub, sub), sub), :]
        lo = pltpu.bitcast(words << 16, F32).astype(BF16)
        hi = pltpu.bitcast(words & jnp.uint32(0xFFFF0000), F32).astype(BF16)
        return lo, hi

    @pl.when(jnp.logical_and(s < n1, rows > 0))
    def _():
        wgu[0] = wg_ref[...].astype(BF16)
        wgu[1] = wu_ref[...].astype(BF16)

        def sub_block(i, c):
            lo, hi = unpack(i)
            g = _dot(lo, wgu[0, 0:dh, :]) + _dot(hi, wgu[0, dh:2 * dh, :]) + bg_ref[...]
            u = _dot(lo, wgu[1, 0:dh, :]) + _dot(hi, wgu[1, dh:2 * dh, :]) + bu_ref[...]
            g = jnp.minimum(g, SWIGLU_LIMIT)
            u = jnp.clip(u, -SWIGLU_LIMIT, SWIGLU_LIMIT)
            a = g * jax.nn.sigmoid(SWIGLU_ALPHA * g) * (u + 1.0)
            act[s, pl.ds(pl.multiple_of(i * sub, sub), sub), :] = a.astype(BF16)
            return c

        lax.fori_loop(0, n_sub, sub_block, 0)

    @pl.when(jnp.logical_and(s >= n1, rows > 0))
    def _():
        wdb[...] = wd_ref[...].astype(BF16)
        col = pl.multiple_of((s - n1) * dt, dt)

        def out_copy(i, slot):
            row0 = pl.multiple_of(start + i * sub, sub)
            return pltpu.make_async_copy(ostage.at[slot], ys_hbm.at[pl.ds(row0, sub), pl.ds(col, dt)],
                                         osem.at[slot])

        def sub_block(i, c):
            slot = i % 2

            @pl.when(i >= 2)
            def _():
                out_copy(i - 2, slot).wait()

            r0 = pl.multiple_of(i * sub, sub)
            y = bd_ref[...] + _dot(act[0, pl.ds(r0, sub), :], wdb[0:tf, :])
            for f in range(1, n1):
                y = y + _dot(act[f, pl.ds(r0, sub), :], wdb[f * tf:(f + 1) * tf, :])
            ostage[slot] = y
            out_copy(i, slot).start()
            return c

        lax.fori_loop(0, n_sub, sub_block, 0)

        @pl.when(n_sub >= 2)
        def _():
            out_copy(n_sub - 2, n_sub % 2).wait()

        out_copy(n_sub - 1, (n_sub - 1) % 2).wait()


def _experts(h_packed, tables, w_gu, b_gu, w_down, b_down, *, sub, rmax):
    _, row_tok, sb_exp, sb_start, sb_rows, n_used, n_rows, n_sb_max = tables
    n_exp, d, f2 = w_gu.shape
    f = f2 // 2
    tf = _tile(f, 256)
    dt = _tile(d, 512)
    n1, n2 = f // tf, d // dt

    def f_idx(sb, s, rows_ref):
        return jnp.where(rows_ref[sb] > 0, jnp.minimum(s, n1 - 1), n1 - 1)

    def d_idx(sb, s, rows_ref):
        return jnp.where(rows_ref[sb] > 0, jnp.maximum(s - n1, 0), n2 - 1)

    in_specs = [
        pl.BlockSpec(memory_space=pl.ANY),
        pl.BlockSpec((None, d, tf), lambda sb, s, e, st, rw, tk, us:(e[sb], 0, f_idx(sb, s, rw))),
        pl.BlockSpec((None, d, tf), lambda sb, s, e, st, rw, tk, us:(e[sb], 0, n1 + f_idx(sb, s, rw))),
        pl.BlockSpec((None, 1, tf), lambda sb, s, e, st, rw, tk, us:(e[sb], 0, f_idx(sb, s, rw))),
        pl.BlockSpec((None, 1, tf), lambda sb, s, e, st, rw, tk, us:(e[sb], 0, n1 + f_idx(sb, s, rw))),
        pl.BlockSpec((None, f, dt), lambda sb, s, e, st, rw, tk, us:(e[sb], 0, d_idx(sb, s, rw))),
        pl.BlockSpec((None, 1, dt), lambda sb, s, e, st, rw, tk, us:(e[sb], 0, d_idx(sb, s, rw))),
    ]
    blk = (rmax * (d // 2) * 4 + rmax * f * 2 + 4 * d * tf * 4 + 2 * f * dt * 4 + 2 * d * tf * 2
           + f * dt * 2 + 2 * sub * dt * 4 + 4 * sub * d * 2)
    return pl.pallas_call(
        functools.partial(_expert_kernel, n1=n1, sub=sub, tf=tf, dt=dt),
        out_shape=jax.ShapeDtypeStruct((n_rows, d), F32),
        grid_spec=pltpu.PrefetchScalarGridSpec(
            num_scalar_prefetch=5,
            grid=(n_sb_max, n1 + n2),
            in_specs=in_specs,
            out_specs=pl.BlockSpec(memory_space=pl.ANY),
            scratch_shapes=[
                pltpu.VMEM((rmax, d // 2), U32),
                pltpu.VMEM((n1, rmax, tf), BF16),
                pltpu.VMEM((2, d, tf), BF16),
                pltpu.VMEM((f, dt), BF16),
                pltpu.VMEM((2, sub, dt), F32),
                pltpu.SemaphoreType.DMA(()),
                pltpu.SemaphoreType.DMA((2,)),
            ]),
        compiler_params=_params(("arbitrary", "arbitrary"), blk),
        name="moe_experts",
    )(sb_exp, sb_start, sb_rows, row_tok, n_used, h_packed, w_gu, w_gu,
      b_gu.reshape(n_exp, 1, f2), b_gu.reshape(n_exp, 1, f2), w_down, b_down.reshape(n_exp, 1, d))


def _combine_kernel(dest_ref, ys_hbm, x_ref, g_ref, gate_ref, o_ref, buf, sem, *, tiles_per_batch):
    tm = x_ref.shape[0]
    tok0 = (pl.program_id(0) * tiles_per_batch + pl.program_id(1)) * tm

    def row_copy(r, k):
        src = dest_ref[(tok0 + r) * TOP_K + k]
        return pltpu.make_async_copy(ys_hbm.at[pl.ds(src, 1)], buf.at[k, pl.ds(r, 1)], sem)

    def issue(r, c):
        for k in range(TOP_K):
            row_copy(r, k).start()
        return c

    def drain(r, c):
        for k in range(TOP_K):
            row_copy(r, k).wait()
        return c

    lax.fori_loop(0, tm, issue, 0)
    lax.fori_loop(0, tm, drain, 0)
    gate = gate_ref[...]
    y = gate[:, 0:1] * buf[0]
    for k in range(1, TOP_K):
        y = y + gate[:, k:k + 1] * buf[k]
    o_ref[...] = x_ref[...] + g_ref[...] * y


def _combine(ys, dest, x, g, gate):
    b, t, d = x.shape
    tm = _tile(t, 128)
    tiles_per_batch = t // tm
    row = lambda b_, i, dref: (b_, i, 0)
    if g.shape[1] == 1:
        g_spec = pl.BlockSpec((None, 1, d), lambda b_, i, dref: (b_, 0, 0))
    else:
        g_spec = pl.BlockSpec((None, tm, d), row)
    blk = TOP_K * tm * d * 4 + 6 * tm * d * 4
    return pl.pallas_call(
        functools.partial(_combine_kernel, tiles_per_batch=tiles_per_batch),
        out_shape=jax.ShapeDtypeStruct((b, t, d), F32),
        grid_spec=pltpu.PrefetchScalarGridSpec(
            num_scalar_prefetch=1,
            grid=(b, tiles_per_batch),
            in_specs=[pl.BlockSpec(memory_space=pl.ANY),
                      pl.BlockSpec((None, tm, d), row),
                      g_spec,
                      pl.BlockSpec((None, tm, LANES), row)],
            out_specs=pl.BlockSpec((None, tm, d), row),
            scratch_shapes=[pltpu.VMEM((TOP_K, tm, d), F32), pltpu.SemaphoreType.DMA(())]),
        compiler_params=_params(("arbitrary", "arbitrary"), blk),
        name="moe_combine",
    )(dest.reshape(-1), ys, x, g, gate)


def _moe(xp, xs, g, mods_p, mods_s, w_router, b_router, w_gu, b_gu, w_down, b_down):
    (sc_p, sh_p, g_p), (sc_s, sh_s, g_s) = mods_p, mods_s
    d = xp.shape[-1]
    n_exp = w_router.shape[1]
    f = w_down.shape[1]
    hp, ti_p, gt_p = _norm_router(xp, g, sc_p, sh_p, w_router, b_router)
    hs, ti_s, gt_s = _norm_router(xs, g, sc_s, sh_s, w_router, b_router)
    n_p = xp.shape[0] * xp.shape[1]
    h_all = jnp.concatenate([hp.reshape(n_p, d // 2), hs.reshape(-1, d // 2)], axis=0)
    topi = jnp.concatenate([ti_p.reshape(n_p, LANES), ti_s.reshape(-1, LANES)], axis=0)
    sub = _tile(f, 256)
    rmax = 6 * sub
    tables = _dispatch_tables(topi, n_exp, sub, rmax)
    ys = _experts(h_all, tables, w_gu, b_gu, w_down, b_down, sub=sub, rmax=rmax)
    dest = tables[0]
    xp = _combine(ys, dest[:n_p], xp, g_p, gt_p)
    xs = _combine(ys, dest[n_p:], xs, g_s, gt_s)
    return xp, xs


def _nsa_project(h, w_in, q_g, k_g):
    d = h.shape[1]
    hd = LANES
    kvh = N_KV_HEADS
    grp = N_HEADS // kvh
    q_dim = N_HEADS * hd
    kv_dim = 2 * kvh * hd
    q = _mm(h, w_in, col0=0, n=q_dim, epi="qnorm", extra=(q_g.reshape(1, hd),), out_dtype=BF16,
            head_dim=hd, scale=hd ** -0.5, name="nsa_q")
    ones = jnp.ones((kvh * hd,), F32)
    zeros = jnp.zeros((kvh * hd,), F32)
    flags = jnp.concatenate([zeros, zeros, ones, zeros, ones, zeros]).reshape(1, 3 * kv_dim)
    gains = jnp.concatenate([ones, ones, jnp.tile(k_g[1], kvh), ones, jnp.tile(k_g[2], kvh), ones])
    kv = _mm(h, w_in, col0=q_dim, n=3 * kv_dim, epi="kvnorm", extra=(flags, gains.reshape(1, 3 * kv_dim)),
             head_dim=hd, name="nsa_kv")
    w_gate = w_in[:, q_dim + 3 * kv_dim:].reshape(d, 3, kvh, grp)
    w_gate = jnp.transpose(w_gate, (0, 2, 1, 3)).reshape(d, kvh, 3 * grp)
    w_gate = jnp.pad(w_gate, ((0, 0), (0, 0), (0, LANES - 3 * grp))).reshape(d, kvh * LANES)
    gates = _mm(h, w_gate, col0=0, n=kvh * LANES, epi="sigmoid", name="nsa_gates")
    return q, kv, gates


def _cmp_part_kernel(pt_ref, *refs, n_pages, kvh):
    pages = refs[:n_pages * 2 * kvh]
    w_ref, pe_ref, p_out, pe_out, wbf = refs[n_pages * 2 * kvh:]

    @pl.when(pl.program_id(0) == 0)
    def _():
        for j in range(2):
            wbf[j] = w_ref[j].astype(BF16)
            pe_out[j] = _dot(pe_ref[j].astype(BF16), wbf[j])

    rows = n_pages * PAGE_SIZE // CMP_STRIDE
    for j in range(2):
        per_head = []
        for h in range(kvh):
            head_pages = pages[j * kvh + h::2 * kvh]
            pieces = []
            for s in range(CMP_STRIDE):
                pieces.append(jnp.concatenate(
                    [pg[pl.ds(s, PAGE_SIZE // CMP_STRIDE, stride=CMP_STRIDE), :] for pg in head_pages], axis=0))
            per_head.append(jnp.concatenate(pieces, axis=1))
        x = jnp.concatenate(per_head, axis=0).astype(BF16)
        y = _dot(x, wbf[j])
        for h in range(kvh):
            p_out[j, h] = y[h * rows:(h + 1) * rows, :]


def _cmp_out_kernel(pk_ref, pv_ref, b1_ref, pe_ref, w2_ref, b2_ref, kg_ref, kc_ref, vc_ref, shifted):
    n_ch = pk_ref.shape[0]
    hid_dim = pk_ref.shape[1] // 2
    for j, (p_ref, o_ref) in enumerate(((pk_ref, kc_ref), (pv_ref, vc_ref))):
        shifted[0:n_ch, :] = p_ref[:, hid_dim:2 * hid_dim]
        shifted[n_ch:n_ch + SUBLANES, :] = jnp.zeros((SUBLANES, hid_dim), F32)
        hid = (b1_ref[j:j + 1, :] + pe_ref[j, 0:1, 0:hid_dim] + pe_ref[j, 1:2, hid_dim:2 * hid_dim]
               + p_ref[:, 0:hid_dim] + shifted[1:1 + n_ch, :])
        a = hid * jax.nn.sigmoid(hid)
        out = _dot(a.astype(BF16), w2_ref[j].astype(BF16)) + b2_ref[j:j + 1, :]
        if j == 0:
            out = _rms(out) * kg_ref[...]
        o_ref[...] = out.astype(o_ref.dtype)


def _compress(src_pages, page_table, batch, w1, b1, w2, b2, pe, kg):
    kvh = N_KV_HEADS
    hd = LANES
    n_log = page_table.shape[0]
    n_pages = _tile(n_log, 8)
    chunks_per_page = PAGE_SIZE // CMP_STRIDE
    n_chunks = n_log * chunks_per_page
    hid2 = w1.shape[1] * w1.shape[-1]
    kdim = CMP_STRIDE * hd
    w1cat = jnp.transpose(w1, (0, 2, 3, 1, 4)).reshape(2, kdim, hid2)
    pe_rows = jnp.pad(pe.reshape(2, -1, kdim), ((0, 0), (0, SUBLANES - pe.shape[1]), (0, 0)))

    def page_spec(r, cb):
        return pl.BlockSpec((None, PAGE_SIZE, hd), lambda i, pt: (pt[i * n_pages + r], 0, cb))

    page_specs = [page_spec(r, cb) for r in range(n_pages) for cb in range(2 * kvh)]
    rows = n_pages * chunks_per_page
    blk = (2 * n_pages * PAGE_SIZE * 2 * kvh * hd * 4 + 3 * 2 * kdim * hid2 * 4
           + 4 * kvh * rows * (kdim + hid2) * 4)
    parts, pe_out = pl.pallas_call(
        functools.partial(_cmp_part_kernel, n_pages=n_pages, kvh=kvh),
        out_shape=(jax.ShapeDtypeStruct((2, kvh, n_chunks, hid2), F32),
                   jax.ShapeDtypeStruct((2, SUBLANES, hid2), F32)),
        grid_spec=pltpu.PrefetchScalarGridSpec(
            num_scalar_prefetch=1,
            grid=(n_log // n_pages,),
            in_specs=page_specs + [
                pl.BlockSpec((2, kdim, hid2), lambda i, pt: (0, 0, 0)),
                pl.BlockSpec((2, SUBLANES, kdim), lambda i, pt: (0, 0, 0))],
            out_specs=(pl.BlockSpec((2, kvh, rows, hid2), lambda i, pt: (0, 0, i, 0)),
                       pl.BlockSpec((2, SUBLANES, hid2), lambda i, pt: (0, 0, 0))),
            scratch_shapes=[pltpu.VMEM((2, kdim, hid2), BF16)]),
        compiler_params=_params(("arbitrary",), blk),
        name="cmp_part",
    )(page_table, *([src_pages] * len(page_specs)), w1cat, pe_rows)

    n_ch = n_chunks // batch
    hid = hid2 // 2
    part_spec = lambda j: pl.BlockSpec((None, None, n_ch, hid2), lambda b, h: (j, h, b, 0))
    full = lambda *shape: pl.BlockSpec(shape, lambda b, h: (0,) * len(shape))
    out_spec = pl.BlockSpec((None, None, n_ch, hd), lambda b, h: (b, h, 0, 0))
    return pl.pallas_call(
        _cmp_out_kernel,
        out_shape=(jax.ShapeDtypeStruct((batch, kvh, n_ch, hd), BF16),) * 2,
        grid=(batch, kvh),
        in_specs=[part_spec(0), part_spec(1), full(2, hid), full(2, SUBLANES, hid2), full(2, hid, hd),
                  full(2, hd), full(1, hd)],
        out_specs=(out_spec, out_spec),
        scratch_shapes=[pltpu.VMEM((n_ch + SUBLANES, hid), F32)],
        compiler_params=_params(("arbitrary", "arbitrary"), 12 * n_ch * hid2 * 4),
        name="cmp_out",
    )(parts, parts, b1, pe_out, w2, b2, kg.reshape(1, hd))


def _overlap(n_ch, n_blk):
    c0 = _iota((n_ch, n_blk), 0) * CMP_STRIDE
    b0 = _iota((n_ch, n_blk), 1) * SEL_BLOCK
    return jnp.where(c0 < b0 + SEL_BLOCK, jnp.where(c0 + CMP_BLOCK > b0, 1.0, 0.0), 0.0).astype(BF16)


def _cmp_scores(q, kc, slope, tpos, n_cmp):
    n_ch = kc.shape[0]
    c_idx = _iota((1, n_ch), 1)
    d_c = tpos - (c_idx * CMP_STRIDE + (CMP_BLOCK - 1))
    valid = jnp.where(c_idx < n_cmp, d_c, -1) >= 0
    s = _dot_nt(q, kc) - slope * d_c.astype(F32)
    s = jnp.where(valid, s, -jnp.inf)
    m = jnp.max(s, axis=-1, keepdims=True)
    m = jnp.where(m == -jnp.inf, 0.0, m)
    e = jnp.exp(s - m)
    return e / jnp.maximum(jnp.sum(e, axis=-1, keepdims=True), 1e-30)


def _force_and_mask(imp, tpos):
    blk = _iota((1, imp.shape[1]), 1)
    cur = tpos // SEL_BLOCK
    forced = jnp.logical_or(blk == 0, jnp.logical_or(blk == cur, blk == cur - 1))
    imp = jnp.where(forced, FORCE_SCORE, imp)
    return jnp.where(blk > cur, -jnp.inf, imp)


def _attn_prompt_kernel(slope_ref, q_ref, kc_ref, vc_ref, ks_ref, vs_ref, kw_ref, vw_ref, gt_ref, o_ref,
                        m_s, l_s, acc_s, oc_s, *, grp, seq, tq, tk, n_cmp, n_blk, n_top, wl):
    hk = pl.program_id(1)
    t0 = pl.program_id(2) * tq
    hd = LANES
    row_t = t0 + _iota((tq, 1), 0)
    heads = [(g, slope_ref[hk * grp + g], slice(g * hd, (g + 1) * hd)) for g in range(grp)]

    kc = kc_ref[...]
    vc = vc_ref[...]
    n_ch = kc.shape[0]
    psum = jnp.zeros((tq, n_ch), F32)
    for g, slope, cols in heads:
        p = _cmp_scores(q_ref[:, cols], kc, slope, row_t, n_cmp)
        psum = psum + p
        oc_s[g] = _dot(p.astype(BF16), vc)
    ov = _overlap(n_ch, n_blk)
    p1, p2, p3 = _split3(psum)
    imp = _force_and_mask(_dot(p1, ov) + _dot(p2, ov) + _dot(p3, ov), row_t)
    blk = _iota((1, n_blk), 1)
    rank = jnp.zeros((tq, n_blk), F32)
    for j in range(n_blk):
        cj = imp[:, j:j + 1]
        tie = jnp.where(blk > j, 1.0, 0.0)
        rank = rank + jnp.where(cj > imp, 1.0, jnp.where(cj == imp, tie, 0.0))
    sel = jnp.where(rank < n_top, 1.0, 0.0).astype(BF16)

    m_s[...] = jnp.full(m_s.shape, MASKED, F32)
    l_s[...] = jnp.zeros(l_s.shape, F32)
    acc_s[...] = jnp.zeros(acc_s.shape, F32)

    def kv_tile(kt, carry):
        k0 = pl.multiple_of(kt * tk, tk)
        kk = ks_ref[pl.ds(k0, tk), :].astype(BF16)
        vv = vs_ref[pl.ds(k0, tk), :].astype(BF16)
        expand = jnp.where(_iota((n_blk, tk), 0) == (k0 + _iota((n_blk, tk), 1)) // SEL_BLOCK, 1.0, 0.0)
        picked = _dot(sel, expand.astype(BF16))
        d = row_t - (k0 + _iota((1, tk), 1))
        ok = jnp.where(d >= 0, picked, 0.0) > 0.5
        df = d.astype(F32)
        for g, slope, cols in heads:
            s = jnp.where(ok, _dot_nt(q_ref[:, cols], kk) - slope * df, MASKED)
            m_old = m_s[g]
            m_new = jnp.maximum(m_old, jnp.max(s, axis=-1, keepdims=True))
            alpha = jnp.exp(m_old - m_new)
            p = jnp.exp(s - m_new)
            l_s[g] = alpha * l_s[g] + jnp.sum(p, axis=-1, keepdims=True)
            acc_s[g] = alpha * acc_s[g] + _dot(p.astype(BF16), vv)
            m_s[g] = m_new
        return carry

    lax.fori_loop(0, (t0 + tq + tk - 1) // tk, kv_tile, 0)

    ws = pl.multiple_of(jnp.clip(t0 + tq - wl, 0, seq - wl), SUBLANES)
    kw = kw_ref[pl.ds(ws, wl), :].astype(BF16)
    vw = vw_ref[pl.ds(ws, wl), :].astype(BF16)
    dw = row_t - (ws + _iota((1, wl), 1))
    okw = jnp.where(dw >= 0, jnp.where(dw < WINDOW, 1.0, 0.0), 0.0) > 0.5
    dwf = dw.astype(F32)
    gates = gt_ref[...]
    for g, slope, cols in heads:
        s = jnp.where(okw, _dot_nt(q_ref[:, cols], kw) - slope * dwf, MASKED)
        e = jnp.exp(s - jnp.max(s, axis=-1, keepdims=True))
        o_w = _dot(e.astype(BF16), vw) / jnp.sum(e, axis=-1, keepdims=True)
        o_s = acc_s[g] / l_s[g]
        out = (gates[:, g:g + 1] * oc_s[g] + gates[:, grp + g:grp + g + 1] * o_s
               + gates[:, 2 * grp + g:2 * grp + g + 1] * o_w)
        o_ref[:, cols] = out.astype(o_ref.dtype)


def _attend_prompt(q, kc, vc, kv, gates, slopes, *, batch, seq):
    kvh = N_KV_HEADS
    grp = N_HEADS // kvh
    hd = LANES
    tq = _tile(seq, 256)
    tk = _tile(seq, 512)
    nq = seq // tq
    n_ch = kc.shape[2]
    n_blk = -(-seq // SEL_BLOCK)
    wl = min(tq + WINDOW, seq)
    kv3 = kv.reshape(batch, seq, kv.shape[1])
    row = lambda b, h, i: (b * nq + i, h)
    kv_spec = lambda cb: pl.BlockSpec((None, seq, hd), lambda b, h, i: (b, 0, cb * kvh + h))
    cmp_spec = pl.BlockSpec((None, None, n_ch, hd), lambda b, h, i: (b, h, 0, 0))
    blk = (8 * seq * hd * 4 + 4 * tq * grp * hd * 2 + 3 * grp * tq * hd * 4 + 10 * tq * max(tk, wl) * 4
           + 4 * n_ch * hd * 2)
    return pl.pallas_call(
        functools.partial(_attn_prompt_kernel, grp=grp, seq=seq, tq=tq, tk=tk, n_cmp=n_ch - 1, n_blk=n_blk,
                          n_top=min(N_SEL, n_blk), wl=wl),
        out_shape=jax.ShapeDtypeStruct((batch * seq, N_HEADS * hd), BF16),
        grid=(batch, kvh, nq),
        in_specs=[pl.BlockSpec(memory_space=pltpu.SMEM),
                  pl.BlockSpec((tq, grp * hd), row), cmp_spec, cmp_spec,
                  kv_spec(2), kv_spec(3), kv_spec(4), kv_spec(5),
                  pl.BlockSpec((tq, LANES), row)],
        out_specs=pl.BlockSpec((tq, grp * hd), row),
        scratch_shapes=[pltpu.VMEM((grp, tq, 1), F32), pltpu.VMEM((grp, tq, 1), F32),
                        pltpu.VMEM((grp, tq, hd), F32), pltpu.VMEM((grp, tq, hd), F32)],
        compiler_params=_params(("arbitrary", "arbitrary", "arbitrary"), blk),
        name="nsa_attend_prompt",
    )(slopes, q, kc, vc, kv3, kv3, kv3, kv3, gates)


def _attn_sample_kernel(pt_ref, slope_ref, q_ref, kc_ref, vc_ref, *rest, n_pages, grp, dec_seq, past, n_cmp,
                        n_blk, n_top, nb_pad):
    k_pages = rest[:n_pages]
    v_pages = rest[n_pages:2 * n_pages]
    (ksn_ref, vsn_ref, kwn_ref, vwn_ref, kwc_ref, vwc_ref, gt_ref, o_ref,
     m_s, l_s, acc_s, oc_s, sel_s) = rest[2 * n_pages:]
    hk = pl.program_id(1)
    c = pl.program_id(2)
    rows = q_ref.shape[0]
    q = q_ref[...]
    row = _iota((rows, 1), 0)
    t_row = row // grp
    g_row = row % grp
    slope = jnp.zeros((rows, 1), F32)
    for g in range(grp):
        slope = jnp.where(g_row == g, slope_ref[hk * grp + g], slope)
    tpos = past + t_row

    def online_update(s, v):
        m_old = m_s[...]
        m_new = jnp.maximum(m_old, jnp.max(s, axis=-1, keepdims=True))
        alpha = jnp.exp(m_old - m_new)
        p = jnp.exp(s - m_new)
        l_s[...] = alpha * l_s[...] + jnp.sum(p, axis=-1, keepdims=True)
        acc_s[...] = alpha * acc_s[...] + _dot(p.astype(BF16), v)
        m_s[...] = m_new

    @pl.when(c == 0)
    def _():
        kc = kc_ref[...]
        n_ch = kc.shape[0]
        p = _cmp_scores(q, kc, slope, tpos, n_cmp)
        oc_s[...] = _dot(p.astype(BF16), vc_ref[...])
        ov = _overlap(n_ch, nb_pad)
        p1, p2, p3 = _split3(p)
        per_head = _dot(p1, ov) + _dot(p2, ov) + _dot(p3, ov)
        same_tok = jnp.where(_iota((rows, rows), 0) // grp == _iota((rows, rows), 1) // grp, 1.0, 0.0).astype(BF16)
        a1, a2, a3 = _split3(per_head)
        imp = _force_and_mask(_dot(same_tok, a1) + _dot(same_tok, a2) + _dot(same_tok, a3), tpos)
        blk_f = _iota((1, nb_pad), 1).astype(F32)
        taken = jnp.where(blk_f >= n_blk, 1.0, 0.0) + jnp.zeros((rows, nb_pad), F32)
        sel = jnp.zeros((rows, nb_pad), F32)
        for _ in range(n_top):
            avail = jnp.where(taken > 0.5, -jnp.inf, imp)
            best = jnp.max(avail, axis=-1, keepdims=True)
            cand = jnp.where(taken > 0.5, 0.0, jnp.where(avail == best, 1.0, 0.0))
            idx = jnp.min(jnp.where(cand > 0.5, blk_f, float(nb_pad)), axis=-1, keepdims=True)
            pick = blk_f == idx
            taken = jnp.where(pick, 1.0, taken)
            sel = jnp.where(pick, 1.0, sel)
        sel_s[...] = sel
        m_s[...] = jnp.full(m_s.shape, MASKED, F32)
        l_s[...] = jnp.zeros(l_s.shape, F32)
        acc_s[...] = jnp.zeros(acc_s.shape, F32)

    tk = n_pages * PAGE_SIZE
    k0 = c * tk
    kk = jnp.concatenate([pg[...] for pg in k_pages], axis=0).astype(BF16)
    vv = jnp.concatenate([pg[...] for pg in v_pages], axis=0).astype(BF16)
    expand = jnp.where(_iota((nb_pad, tk), 0) == (k0 + _iota((nb_pad, tk), 1)) // SEL_BLOCK, 1.0, 0.0)
    picked = _dot(sel_s[...].astype(BF16), expand.astype(BF16))
    d = tpos - (k0 + _iota((1, tk), 1))
    ok = jnp.where(d >= 0, picked, 0.0) > 0.5
    online_update(jnp.where(ok, _dot_nt(q, kk) - slope * d.astype(F32), MASKED), vv)

    @pl.when(c == pl.num_programs(2) - 1)
    def _():
        t_new = _iota((1, ksn_ref.shape[0]), 1)
        dn = t_row - t_new
        causal_new = jnp.where(dn >= 0, jnp.where(t_new < dec_seq, 1.0, 0.0), 0.0)
        ok_new = causal_new > 0.5
        dnf = dn.astype(F32)
        online_update(jnp.where(ok_new, _dot_nt(q, ksn_ref[...].astype(BF16)) - slope * dnf, MASKED),
                      vsn_ref[...].astype(BF16))
        o_s = acc_s[...] / l_s[...]
        n_buf = kwc_ref.shape[0]
        dc = tpos - (past - n_buf + _iota((1, n_buf), 1))
        ok_c = jnp.where(dc >= 0, jnp.where(dc < WINDOW, 1.0, 0.0), 0.0) > 0.5
        s_c = jnp.where(ok_c, _dot_nt(q, kwc_ref[...].astype(BF16)) - slope * dc.astype(F32), MASKED)
        ok_n = jnp.where(dn < WINDOW, causal_new, 0.0) > 0.5
        s_n = jnp.where(ok_n, _dot_nt(q, kwn_ref[...].astype(BF16)) - slope * dnf, MASKED)
        m = jnp.maximum(jnp.max(s_c, axis=-1, keepdims=True), jnp.max(s_n, axis=-1, keepdims=True))
        e_c = jnp.exp(s_c - m)
        e_n = jnp.exp(s_n - m)
        den = jnp.sum(e_c, axis=-1, keepdims=True) + jnp.sum(e_n, axis=-1, keepdims=True)
        o_w = (_dot(e_c.astype(BF16), vwc_ref[...].astype(BF16))
               + _dot(e_n.astype(BF16), vwn_ref[...].astype(BF16))) / den
        gates = gt_ref[...]
        o_ref[...] = (gates[:, 0:1] * oc_s[...] + gates[:, 1:2] * o_s + gates[:, 2:3] * o_w).astype(o_ref.dtype)


def _attend_sample(q_rows, kc, vc, cache_sel, page_table, kv_new, cache_win, gate_rows, slopes, *, past, dec_seq):
    batch, kvh, rows, hd = q_rows.shape
    grp = N_HEADS // kvh
    assert past % SEL_BLOCK == 0 and dec_seq <= SEL_BLOCK and past % PAGE_SIZE == 0
    pages_per_batch = past // PAGE_SIZE
    n_pages = _tile(pages_per_batch, 8)
    n_ch = kc.shape[2]
    n_blk = -(-(past + dec_seq) // SEL_BLOCK)
    nb_pad = -(-n_blk // LANES) * LANES
    n_buf = cache_win.shape[1]
    new_rows = kv_new.shape[1]

    def page_spec(r, j):
        return pl.BlockSpec(
            (None, PAGE_SIZE, hd),
            lambda b, h, c, pt: (pt[b * pages_per_batch + c * n_pages + r], 0, j * kvh + h))

    bh = lambda b, h, c, pt: (b, h, 0, 0)
    new_spec = lambda cb: pl.BlockSpec((None, new_rows, hd), lambda b, h, c, pt: (b, 0, cb * kvh + h))
    win_spec = lambda j: pl.BlockSpec((None, n_buf, hd), lambda b, h, c, pt: (b, 0, j * kvh + h))
    cmp_spec = pl.BlockSpec((None, None, n_ch, hd), bh)
    row_spec = pl.BlockSpec((None, None, rows, hd), bh)
    tk = n_pages * PAGE_SIZE
    blk = (4 * n_pages * PAGE_SIZE * hd * 4 + 4 * n_ch * hd * 2 + 4 * n_buf * hd * 4
           + 12 * rows * max(tk, n_ch, nb_pad) * 4 + 2 * nb_pad * max(tk, n_ch) * 4)
    return pl.pallas_call(
        functools.partial(_attn_sample_kernel, n_pages=n_pages, grp=grp, dec_seq=dec_seq, past=past,
                          n_cmp=n_ch - 1, n_blk=n_blk, n_top=min(N_SEL, n_blk), nb_pad=nb_pad),
        out_shape=jax.ShapeDtypeStruct((batch, kvh, rows, hd), BF16),
        grid_spec=pltpu.PrefetchScalarGridSpec(
            num_scalar_prefetch=1,
            grid=(batch, kvh, pages_per_batch // n_pages),
            in_specs=[pl.BlockSpec(memory_space=pltpu.SMEM), row_spec, cmp_spec, cmp_spec]
            + [page_spec(r, 0) for r in range(n_pages)] + [page_spec(r, 1) for r in range(n_pages)]
            + [new_spec(2), new_spec(3), new_spec(4), new_spec(5), win_spec(0), win_spec(1), row_spec],
            out_specs=row_spec,
            scratch_shapes=[pltpu.VMEM((rows, 1), F32), pltpu.VMEM((rows, 1), F32), pltpu.VMEM((rows, hd), F32),
                            pltpu.VMEM((rows, hd), F32), pltpu.VMEM((rows, nb_pad), F32)]),
        compiler_params=_params(("arbitrary", "arbitrary", "arbitrary"), blk),
        name="nsa_attend_sample",
    )(page_table, slopes, q_rows, kc, vc, *([cache_sel] * (2 * n_pages)), kv_new, kv_new, kv_new, kv_new,
      cache_win, cache_win, gate_rows)


def _nsa_layer(hp, hs, xp, xs, g1p, g1s, caches, page_table, weights, slopes, *, batch, seq, dec_batch, dec_seq):
    (w_in, w_out, q_g, k_g, pe, w1, b1, w2, b2) = weights
    cache_cmp, cache_sel, cache_win = caches
    d = xp.shape[-1]
    kvh = N_KV_HEADS
    grp = N_HEADS // kvh
    hd = LANES
    cols = 2 * kvh * hd
    n_s = dec_batch * dec_seq
    past = page_table.shape[1] * PAGE_SIZE
    assert seq % PAGE_SIZE == 0 and past % CMP_STRIDE == 0 and dec_seq < CMP_STRIDE
    cmp_w = (w1, b1, w2, b2, pe, k_g[0])

    q, kv, gates = _nsa_project(hp, w_in, q_g, k_g)
    n_pages_p = batch * seq // PAGE_SIZE
    kc, vc = _compress(kv.reshape(n_pages_p, PAGE_SIZE, 3 * cols), jnp.arange(n_pages_p, dtype=I32), batch, *cmp_w)
    o = _attend_prompt(q, kc, vc, kv, gates, slopes, batch=batch, seq=seq)
    xp = _mm(o, w_out, col0=0, n=d, epi="res", extra=(xp.reshape(batch * seq, d), g1p), rows_per_batch=seq,
             name="nsa_out").reshape(batch, seq, d)
    kv_p = kv.reshape(batch, seq, 3, 2, kvh, hd)
    new_p = (kv_p[:, :, 0], kv_p[:, :, 1], kv_p[:, seq - min(WINDOW, seq):, 2])

    qs, kvs, gs = _nsa_project(hs, w_in, q_g, k_g)
    pt = page_table.reshape(-1)
    n_phys = cache_cmp.shape[0]
    kcs, vcs = _compress(cache_cmp.reshape(n_phys, PAGE_SIZE, cols), pt, dec_batch, *cmp_w)
    q_rows = jnp.transpose(qs.reshape(dec_batch, dec_seq, kvh, grp, hd), (0, 2, 1, 3, 4))
    q_rows = q_rows.reshape(dec_batch, kvh, dec_seq * grp, hd)
    gate_rows = gs.reshape(dec_batch, dec_seq, kvh, LANES)[..., :3 * grp].reshape(dec_batch, dec_seq, kvh, 3, grp)
    gate_rows = jnp.transpose(gate_rows, (0, 2, 1, 4, 3)).reshape(dec_batch, kvh, dec_seq * grp, 3)
    gate_rows = jnp.pad(gate_rows, ((0, 0), (0, 0), (0, 0), (0, LANES - 3)))
    kv_new = jnp.pad(kvs.reshape(dec_batch, dec_seq, 3 * cols), ((0, 0), (0, SUBLANES - dec_seq), (0, 0)))
    n_buf = cache_win.shape[1]
    o_rows = _attend_sample(q_rows, kcs, vcs, cache_sel.reshape(n_phys, PAGE_SIZE, cols), pt, kv_new,
                            cache_win.reshape(dec_batch, n_buf, cols), gate_rows, slopes, past=past,
                            dec_seq=dec_seq)
    o_s = jnp.transpose(o_rows.reshape(dec_batch, kvh, dec_seq, grp, hd), (0, 2, 1, 3, 4)).reshape(n_s, N_HEADS * hd)
    xs = _mm(o_s, w_out, col0=0, n=d, epi="res", extra=(xs.reshape(n_s, d), g1s[0]),
             name="nsa_out").reshape(1, n_s, d)
    kv_s = kvs.reshape(dec_batch, dec_seq, 3, 2, kvh, hd)
    win_s = jnp.concatenate([cache_win, kv_s[:, :, 2]], axis=1)[:, -n_buf:]
    new_s = (kv_s[:, :, 0], kv_s[:, :, 1], win_s)
    return xp, xs, new_p, new_s


def kernel(x_prompt, x_sample, c_prompt, c_sample, state_conv, cache_cmp_kv, cache_sel_kv, cache_win_kv,
           page_table, w_mod, b_mod, norm_g, conv_w_in, conv_w, conv_w_out, nsa_w_in, nsa_w_out, q_norm_g,
           k_norm_g, cmp_pe, cmp_w1, cmp_b1, cmp_w2, cmp_b2, router_w, router_b, moe_w_gu, moe_b_gu,
           moe_w_down, moe_b_down):
    batch, seq, d = x_prompt.shape
    dec_batch, dec_seq, _ = x_sample.shape
    depth = w_mod.shape[0]
    n_s = dec_batch * dec_seq
    assert dec_seq >= conv_w.shape[1] - 1

    c_all = jnp.concatenate([c_prompt, c_sample], axis=0)
    c_all = jnp.pad(c_all, ((0, -c_all.shape[0] % SUBLANES), (0, 0)))
    mod = _adaln(c_all, w_mod, b_mod)
    slopes = jnp.exp2(-8.0 * jnp.arange(1, N_HEADS + 1, dtype=F32) / N_HEADS)

    xp = x_prompt
    xs = x_sample.reshape(1, n_s, d)
    conv_p, conv_s, cmp_p, cmp_s, sel_p, sel_s, win_p, win_s = [], [], [], [], [], [], [], []
    for i in range(depth):
        j = i // 2
        sh1p, sc1p, g1p, sh2p, sc2p, g2p = [m[:, None, :] for m in jnp.split(mod[i, :batch], 6, axis=-1)]
        sh1s, sc1s, g1s, sh2s, sc2s, g2s = [jnp.repeat(m, dec_seq, axis=0)[None]
                                            for m in jnp.split(mod[i, batch:batch + dec_batch], 6, axis=-1)]
        hp = _norm_mod(xp, norm_g[i, 0], sc1p, sh1p).reshape(batch * seq, d)
        hs = _norm_mod(xs, norm_g[i, 0], sc1s, sh1s).reshape(n_s, d)
        if i % 2 == 0:
            a_p, state_p = _conv_in(hp, conv_w_in[j], conv_w[j], batch=batch, seq=seq)
            a_s, v_s = _conv_in(hs, conv_w_in[j], conv_w[j], batch=dec_batch, seq=dec_seq,
                                prev=_conv_prev_rows(state_conv[j], dec_seq))
            xp = _mm(a_p, conv_w_out[j], col0=0, n=d, epi="res", extra=(xp.reshape(batch * seq, d), g1p),
                     rows_per_batch=seq, name="conv_out").reshape(batch, seq, d)
            xs = _mm(a_s, conv_w_out[j], col0=0, n=d, epi="res", extra=(xs.reshape(n_s, d), g1s[0]),
                     name="conv_out").reshape(1, n_s, d)
            conv_p.append(state_p)
            conv_s.append(v_s.reshape(dec_batch, dec_seq, d)[:, dec_seq - 2:])
        else:
            weights = (nsa_w_in[j], nsa_w_out[j], q_norm_g[j], k_norm_g[j], cmp_pe[j], cmp_w1[j], cmp_b1[j],
                       cmp_w2[j], cmp_b2[j])
            xp, xs, new_p, new_s = _nsa_layer(
                hp, hs, xp, xs, g1p, g1s, (cache_cmp_kv[j], cache_sel_kv[j], cache_win_kv[j]), page_table,
                weights, slopes, batch=batch, seq=seq, dec_batch=dec_batch, dec_seq=dec_seq)
            cmp_p.append(new_p[0])
            sel_p.append(new_p[1])
            win_p.append(new_p[2])
            cmp_s.append(new_s[0])
            sel_s.append(new_s[1])
            win_s.append(new_s[2])
        xp, xs = _moe(xp, xs, norm_g[i, 1], (sc2p, sh2p, g2p), (sc2s, sh2s, g2s), router_w[i], router_b[i],
                      moe_w_gu[i], moe_b_gu[i], moe_w_down[i], moe_b_down[i])
    return (xp, xs.reshape(dec_batch, dec_seq, d), jnp.stack(conv_p), jnp.stack(conv_s), jnp.stack(cmp_p),
            jnp.stack(cmp_s), jnp.stack(sel_p), jnp.stack(sel_s), jnp.stack(win_p), jnp.stack(win_s))
```

```python
import functools

import jax
import jax.numpy as jnp
from jax import lax
from jax.experimental import pallas as pl
from jax.experimental.pallas import tpu as pltpu

F32 = jnp.float32
BF16 = jnp.bfloat16
I32 = jnp.int32
U32 = jnp.uint32

N_HEADS = 32
N_KV_HEADS = 4
CMP_BLOCK = 32
CMP_STRIDE = 16
SEL_BLOCK = 64
N_SEL = 16
WINDOW = 512
PAGE_SIZE = 128
TOP_K = 4
SWIGLU_LIMIT = 7.0
SWIGLU_ALPHA = 1.702
EPS = 1e-6
FORCE_SCORE = 1e4
MASKED = -1e30

LANES = 128
SUBLANES = 8
VMEM_PHYSICAL_BYTES = 64 * 1024 * 1024
VMEM_CAP_BYTES = VMEM_PHYSICAL_BYTES - 6 * 1024 * 1024


def _vmem_limit(block_bytes):
    return int(min(VMEM_CAP_BYTES, block_bytes * 5 // 4 + (4 << 20)))


def _params(sem, block_bytes):
    return pltpu.CompilerParams(dimension_semantics=sem, vmem_limit_bytes=_vmem_limit(block_bytes))


def _tile(n, pref):
    if n <= pref:
        return n
    t = pref
    while n % t:
        t //= 2
    return t


def _dot(a, b):
    return jnp.dot(a, b, preferred_element_type=F32)


def _dot_nt(a, b):
    return lax.dot_general(a, b, (((1,), (1,)), ((), ())), preferred_element_type=F32)


def _split3(x):
    hi = x.astype(BF16)
    r = x - hi.astype(F32)
    mid = r.astype(BF16)
    lo = (r - mid.astype(F32)).astype(BF16)
    return hi, mid, lo


def _iota(shape, dim):
    return lax.broadcasted_iota(I32, shape, dim)


def _rms(a):
    return a * lax.rsqrt(jnp.mean(a * a, axis=-1, keepdims=True) + EPS)


def _adaln_kernel(c_ref, w_ref, b_ref, o_ref, *, kc):
    c = c_ref[...]
    a = (c * jax.nn.sigmoid(c)).astype(BF16)
    acc = jnp.zeros(o_ref.shape, F32)
    for k0 in range(0, a.shape[1], kc):
        acc = acc + _dot(a[:, k0:k0 + kc], w_ref[k0:k0 + kc, :].astype(BF16))
    o_ref[...] = acc + b_ref[...]


def _adaln(c_all, w_mod, b_mod):
    n_layers, d, n6 = w_mod.shape
    r = c_all.shape[0]
    tn = _tile(n6, 1024)
    kc = _tile(d, 1024)
    blk = 2 * d * tn * 4 + d * tn * 2 + r * d * 4
    return pl.pallas_call(
        functools.partial(_adaln_kernel, kc=kc),
        out_shape=jax.ShapeDtypeStruct((n_layers, r, n6), F32),
        grid=(n_layers, n6 // tn),
        in_specs=[
            pl.BlockSpec((r, d), lambda l, j: (0, 0)),
            pl.BlockSpec((None, d, tn), lambda l, j: (l, 0, j)),
            pl.BlockSpec((None, 1, tn), lambda l, j: (l, 0, j)),
        ],
        out_specs=pl.BlockSpec((None, r, tn), lambda l, j: (l, 0, j)),
        compiler_params=_params(("arbitrary", "arbitrary"), blk),
        name="adaln_mod",
    )(c_all, w_mod, b_mod.reshape(n_layers, 1, n6))


def _modulated(x_ref, g_ref, sc_ref, sh_ref):
    return _rms(x_ref[...]) * g_ref[...] * (1.0 + sc_ref[...]) + sh_ref[...]


def _norm_mod_kernel(x_ref, g_ref, sc_ref, sh_ref, h_ref):
    h_ref[...] = _modulated(x_ref, g_ref, sc_ref, sh_ref).astype(h_ref.dtype)


def _mod_spec(mod, tm, d):
    if mod.shape[1] == 1:
        return pl.BlockSpec((None, 1, d), lambda b, i: (b, 0, 0))
    return pl.BlockSpec((None, tm, d), lambda b, i: (b, i, 0))


def _norm_mod(x, g, scale, shift):
    b, t, d = x.shape
    tm = _tile(t, 512)
    blk = 2 * tm * d * (4 + 2) + 6 * d * 4 + 2 * tm * d * 4
    return pl.pallas_call(
        _norm_mod_kernel,
        out_shape=jax.ShapeDtypeStruct((b, t, d), BF16),
        grid=(b, t // tm),
        in_specs=[
            pl.BlockSpec((None, tm, d), lambda b_, i: (b_, i, 0)),
            pl.BlockSpec((1, d), lambda b_, i: (0, 0)),
            _mod_spec(scale, tm, d),
            _mod_spec(shift, tm, d),
        ],
        out_specs=pl.BlockSpec((None, tm, d), lambda b_, i: (b_, i, 0)),
        compiler_params=_params(("arbitrary", "arbitrary"), blk),
        name="norm_mod",
    )(x, g.reshape(1, d), scale, shift)


def _norm_router_kernel(x_ref, g_ref, sc_ref, sh_ref, wr_ref, br_ref, hp_ref, ti_ref, gt_ref):
    h = _modulated(x_ref, g_ref, sc_ref, sh_ref)
    tm, d = h.shape
    dh = d // 2
    lo = pltpu.bitcast(h[:, :dh].astype(BF16).astype(F32), U32)
    hi = pltpu.bitcast(h[:, dh:].astype(BF16).astype(F32), U32)
    hp_ref[...] = (lo >> 16) | hi

    h1, h2, h3 = _split3(h)
    w1, w2, w3 = _split3(wr_ref[...])
    logits = (_dot(h1, w1) + (_dot(h1, w2) + _dot(h2, w1))
              + (_dot(h2, w2) + _dot(h1, w3) + _dot(h3, w1))) + br_ref[...]
    n_exp = logits.shape[1]
    lane = _iota(logits.shape, 1).astype(F32)
    work = logits
    vals, idxs = [], []
    for _ in range(TOP_K):
        m = jnp.max(work, axis=-1, keepdims=True)
        idx = jnp.min(jnp.where(work == m, lane, float(n_exp)), axis=-1, keepdims=True)
        vals.append(m)
        idxs.append(idx)
        work = jnp.where(lane == idx, -jnp.inf, work)
    es = [jnp.exp(v - vals[0]) for v in vals]
    den = es[0]
    for e in es[1:]:
        den = den + e
    lane_o = _iota((tm, LANES), 1)
    ti = jnp.zeros((tm, LANES), F32)
    gt = jnp.zeros((tm, LANES), F32)
    for k in range(TOP_K):
        ti = jnp.where(lane_o == k, idxs[k], ti)
        gt = jnp.where(lane_o == k, es[k] / den, gt)
    ti_ref[...] = ti.astype(I32)
    gt_ref[...] = gt


def _norm_router(x, g, scale, shift, w_router, b_router):
    b, t, d = x.shape
    n_exp = w_router.shape[1]
    tm = _tile(t, 256)
    blk = 2 * tm * d * 4 + 2 * tm * d * 2 + 8 * tm * d * 4 + 2 * d * n_exp * 4
    row = lambda b_, i: (b_, i, 0)
    return pl.pallas_call(
        _norm_router_kernel,
        out_shape=(jax.ShapeDtypeStruct((b, t, d // 2), U32),
                   jax.ShapeDtypeStruct((b, t, LANES), I32),
                   jax.ShapeDtypeStruct((b, t, LANES), F32)),
        grid=(b, t // tm),
        in_specs=[
            pl.BlockSpec((None, tm, d), row),
            pl.BlockSpec((1, d), lambda b_, i: (0, 0)),
            _mod_spec(scale, tm, d),
            _mod_spec(shift, tm, d),
            pl.BlockSpec((d, n_exp), lambda b_, i: (0, 0)),
            pl.BlockSpec((1, n_exp), lambda b_, i: (0, 0)),
        ],
        out_specs=(pl.BlockSpec((None, tm, d // 2), row),
                   pl.BlockSpec((None, tm, LANES), row),
                   pl.BlockSpec((None, tm, LANES), row)),
        compiler_params=_params(("arbitrary", "arbitrary"), blk),
        name="norm_router",
    )(x, g.reshape(1, d), scale, shift, w_router, b_router.reshape(1, n_exp))


def _mm_kernel(a_ref, w_ref, *rest, epi, head_dim, scale):
    *ins, o_ref, wb = rest

    @pl.when(pl.program_id(1) == 0)
    def _():
        wb[...] = w_ref[...].astype(BF16)

    acc = _dot(a_ref[...], wb[...])
    tn = acc.shape[1]
    if epi == "res":
        x_ref, g_ref = ins
        o_ref[...] = x_ref[...] + g_ref[...] * acc
    elif epi == "qnorm":
        (gq_ref,) = ins
        for c in range(tn // head_dim):
            a = acc[:, c * head_dim:(c + 1) * head_dim]
            o_ref[:, c * head_dim:(c + 1) * head_dim] = (_rms(a) * gq_ref[...] * scale).astype(o_ref.dtype)
    elif epi == "kvnorm":
        fl_ref, gk_ref = ins
        for c in range(tn // head_dim):
            sl = slice(c * head_dim, (c + 1) * head_dim)
            a = acc[:, sl]
            o_ref[:, sl] = jnp.where(fl_ref[:, sl] > 0.5, _rms(a) * gk_ref[:, sl], a)
    elif epi == "sigmoid":
        o_ref[...] = jax.nn.sigmoid(acc)
    else:
        raise ValueError(epi)


def _mm(a, w, *, col0, n, epi, extra=(), rows_per_batch=None, out_dtype=F32, head_dim=LANES, scale=1.0,
        name="mm"):
    m, k = a.shape
    tm = _tile(m, 512)
    tn = _tile(n, 512)
    assert col0 % tn == 0
    j0 = col0 // tn
    in_specs = [pl.BlockSpec((tm, k), lambda j, i: (i, 0)),
                pl.BlockSpec((k, tn), lambda j, i: (0, j + j0))]
    operands = [a, w]
    if epi == "res":
        x, g = extra
        in_specs.append(pl.BlockSpec((tm, tn), lambda j, i: (i, j)))
        if g.ndim == 3:
            assert rows_per_batch % tm == 0
            in_specs.append(pl.BlockSpec((None, 1, tn), lambda j, i: (i * tm // rows_per_batch, 0, j)))
        else:
            in_specs.append(pl.BlockSpec((tm, tn), lambda j, i: (i, j)))
        operands += [x, g]
    elif epi == "qnorm":
        in_specs.append(pl.BlockSpec((1, head_dim), lambda j, i: (0, 0)))
        operands += list(extra)
    elif epi == "kvnorm":
        in_specs += [pl.BlockSpec((1, tn), lambda j, i: (0, j))] * 2
        operands += list(extra)
    blk = 2 * tm * k * 2 + 2 * k * tn * 4 + k * tn * 2 + 6 * tm * tn * 4
    return pl.pallas_call(
        functools.partial(_mm_kernel, epi=epi, head_dim=head_dim, scale=scale),
        out_shape=jax.ShapeDtypeStruct((m, n), out_dtype),
        grid=(n // tn, m // tm),
        in_specs=in_specs,
        out_specs=pl.BlockSpec((tm, tn), lambda j, i: (i, j)),
        scratch_shapes=[pltpu.VMEM((k, tn), BF16)],
        compiler_params=_params(("arbitrary", "arbitrary"), blk),
        name=name,
    )(*operands)


def _conv_in_kernel(a_ref, wb_ref, wc_ref, wu_ref, cw_ref, *rest, tiles_per_batch, seq, per_token_prev):
    if per_token_prev:
        p1_ref, p2_ref, o_ref, v_ref, wbuf, ext = rest
    else:
        o_ref, st_ref, wbuf, ext = rest
    i = pl.program_id(1)

    @pl.when(i == 0)
    def _():
        wbuf[0] = wb_ref[...].astype(BF16)
        wbuf[1] = wc_ref[...].astype(BF16)
        wbuf[2] = wu_ref[...].astype(BF16)

    a = a_ref[...]
    b_gate = _dot(a, wbuf[0])
    v = _dot(a, wbuf[1]) * _dot(a, wbuf[2])
    tm, tn = v.shape

    if per_token_prev:
        ext[0:SUBLANES, :] = jnp.zeros((SUBLANES, tn), F32)
    else:
        @pl.when(i % tiles_per_batch == 0)
        def _():
            ext[0:SUBLANES, :] = jnp.zeros((SUBLANES, tn), F32)

    ext[SUBLANES:SUBLANES + tm, :] = v
    s1 = ext[SUBLANES - 1:SUBLANES - 1 + tm, :]
    s2 = ext[SUBLANES - 2:SUBLANES - 2 + tm, :]
    if per_token_prev:
        tpos = _iota((tm, 1), 0) % seq
        s1 = jnp.where(tpos >= 1, s1, p1_ref[...])
        s2 = jnp.where(tpos >= 2, s2, p2_ref[...])
    cw = cw_ref[...]
    conv = s2 * cw[0:1, :] + s1 * cw[1:2, :] + v * cw[2:3, :]
    o_ref[...] = (b_gate * conv).astype(o_ref.dtype)

    if per_token_prev:
        v_ref[...] = v
    else:
        ext[0:SUBLANES, :] = ext[tm:tm + SUBLANES, :]

        @pl.when(i % tiles_per_batch == tiles_per_batch - 1)
        def _():
            st_ref[...] = ext[tm + SUBLANES - 2:tm + SUBLANES, :]


def _conv_in(h, w_in, conv_w, *, batch, seq, prev=None):
    m, d = h.shape
    tn = _tile(d, 256)
    nd = d // tn
    per_token_prev = prev is not None
    tm = m if per_token_prev else _tile(seq, 512)
    tiles_per_batch = max(seq // tm, 1)
    in_specs = [pl.BlockSpec((tm, d), lambda j, i: (i, 0)),
                pl.BlockSpec((d, tn), lambda j, i: (0, j)),
                pl.BlockSpec((d, tn), lambda j, i: (0, j + nd)),
                pl.BlockSpec((d, tn), lambda j, i: (0, j + 2 * nd)),
                pl.BlockSpec((3, tn), lambda j, i: (0, j))]
    operands = [h, w_in, w_in, w_in, conv_w]
    tile_spec = pl.BlockSpec((tm, tn), lambda j, i: (i, j))
    if per_token_prev:
        in_specs += [tile_spec, tile_spec]
        operands += list(prev)
        out_shape = (jax.ShapeDtypeStruct((m, d), BF16), jax.ShapeDtypeStruct((m, d), F32))
        out_specs = (tile_spec, tile_spec)
    else:
        out_shape = (jax.ShapeDtypeStruct((m, d), BF16), jax.ShapeDtypeStruct((batch, 2, d), F32))
        out_specs = (tile_spec, pl.BlockSpec((None, 2, tn), lambda j, i: (i // tiles_per_batch, 0, j)))
    blk = 2 * tm * d * 2 + 6 * d * tn * 4 + 3 * d * tn * 2 + 10 * tm * tn * 4
    return pl.pallas_call(
        functools.partial(_conv_in_kernel, tiles_per_batch=tiles_per_batch, seq=seq,
                          per_token_prev=per_token_prev),
        out_shape=out_shape,
        grid=(nd, m // tm),
        in_specs=in_specs,
        out_specs=out_specs,
        scratch_shapes=[pltpu.VMEM((3, d, tn), BF16), pltpu.VMEM((tm + 2 * SUBLANES, tn), F32)],
        compiler_params=_params(("arbitrary", "arbitrary"), blk),
        name="conv_in",
    )(*operands)


def _conv_prev_rows(state, seq):
    b, _, d = state.shape
    zeros = jnp.zeros((b, seq, d), state.dtype)
    p1 = zeros.at[:, 0].set(state[:, 1])
    p2 = zeros.at[:, 0].set(state[:, 0]).at[:, 1].set(state[:, 1])
    return p1.reshape(b * seq, d), p2.reshape(b * seq, d)


def _plan_kernel(ti_ref, pos_ref, cnt_ref, carry):
    @pl.when(pl.program_id(0) == 0)
    def _():
        carry[...] = jnp.zeros(carry.shape, F32)

    ti = ti_ref[...]
    tm = ti.shape[0]
    e_iota = _iota((tm, LANES), 1)
    onehots = [jnp.where(ti[:, k:k + 1] == e_iota, 1.0, 0.0) for k in range(TOP_K)]
    hits = onehots[0]
    for oh in onehots[1:]:
        hits = hits + oh
    strictly_lower = jnp.where(_iota((tm, tm), 0) > _iota((tm, tm), 1), 1.0, 0.0).astype(BF16)
    before = _dot(strictly_lower, hits.astype(BF16)) + carry[...]
    out = jnp.zeros((tm, LANES), F32)
    for k in range(TOP_K):
        out = jnp.where(e_iota == k, jnp.sum(onehots[k] * before, axis=-1, keepdims=True), out)
    pos_ref[...] = out.astype(I32)
    carry[...] = carry[...] + jnp.sum(hits, axis=0, keepdims=True)
    cnt_ref[...] = carry[...]


def _plan(topi):
    n = topi.shape[0]
    tm = _tile(n, 256)
    return pl.pallas_call(
        _plan_kernel,
        out_shape=(jax.ShapeDtypeStruct((n, LANES), I32), jax.ShapeDtypeStruct((1, LANES), F32)),
        grid=(n // tm,),
        in_specs=[pl.BlockSpec((tm, LANES), lambda i: (i, 0))],
        out_specs=(pl.BlockSpec((tm, LANES), lambda i: (i, 0)), pl.BlockSpec((1, LANES), lambda i: (0, 0))),
        scratch_shapes=[pltpu.VMEM((1, LANES), F32)],
        compiler_params=_params(("arbitrary",), 8 * tm * LANES * 4 + tm * tm * 8),
        name="moe_plan",
    )(topi)


def _dispatch_tables(topi, n_exp, sub, rmax):
    n = topi.shape[0]
    ids = topi[:, :TOP_K]
    n_pairs = n * TOP_K
    n_pad = -(-n // 256) * 256
    pos, counts = _plan(jnp.pad(topi, ((0, n_pad - n), (0, 0)), constant_values=-1))
    counts = counts[0, :n_exp].astype(I32)
    padded = (counts + sub - 1) // sub * sub
    pad_start = jnp.cumsum(padded) - padded
    dest = pad_start[ids] + pos[:n, :TOP_K]
    n_rows = -(-(n_pairs + n_exp * (sub - 1)) // sub) * sub
    row_tok = jnp.zeros((n_rows,), I32).at[dest.reshape(-1)].set(jnp.arange(n_pairs, dtype=I32) // TOP_K)
    n_sb = (counts + rmax - 1) // rmax
    cum = jnp.cumsum(n_sb)
    total = cum[-1]
    n_sb_max = n_exp + n_pairs // rmax
    s = jnp.arange(n_sb_max, dtype=I32)
    e_of = jnp.minimum(jnp.searchsorted(cum, s, side="right"), n_exp - 1).astype(I32)
    local = s - (cum - n_sb)[e_of]
    active = s < total
    rows = jnp.where(active, jnp.minimum(counts[e_of] - local * rmax, rmax), 0).astype(I32)
    start = jnp.where(active, pad_start[e_of] + local * rmax, 0).astype(I32)
    sb_exp = jnp.where(active, e_of, e_of[jnp.maximum(total - 1, 0)]).astype(I32)
    n_used = jnp.sum(padded).astype(I32).reshape(1)
    return dest.astype(I32), row_tok, sb_exp, start, rows, n_used, n_rows, n_sb_max


def _expert_kernel(exp_ref, start_ref, rows_ref, tok_ref, used_ref,
                   h_hbm, wg_ref, wu_ref, bg_ref, bu_ref, wd_ref, bd_ref, ys_hbm,
                   xbuf, act, wgu, wdb, ostage, gsem, osem, *, n1, sub, tf, dt):
    sb = pl.program_id(0)
    s = pl.program_id(1)
    rows = rows_ref[sb]
    start = start_ref[sb]
    n_sub = (rows + sub - 1) // sub
    dh = xbuf.shape[1]

    @pl.when(jnp.logical_and(sb == 0, s == 0))
    def _():
        ostage[0] = jnp.zeros(ostage.shape[1:], F32)
        n_tail = (ys_hbm.shape[0] - used_ref[0]) // sub

        def tail_copy(t, col):
            row0 = pl.multiple_of(used_ref[0] + t * sub, sub)
            return pltpu.make_async_copy(ostage.at[0], ys_hbm.at[pl.ds(row0, sub), pl.ds(col * dt, dt)],
                                         osem.at[0])

        def issue(t, c):
            for col in range(ys_hbm.shape[1] // dt):
                tail_copy(t, col).start()
            return c

        def drain(t, c):
            for col in range(ys_hbm.shape[1] // dt):
                tail_copy(t, col).wait()
            return c

        lax.fori_loop(0, n_tail, issue, 0)
        lax.fori_loop(0, n_tail, drain, 0)

    def gather_copy(r):
        tok = tok_ref[start + r]
        return pltpu.make_async_copy(h_hbm.at[pl.ds(tok, 1)], xbuf.at[pl.ds(r, 1)], gsem)

    @pl.when(jnp.logical_and(s == 0, rows > 0))
    def _():
        def issue(r, c):
            gather_copy(r).start()
            return c

        def drain(r, c):
            gather_copy(r).wait()
            return c

        lax.fori_loop(0, n_sub * sub, issue, 0)
        lax.fori_loop(0, n_sub * sub, drain, 0)

    def unpack(i):
        words = xbuf[pl.ds(pl.multiple_of(i * sub, sub), sub), :]
        lo = pltpu.bitcast(words << 16, F32).astype(BF16)
        hi = pltpu.bitcast(words & jnp.uint32(0xFFFF0000), F32).astype(BF16)
        return lo, hi

    def swiglu(g, u):
        g = jnp.minimum(g, SWIGLU_LIMIT)
        u = jnp.clip(u, -SWIGLU_LIMIT, SWIGLU_LIMIT)
        return (g * jax.nn.sigmoid(SWIGLU_ALPHA * g) * (u + 1.0)).astype(BF16)

    @pl.when(jnp.logical_and(s < n1, rows > 0))
    def _():
        lo, hi = unpack(0)
        g = bg_ref[...]
        u = bu_ref[...]
        kc = min(dh, 512)
        for k0 in range(0, 2 * dh, kc):
            xk = lo[:, k0:k0 + kc] if k0 < dh else hi[:, k0 - dh:k0 - dh + kc]
            wgc = wg_ref[k0:k0 + kc, :].astype(BF16)
            wuc = wu_ref[k0:k0 + kc, :].astype(BF16)
            wgu[0, k0:k0 + kc, :] = wgc
            wgu[1, k0:k0 + kc, :] = wuc
            g = g + _dot(xk, wgc)
            u = u + _dot(xk, wuc)
        act[s, 0:sub, :] = swiglu(g, u)

        def sub_block(i, c):
            lo, hi = unpack(i)
            g = _dot(lo, wgu[0, 0:dh, :]) + _dot(hi, wgu[0, dh:2 * dh, :]) + bg_ref[...]
            u = _dot(lo, wgu[1, 0:dh, :]) + _dot(hi, wgu[1, dh:2 * dh, :]) + bu_ref[...]
            act[s, pl.ds(pl.multiple_of(i * sub, sub), sub), :] = swiglu(g, u)
            return c

        lax.fori_loop(1, n_sub, sub_block, 0)

    @pl.when(jnp.logical_and(s >= n1, rows > 0))
    def _():
        col = pl.multiple_of((s - n1) * dt, dt)

        def out_copy(i, slot):
            row0 = pl.multiple_of(start + i * sub, sub)
            return pltpu.make_async_copy(ostage.at[slot], ys_hbm.at[pl.ds(row0, sub), pl.ds(col, dt)],
                                         osem.at[slot])

        y = bd_ref[...]
        for f in range(n1):
            wdc = wd_ref[f * tf:(f + 1) * tf, :].astype(BF16)
            wdb[f * tf:(f + 1) * tf, :] = wdc
            y = y + _dot(act[f, 0:sub, :], wdc)
        ostage[0] = y
        out_copy(0, 0).start()

        def sub_block(i, c):
            slot = i % 2

            @pl.when(i >= 2)
            def _():
                out_copy(i - 2, slot).wait()

            r0 = pl.multiple_of(i * sub, sub)
            y = bd_ref[...] + _dot(act[0, pl.ds(r0, sub), :], wdb[0:tf, :])
            for f in range(1, n1):
                y = y + _dot(act[f, pl.ds(r0, sub), :], wdb[f * tf:(f + 1) * tf, :])
            ostage[slot] = y
            out_copy(i, slot).start()
            return c

        lax.fori_loop(1, n_sub, sub_block, 0)

        @pl.when(n_sub >= 2)
        def _():
            out_copy(n_sub - 2, n_sub % 2).wait()

        out_copy(n_sub - 1, (n_sub - 1) % 2).wait()


def _experts(h_packed, tables, w_gu, b_gu, w_down, b_down, *, layer, sub, rmax):
    _, row_tok, sb_exp, sb_start, sb_rows, n_used, n_rows, n_sb_max = tables
    n_layers, n_exp, d, f2 = w_gu.shape
    f = f2 // 2
    tf = _tile(f, 256)
    dt = _tile(d, 512)
    n1, n2 = f // tf, d // dt

    def f_idx(sb, s, rows_ref):
        return jnp.where(rows_ref[sb] > 0, jnp.minimum(s, n1 - 1), n1 - 1)

    def d_idx(sb, s, rows_ref):
        return jnp.where(rows_ref[sb] > 0, jnp.maximum(s - n1, 0), n2 - 1)

    in_specs = [
        pl.BlockSpec(memory_space=pl.ANY),
        pl.BlockSpec((None, None, d, tf),
                     lambda sb, s, e, st, rw, tk, us: (layer, e[sb], 0, f_idx(sb, s, rw))),
        pl.BlockSpec((None, None, d, tf),
                     lambda sb, s, e, st, rw, tk, us: (layer, e[sb], 0, n1 + f_idx(sb, s, rw))),
        pl.BlockSpec((None, None, 1, tf),
                     lambda sb, s, e, st, rw, tk, us: (layer, e[sb], 0, f_idx(sb, s, rw))),
        pl.BlockSpec((None, None, 1, tf),
                     lambda sb, s, e, st, rw, tk, us: (layer, e[sb], 0, n1 + f_idx(sb, s, rw))),
        pl.BlockSpec((None, None, f, dt),
                     lambda sb, s, e, st, rw, tk, us: (layer, e[sb], 0, d_idx(sb, s, rw))),
        pl.BlockSpec((None, None, 1, dt),
                     lambda sb, s, e, st, rw, tk, us: (layer, e[sb], 0, d_idx(sb, s, rw))),
    ]
    blk = (rmax * (d // 2) * 4 + rmax * f * 2 + 4 * d * tf * 4 + 2 * f * dt * 4 + 2 * d * tf * 2
           + f * dt * 2 + 2 * sub * dt * 4 + 4 * sub * d * 2)
    return pl.pallas_call(
        functools.partial(_expert_kernel, n1=n1, sub=sub, tf=tf, dt=dt),
        out_shape=jax.ShapeDtypeStruct((n_rows, d), F32),
        grid_spec=pltpu.PrefetchScalarGridSpec(
            num_scalar_prefetch=5,
            grid=(n_sb_max, n1 + n2),
            in_specs=in_specs,
            out_specs=pl.BlockSpec(memory_space=pl.ANY),
            scratch_shapes=[
                pltpu.VMEM((rmax, d // 2), U32),
                pltpu.VMEM((n1, rmax, tf), BF16),
                pltpu.VMEM((2, d, tf), BF16),
                pltpu.VMEM((f, dt), BF16),
                pltpu.VMEM((2, sub, dt), F32),
                pltpu.SemaphoreType.DMA(()),
                pltpu.SemaphoreType.DMA((2,)),
            ]),
        compiler_params=_params(("arbitrary", "arbitrary"), blk),
        name="moe_experts",
    )(sb_exp, sb_start, sb_rows, row_tok, n_used, h_packed, w_gu, w_gu,
      b_gu.reshape(n_layers, n_exp, 1, f2), b_gu.reshape(n_layers, n_exp, 1, f2), w_down,
      b_down.reshape(n_layers, n_exp, 1, d))


def _combine_kernel(dest_ref, ys_hbm, x_ref, g_ref, gate_ref, o_ref, buf, sem, *, tiles_per_batch):
    tm = x_ref.shape[0]
    tok0 = (pl.program_id(0) * tiles_per_batch + pl.program_id(1)) * tm

    def row_copy(r, k):
        src = dest_ref[(tok0 + r) * TOP_K + k]
        return pltpu.make_async_copy(ys_hbm.at[pl.ds(src, 1)], buf.at[k, pl.ds(r, 1)], sem)

    def issue(r, c):
        for k in range(TOP_K):
            row_copy(r, k).start()
        return c

    def drain(r, c):
        for k in range(TOP_K):
            row_copy(r, k).wait()
        return c

    lax.fori_loop(0, tm, issue, 0)
    lax.fori_loop(0, tm, drain, 0)
    gate = gate_ref[...]
    y = gate[:, 0:1] * buf[0]
    for k in range(1, TOP_K):
        y = y + gate[:, k:k + 1] * buf[k]
    o_ref[...] = x_ref[...] + g_ref[...] * y


def _combine(ys, dest, x, g, gate):
    b, t, d = x.shape
    tm = _tile(t, 128)
    tiles_per_batch = t // tm
    row = lambda b_, i, dref: (b_, i, 0)
    if g.shape[1] == 1:
        g_spec = pl.BlockSpec((None, 1, d), lambda b_, i, dref: (b_, 0, 0))
    else:
        g_spec = pl.BlockSpec((None, tm, d), row)
    blk = TOP_K * tm * d * 4 + 6 * tm * d * 4
    return pl.pallas_call(
        functools.partial(_combine_kernel, tiles_per_batch=tiles_per_batch),
        out_shape=jax.ShapeDtypeStruct((b, t, d), F32),
        grid_spec=pltpu.PrefetchScalarGridSpec(
            num_scalar_prefetch=1,
            grid=(b, tiles_per_batch),
            in_specs=[pl.BlockSpec(memory_space=pl.ANY),
                      pl.BlockSpec((None, tm, d), row),
                      g_spec,
                      pl.BlockSpec((None, tm, LANES), row)],
            out_specs=pl.BlockSpec((None, tm, d), row),
            scratch_shapes=[pltpu.VMEM((TOP_K, tm, d), F32), pltpu.SemaphoreType.DMA(())]),
        compiler_params=_params(("arbitrary", "arbitrary"), blk),
        name="moe_combine",
    )(dest.reshape(-1), ys, x, g, gate)


def _moe(xp, xs, g, mods_p, mods_s, w_router, b_router, w_gu, b_gu, w_down, b_down, *, layer):
    (sc_p, sh_p, g_p), (sc_s, sh_s, g_s) = mods_p, mods_s
    d = xp.shape[-1]
    n_exp = w_router.shape[1]
    f = w_down.shape[2]
    hp, ti_p, gt_p = _norm_router(xp, g, sc_p, sh_p, w_router, b_router)
    hs, ti_s, gt_s = _norm_router(xs, g, sc_s, sh_s, w_router, b_router)
    n_p = xp.shape[0] * xp.shape[1]
    h_all = jnp.concatenate([hp.reshape(n_p, d // 2), hs.reshape(-1, d // 2)], axis=0)
    topi = jnp.concatenate([ti_p.reshape(n_p, LANES), ti_s.reshape(-1, LANES)], axis=0)
    sub = _tile(f, 256)
    rmax = 6 * sub
    tables = _dispatch_tables(topi, n_exp, sub, rmax)
    ys = _experts(h_all, tables, w_gu, b_gu, w_down, b_down, layer=layer, sub=sub, rmax=rmax)
    dest = tables[0]
    xp = _combine(ys, dest[:n_p], xp, g_p, gt_p)
    xs = _combine(ys, dest[n_p:], xs, g_s, gt_s)
    return xp, xs


def _nsa_project(h, w_in, q_g, k_g):
    d = h.shape[1]
    hd = LANES
    kvh = N_KV_HEADS
    grp = N_HEADS // kvh
    q_dim = N_HEADS * hd
    kv_dim = 2 * kvh * hd
    q = _mm(h, w_in, col0=0, n=q_dim, epi="qnorm", extra=(q_g.reshape(1, hd),), out_dtype=BF16,
            head_dim=hd, scale=hd ** -0.5, name="nsa_q")
    ones = jnp.ones((kvh * hd,), F32)
    zeros = jnp.zeros((kvh * hd,), F32)
    flags = jnp.concatenate([zeros, zeros, ones, zeros, ones, zeros]).reshape(1, 3 * kv_dim)
    gains = jnp.concatenate([ones, ones, jnp.tile(k_g[1], kvh), ones, jnp.tile(k_g[2], kvh), ones])
    kv = _mm(h, w_in, col0=q_dim, n=3 * kv_dim, epi="kvnorm", extra=(flags, gains.reshape(1, 3 * kv_dim)),
             head_dim=hd, name="nsa_kv")
    w_gate = w_in[:, q_dim + 3 * kv_dim:].reshape(d, 3, kvh, grp)
    w_gate = jnp.transpose(w_gate, (0, 2, 1, 3)).reshape(d, kvh, 3 * grp)
    w_gate = jnp.pad(w_gate, ((0, 0), (0, 0), (0, LANES - 3 * grp))).reshape(d, kvh * LANES)
    gates = _mm(h, w_gate, col0=0, n=kvh * LANES, epi="sigmoid", name="nsa_gates")
    return q, kv, gates


def _cmp_part_kernel(pt_ref, *refs, n_pages, kvh):
    pages = refs[:n_pages * 2 * kvh]
    w_ref, pe_ref, p_out, pe_out, wbf = refs[n_pages * 2 * kvh:]

    @pl.when(pl.program_id(0) == 0)
    def _():
        for j in range(2):
            wbf[j] = w_ref[j].astype(BF16)
            pe_out[j] = _dot(pe_ref[j].astype(BF16), wbf[j])

    rows = n_pages * PAGE_SIZE // CMP_STRIDE
    for j in range(2):
        per_head = []
        for h in range(kvh):
            head_pages = pages[j * kvh + h::2 * kvh]
            pieces = []
            for s in range(CMP_STRIDE):
                pieces.append(jnp.concatenate(
                    [pg[pl.ds(s, PAGE_SIZE // CMP_STRIDE, stride=CMP_STRIDE), :] for pg in head_pages], axis=0))
            per_head.append(jnp.concatenate(pieces, axis=1))
        x = jnp.concatenate(per_head, axis=0).astype(BF16)
        y = _dot(x, wbf[j])
        for h in range(kvh):
            p_out[j, h] = y[h * rows:(h + 1) * rows, :]


def _cmp_out_kernel(pk_ref, pv_ref, b1_ref, pe_ref, w2_ref, b2_ref, kg_ref, kc_ref, vc_ref, shifted):
    n_ch = pk_ref.shape[0]
    hid_dim = pk_ref.shape[1] // 2
    for j, (p_ref, o_ref) in enumerate(((pk_ref, kc_ref), (pv_ref, vc_ref))):
        shifted[0:n_ch, :] = p_ref[:, hid_dim:2 * hid_dim]
        shifted[n_ch:n_ch + SUBLANES, :] = jnp.zeros((SUBLANES, hid_dim), F32)
        hid = (b1_ref[j:j + 1, :] + pe_ref[j, 0:1, 0:hid_dim] + pe_ref[j, 1:2, hid_dim:2 * hid_dim]
               + p_ref[:, 0:hid_dim] + shifted[1:1 + n_ch, :])
        a = hid * jax.nn.sigmoid(hid)
        out = _dot(a.astype(BF16), w2_ref[j].astype(BF16)) + b2_ref[j:j + 1, :]
        if j == 0:
            out = _rms(out) * kg_ref[...]
        o_ref[...] = out.astype(o_ref.dtype)


def _compress(src_pages, page_table, batch, w1, b1, w2, b2, pe, kg):
    kvh = N_KV_HEADS
    hd = LANES
    n_log = page_table.shape[0]
    n_pages = _tile(n_log, 8)
    chunks_per_page = PAGE_SIZE // CMP_STRIDE
    n_chunks = n_log * chunks_per_page
    hid2 = w1.shape[1] * w1.shape[-1]
    kdim = CMP_STRIDE * hd
    w1cat = jnp.transpose(w1, (0, 2, 3, 1, 4)).reshape(2, kdim, hid2)
    pe_rows = jnp.pad(pe.reshape(2, -1, kdim), ((0, 0), (0, SUBLANES - pe.shape[1]), (0, 0)))

    def page_spec(r, j, h):
        return pl.BlockSpec((None, PAGE_SIZE, hd), lambda i, pt: (pt[i * n_pages + r], 0, j * kvh + h))

    page_specs = [page_spec(r, j, h) for r in range(n_pages) for j in range(2) for h in range(kvh)]
    rows = n_pages * chunks_per_page
    blk = (2 * n_pages * PAGE_SIZE * 2 * kvh * hd * 4 + 3 * 2 * kdim * hid2 * 4
           + 4 * kvh * rows * (kdim + hid2) * 4)
    parts, pe_out = pl.pallas_call(
        functools.partial(_cmp_part_kernel, n_pages=n_pages, kvh=kvh),
        out_shape=(jax.ShapeDtypeStruct((2, kvh, n_chunks, hid2), F32),
                   jax.ShapeDtypeStruct((2, SUBLANES, hid2), F32)),
        grid_spec=pltpu.PrefetchScalarGridSpec(
            num_scalar_prefetch=1,
            grid=(n_log // n_pages,),
            in_specs=page_specs + [
                pl.BlockSpec((2, kdim, hid2), lambda i, pt: (0, 0, 0)),
                pl.BlockSpec((2, SUBLANES, kdim), lambda i, pt: (0, 0, 0))],
            out_specs=(pl.BlockSpec((2, kvh, rows, hid2), lambda i, pt: (0, 0, i, 0)),
                       pl.BlockSpec((2, SUBLANES, hid2), lambda i, pt: (0, 0, 0))),
            scratch_shapes=[pltpu.VMEM((2, kdim, hid2), BF16)]),
        compiler_params=_params(("arbitrary",), blk),
        name="cmp_part",
    )(page_table, *([src_pages] * len(page_specs)), w1cat, pe_rows)

    n_ch = n_chunks // batch
    hid = hid2 // 2
    part_spec = lambda j: pl.BlockSpec((None, None, n_ch, hid2), lambda b, h: (j, h, b, 0))
    full = lambda *shape: pl.BlockSpec(shape, lambda b, h: (0,) * len(shape))
    out_spec = pl.BlockSpec((None, None, n_ch, hd), lambda b, h: (b, h, 0, 0))
    return pl.pallas_call(
        _cmp_out_kernel,
        out_shape=(jax.ShapeDtypeStruct((batch, kvh, n_ch, hd), BF16),) * 2,
        grid=(batch, kvh),
        in_specs=[part_spec(0), part_spec(1), full(2, hid), full(2, SUBLANES, hid2), full(2, hid, hd),
                  full(2, hd), full(1, hd)],
        out_specs=(out_spec, out_spec),
        scratch_shapes=[pltpu.VMEM((n_ch + SUBLANES, hid), F32)],
        compiler_params=_params(("arbitrary", "arbitrary"), 12 * n_ch * hid2 * 4),
        name="cmp_out",
    )(parts, parts, b1, pe_out, w2, b2, kg.reshape(1, hd))


def _overlap(n_ch, n_blk):
    c0 = _iota((n_ch, n_blk), 0) * CMP_STRIDE
    b0 = _iota((n_ch, n_blk), 1) * SEL_BLOCK
    return jnp.where(c0 < b0 + SEL_BLOCK, jnp.where(c0 + CMP_BLOCK > b0, 1.0, 0.0), 0.0).astype(BF16)


def _cmp_scores(q, kc, slope, tpos, n_cmp):
    n_ch = kc.shape[0]
    c_idx = _iota((1, n_ch), 1)
    d_c = tpos - (c_idx * CMP_STRIDE + (CMP_BLOCK - 1))
    valid = jnp.where(c_idx < n_cmp, d_c, -1) >= 0
    s = _dot_nt(q, kc) - slope * d_c.astype(F32)
    s = jnp.where(valid, s, -jnp.inf)
    m = jnp.max(s, axis=-1, keepdims=True)
    m = jnp.where(m == -jnp.inf, 0.0, m)
    e = jnp.exp(s - m)
    return e / jnp.maximum(jnp.sum(e, axis=-1, keepdims=True), 1e-30)


def _force_and_mask(imp, tpos):
    blk = _iota((1, imp.shape[1]), 1)
    cur = tpos // SEL_BLOCK
    forced = jnp.logical_or(blk == 0, jnp.logical_or(blk == cur, blk == cur - 1))
    imp = jnp.where(forced, FORCE_SCORE, imp)
    return jnp.where(blk > cur, -jnp.inf, imp)


def _attn_prompt_kernel(slope_ref, q_ref, kc_ref, vc_ref, ks_ref, vs_ref, kw_ref, vw_ref, gt_ref, o_ref,
                        m_s, l_s, acc_s, oc_s, *, grp, seq, tq, tk, n_cmp, n_blk, n_top, wl):
    hk = pl.program_id(1)
    t0 = pl.program_id(2) * tq
    hd = LANES
    row_t = t0 + _iota((tq, 1), 0)
    heads = [(g, slope_ref[hk * grp + g], slice(g * hd, (g + 1) * hd)) for g in range(grp)]

    kc = kc_ref[...]
    vc = vc_ref[...]
    n_ch = kc.shape[0]
    psum = jnp.zeros((tq, n_ch), F32)
    for g, slope, cols in heads:
        p = _cmp_scores(q_ref[:, cols], kc, slope, row_t, n_cmp)
        psum = psum + p
        oc_s[g] = _dot(p.astype(BF16), vc)
    ov = _overlap(n_ch, n_blk)
    p1, p2, p3 = _split3(psum)
    imp = _force_and_mask(_dot(p1, ov) + _dot(p2, ov) + _dot(p3, ov), row_t)
    blk = _iota((1, n_blk), 1)
    rank = jnp.zeros((tq, n_blk), F32)
    for j in range(n_blk):
        cj = imp[:, j:j + 1]
        tie = jnp.where(blk > j, 1.0, 0.0)
        rank = rank + jnp.where(cj > imp, 1.0, jnp.where(cj == imp, tie, 0.0))
    sel = jnp.where(rank < n_top, 1.0, 0.0).astype(BF16)

    m_s[...] = jnp.full(m_s.shape, MASKED, F32)
    l_s[...] = jnp.zeros(l_s.shape, F32)
    acc_s[...] = jnp.zeros(acc_s.shape, F32)

    def kv_tile(kt, carry):
        k0 = pl.multiple_of(kt * tk, tk)
        kk = ks_ref[pl.ds(k0, tk), :].astype(BF16)
        vv = vs_ref[pl.ds(k0, tk), :].astype(BF16)
        expand = jnp.where(_iota((n_blk, tk), 0) == (k0 + _iota((n_blk, tk), 1)) // SEL_BLOCK, 1.0, 0.0)
        picked = _dot(sel, expand.astype(BF16))
        d = row_t - (k0 + _iota((1, tk), 1))
        ok = jnp.where(d >= 0, picked, 0.0) > 0.5
        df = d.astype(F32)
        for g, slope, cols in heads:
            s = jnp.where(ok, _dot_nt(q_ref[:, cols], kk) - slope * df, MASKED)
            m_old = m_s[g]
            m_new = jnp.maximum(m_old, jnp.max(s, axis=-1, keepdims=True))
            alpha = jnp.exp(m_old - m_new)
            p = jnp.exp(s - m_new)
            l_s[g] = alpha * l_s[g] + jnp.sum(p, axis=-1, keepdims=True)
            acc_s[g] = alpha * acc_s[g] + _dot(p.astype(BF16), vv)
            m_s[g] = m_new
        return carry

    lax.fori_loop(0, (t0 + tq + tk - 1) // tk, kv_tile, 0)

    ws = pl.multiple_of(jnp.clip(t0 + tq - wl, 0, seq - wl), SUBLANES)
    kw = kw_ref[pl.ds(ws, wl), :].astype(BF16)
    vw = vw_ref[pl.ds(ws, wl), :].astype(BF16)
    dw = row_t - (ws + _iota((1, wl), 1))
    okw = jnp.where(dw >= 0, jnp.where(dw < WINDOW, 1.0, 0.0), 0.0) > 0.5
    dwf = dw.astype(F32)
    gates = gt_ref[...]
    for g, slope, cols in heads:
        s = jnp.where(okw, _dot_nt(q_ref[:, cols], kw) - slope * dwf, MASKED)
        e = jnp.exp(s - jnp.max(s, axis=-1, keepdims=True))
        o_w = _dot(e.astype(BF16), vw) / jnp.sum(e, axis=-1, keepdims=True)
        o_s = acc_s[g] / l_s[g]
        out = (gates[:, g:g + 1] * oc_s[g] + gates[:, grp + g:grp + g + 1] * o_s
               + gates[:, 2 * grp + g:2 * grp + g + 1] * o_w)
        o_ref[:, cols] = out.astype(o_ref.dtype)


def _attend_prompt(q, kc, vc, kv, gates, slopes, *, batch, seq):
    kvh = N_KV_HEADS
    grp = N_HEADS // kvh
    hd = LANES
    tq = _tile(seq, 256)
    tk = _tile(seq, 512)
    nq = seq // tq
    n_ch = kc.shape[2]
    n_blk = -(-seq // SEL_BLOCK)
    wl = min(tq + WINDOW, seq)
    kv3 = kv.reshape(batch, seq, kv.shape[1])
    row = lambda b, h, i: (b * nq + i, h)
    kv_spec = lambda cb: pl.BlockSpec((None, seq, hd), lambda b, h, i: (b, 0, cb * kvh + h))
    cmp_spec = pl.BlockSpec((None, None, n_ch, hd), lambda b, h, i: (b, h, 0, 0))
    blk = (8 * seq * hd * 4 + 4 * tq * grp * hd * 2 + 3 * grp * tq * hd * 4 + 10 * tq * max(tk, wl) * 4
           + 4 * n_ch * hd * 2)
    return pl.pallas_call(
        functools.partial(_attn_prompt_kernel, grp=grp, seq=seq, tq=tq, tk=tk, n_cmp=n_ch - 1, n_blk=n_blk,
                          n_top=min(N_SEL, n_blk), wl=wl),
        out_shape=jax.ShapeDtypeStruct((batch * seq, N_HEADS * hd), BF16),
        grid=(batch, kvh, nq),
        in_specs=[pl.BlockSpec(memory_space=pltpu.SMEM),
                  pl.BlockSpec((tq, grp * hd), row), cmp_spec, cmp_spec,
                  kv_spec(2), kv_spec(3), kv_spec(4), kv_spec(5),
                  pl.BlockSpec((tq, LANES), row)],
        out_specs=pl.BlockSpec((tq, grp * hd), row),
        scratch_shapes=[pltpu.VMEM((grp, tq, 1), F32), pltpu.VMEM((grp, tq, 1), F32),
                        pltpu.VMEM((grp, tq, hd), F32), pltpu.VMEM((grp, tq, hd), F32)],
        compiler_params=_params(("arbitrary", "arbitrary", "arbitrary"), blk),
        name="nsa_attend_prompt",
    )(slopes, q, kc, vc, kv3, kv3, kv3, kv3, gates)


def _attn_sample_kernel(pt_ref, slope_ref, q_ref, kc_ref, vc_ref, *rest, n_pages, grp, dec_seq, past, n_cmp,
                        n_blk, n_top, nb_pad):
    k_pages = rest[:n_pages]
    v_pages = rest[n_pages:2 * n_pages]
    (ksn_ref, vsn_ref, kwn_ref, vwn_ref, kwc_ref, vwc_ref, gt_ref, o_ref,
     m_s, l_s, acc_s, oc_s, sel_s) = rest[2 * n_pages:]
    hk = pl.program_id(1)
    c = pl.program_id(2)
    rows = q_ref.shape[0]
    q = q_ref[...]
    row = _iota((rows, 1), 0)
    t_row = row // grp
    g_row = row % grp
    slope = jnp.zeros((rows, 1), F32)
    for g in range(grp):
        slope = jnp.where(g_row == g, slope_ref[hk * grp + g], slope)
    tpos = past + t_row

    def online_update(s, v):
        m_old = m_s[...]
        m_new = jnp.maximum(m_old, jnp.max(s, axis=-1, keepdims=True))
        alpha = jnp.exp(m_old - m_new)
        p = jnp.exp(s - m_new)
        l_s[...] = alpha * l_s[...] + jnp.sum(p, axis=-1, keepdims=True)
        acc_s[...] = alpha * acc_s[...] + _dot(p.astype(BF16), v)
        m_s[...] = m_new

    @pl.when(c == 0)
    def _():
        kc = kc_ref[...]
        n_ch = kc.shape[0]
        p = _cmp_scores(q, kc, slope, tpos, n_cmp)
        oc_s[...] = _dot(p.astype(BF16), vc_ref[...])
        ov = _overlap(n_ch, nb_pad)
        p1, p2, p3 = _split3(p)
        per_head = _dot(p1, ov) + _dot(p2, ov) + _dot(p3, ov)
        same_tok = jnp.where(_iota((rows, rows), 0) // grp == _iota((rows, rows), 1) // grp, 1.0, 0.0).astype(BF16)
        a1, a2, a3 = _split3(per_head)
        imp = _force_and_mask(_dot(same_tok, a1) + _dot(same_tok, a2) + _dot(same_tok, a3), tpos)
        blk_f = _iota((1, nb_pad), 1).astype(F32)
        taken = jnp.where(blk_f >= n_blk, 1.0, 0.0) + jnp.zeros((rows, nb_pad), F32)
        sel = jnp.zeros((rows, nb_pad), F32)
        for _ in range(n_top):
            avail = jnp.where(taken > 0.5, -jnp.inf, imp)
            best = jnp.max(avail, axis=-1, keepdims=True)
            cand = jnp.where(taken > 0.5, 0.0, jnp.where(avail == best, 1.0, 0.0))
            idx = jnp.min(jnp.where(cand > 0.5, blk_f, float(nb_pad)), axis=-1, keepdims=True)
            pick = blk_f == idx
            taken = jnp.where(pick, 1.0, taken)
            sel = jnp.where(pick, 1.0, sel)
        sel_s[...] = sel
        m_s[...] = jnp.full(m_s.shape, MASKED, F32)
        l_s[...] = jnp.zeros(l_s.shape, F32)
        acc_s[...] = jnp.zeros(acc_s.shape, F32)

    tk = n_pages * PAGE_SIZE
    k0 = c * tk
    kk = jnp.concatenate([pg[...] for pg in k_pages], axis=0).astype(BF16)
    vv = jnp.concatenate([pg[...] for pg in v_pages], axis=0).astype(BF16)
    expand = jnp.where(_iota((nb_pad, tk), 0) == (k0 + _iota((nb_pad, tk), 1)) // SEL_BLOCK, 1.0, 0.0)
    picked = _dot(sel_s[...].astype(BF16), expand.astype(BF16))
    d = tpos - (k0 + _iota((1, tk), 1))
    ok = jnp.where(d >= 0, picked, 0.0) > 0.5
    online_update(jnp.where(ok, _dot_nt(q, kk) - slope * d.astype(F32), MASKED), vv)

    @pl.when(c == pl.num_programs(2) - 1)
    def _():
        t_new = _iota((1, ksn_ref.shape[0]), 1)
        dn = t_row - t_new
        causal_new = jnp.where(dn >= 0, jnp.where(t_new < dec_seq, 1.0, 0.0), 0.0)
        ok_new = causal_new > 0.5
        dnf = dn.astype(F32)
        online_update(jnp.where(ok_new, _dot_nt(q, ksn_ref[...].astype(BF16)) - slope * dnf, MASKED),
                      vsn_ref[...].astype(BF16))
        o_s = acc_s[...] / l_s[...]
        n_buf = kwc_ref.shape[0]
        dc = tpos - (past - n_buf + _iota((1, n_buf), 1))
        ok_c = jnp.where(dc >= 0, jnp.where(dc < WINDOW, 1.0, 0.0), 0.0) > 0.5
        s_c = jnp.where(ok_c, _dot_nt(q, kwc_ref[...].astype(BF16)) - slope * dc.astype(F32), MASKED)
        ok_n = jnp.where(dn < WINDOW, causal_new, 0.0) > 0.5
        s_n = jnp.where(ok_n, _dot_nt(q, kwn_ref[...].astype(BF16)) - slope * dnf, MASKED)
        m = jnp.maximum(jnp.max(s_c, axis=-1, keepdims=True), jnp.max(s_n, axis=-1, keepdims=True))
        e_c = jnp.exp(s_c - m)
        e_n = jnp.exp(s_n - m)
        den = jnp.sum(e_c, axis=-1, keepdims=True) + jnp.sum(e_n, axis=-1, keepdims=True)
        o_w = (_dot(e_c.astype(BF16), vwc_ref[...].astype(BF16))
               + _dot(e_n.astype(BF16), vwn_ref[...].astype(BF16))) / den
        gates = gt_ref[...]
        o_ref[...] = (gates[:, 0:1] * oc_s[...] + gates[:, 1:2] * o_s + gates[:, 2:3] * o_w).astype(o_ref.dtype)


def _attend_sample(q_rows, kc, vc, cache_sel, page_table, kv_new, cache_win, gate_rows, slopes, *, past, dec_seq):
    batch, kvh, rows, hd = q_rows.shape
    grp = N_HEADS // kvh
    assert past % SEL_BLOCK == 0 and dec_seq <= SEL_BLOCK and past % PAGE_SIZE == 0
    pages_per_batch = past // PAGE_SIZE
    n_pages = _tile(pages_per_batch, 8)
    n_ch = kc.shape[2]
    n_blk = -(-(past + dec_seq) // SEL_BLOCK)
    nb_pad = -(-n_blk // LANES) * LANES
    n_buf = cache_win.shape[1]
    new_rows = kv_new.shape[1]

    def page_spec(r, j):
        return pl.BlockSpec(
            (None, PAGE_SIZE, hd),
            lambda b, h, c, pt: (pt[b * pages_per_batch + c * n_pages + r], 0, j * kvh + h))

    bh = lambda b, h, c, pt: (b, h, 0, 0)
    new_spec = lambda cb: pl.BlockSpec((None, new_rows, hd), lambda b, h, c, pt: (b, 0, cb * kvh + h))
    win_spec = lambda j: pl.BlockSpec((None, n_buf, hd), lambda b, h, c, pt: (b, 0, j * kvh + h))
    cmp_spec = pl.BlockSpec((None, None, n_ch, hd), bh)
    row_spec = pl.BlockSpec((None, None, rows, hd), bh)
    tk = n_pages * PAGE_SIZE
    blk = (4 * n_pages * PAGE_SIZE * hd * 4 + 4 * n_ch * hd * 2 + 4 * n_buf * hd * 4
           + 12 * rows * max(tk, n_ch, nb_pad) * 4 + 2 * nb_pad * max(tk, n_ch) * 4)
    return pl.pallas_call(
        functools.partial(_attn_sample_kernel, n_pages=n_pages, grp=grp, dec_seq=dec_seq, past=past,
                          n_cmp=n_ch - 1, n_blk=n_blk, n_top=min(N_SEL, n_blk), nb_pad=nb_pad),
        out_shape=jax.ShapeDtypeStruct((batch, kvh, rows, hd), BF16),
        grid_spec=pltpu.PrefetchScalarGridSpec(
            num_scalar_prefetch=1,
            grid=(batch, kvh, pages_per_batch // n_pages),
            in_specs=[pl.BlockSpec(memory_space=pltpu.SMEM), row_spec, cmp_spec, cmp_spec]
            + [page_spec(r, 0) for r in range(n_pages)] + [page_spec(r, 1) for r in range(n_pages)]
            + [new_spec(2), new_spec(3), new_spec(4), new_spec(5), win_spec(0), win_spec(1), row_spec],
            out_specs=row_spec,
            scratch_shapes=[pltpu.VMEM((rows, 1), F32), pltpu.VMEM((rows, 1), F32), pltpu.VMEM((rows, hd), F32),
                            pltpu.VMEM((rows, hd), F32), pltpu.VMEM((rows, nb_pad), F32)]),
        compiler_params=_params(("arbitrary", "arbitrary", "arbitrary"), blk),
        name="nsa_attend_sample",
    )(page_table, slopes, q_rows, kc, vc, *([cache_sel] * (2 * n_pages)), kv_new, kv_new, kv_new, kv_new,
      cache_win, cache_win, gate_rows)


def _nsa_layer(hp, hs, xp, xs, g1p, g1s, caches, page_table, weights, slopes, *, layer, batch, seq, dec_batch,
               dec_seq):
    (w_in, w_out, q_g, k_g, pe, w1, b1, w2, b2) = weights
    cache_cmp, cache_sel, cache_win = caches
    d = xp.shape[-1]
    kvh = N_KV_HEADS
    grp = N_HEADS // kvh
    hd = LANES
    cols = 2 * kvh * hd
    n_s = dec_batch * dec_seq
    past = page_table.shape[1] * PAGE_SIZE
    assert seq % PAGE_SIZE == 0 and past % CMP_STRIDE == 0 and dec_seq < CMP_STRIDE
    cmp_w = (w1, b1, w2, b2, pe, k_g[0])

    q, kv, gates = _nsa_project(hp, w_in, q_g, k_g)
    n_pages_p = batch * seq // PAGE_SIZE
    kc, vc = _compress(kv.reshape(n_pages_p, PAGE_SIZE, 3 * cols), jnp.arange(n_pages_p, dtype=I32), batch, *cmp_w)
    o = _attend_prompt(q, kc, vc, kv, gates, slopes, batch=batch, seq=seq)
    xp = _mm(o, w_out, col0=0, n=d, epi="res", extra=(xp.reshape(batch * seq, d), g1p), rows_per_batch=seq,
             name="nsa_out").reshape(batch, seq, d)
    kv_p = kv.reshape(batch, seq, 3, 2, kvh, hd)
    new_p = (kv_p[:, :, 0], kv_p[:, :, 1], kv_p[:, seq - min(WINDOW, seq):, 2])

    qs, kvs, gs = _nsa_project(hs, w_in, q_g, k_g)
    pt = page_table.reshape(-1)
    n_phys = cache_cmp.shape[1]
    kcs, vcs = _compress(cache_cmp[layer].reshape(n_phys, PAGE_SIZE, cols), pt, dec_batch, *cmp_w)
    q_rows = jnp.transpose(qs.reshape(dec_batch, dec_seq, kvh, grp, hd), (0, 2, 1, 3, 4))
    q_rows = q_rows.reshape(dec_batch, kvh, dec_seq * grp, hd)
    gate_rows = gs.reshape(dec_batch, dec_seq, kvh, LANES)[..., :3 * grp].reshape(dec_batch, dec_seq, kvh, 3, grp)
    gate_rows = jnp.transpose(gate_rows, (0, 2, 1, 4, 3)).reshape(dec_batch, kvh, dec_seq * grp, 3)
    gate_rows = jnp.pad(gate_rows, ((0, 0), (0, 0), (0, 0), (0, LANES - 3)))
    kv_new = jnp.pad(kvs.reshape(dec_batch, dec_seq, 3 * cols), ((0, 0), (0, SUBLANES - dec_seq), (0, 0)))
    n_buf = cache_win.shape[2]
    o_rows = _attend_sample(q_rows, kcs, vcs, cache_sel[layer].reshape(n_phys, PAGE_SIZE, cols), pt, kv_new,
                            cache_win[layer].reshape(dec_batch, n_buf, cols), gate_rows, slopes, past=past,
                            dec_seq=dec_seq)
    o_s = jnp.transpose(o_rows.reshape(dec_batch, kvh, dec_seq, grp, hd), (0, 2, 1, 3, 4)).reshape(n_s, N_HEADS * hd)
    xs = _mm(o_s, w_out, col0=0, n=d, epi="res", extra=(xs.reshape(n_s, d), g1s[0]),
             name="nsa_out").reshape(1, n_s, d)
    kv_s = kvs.reshape(dec_batch, dec_seq, 3, 2, kvh, hd)
    win_s = jnp.concatenate([cache_win[layer], kv_s[:, :, 2]], axis=1)[:, -n_buf:]
    new_s = (kv_s[:, :, 0], kv_s[:, :, 1], win_s)
    return xp, xs, new_p, new_s


def kernel(x_prompt, x_sample, c_prompt, c_sample, state_conv, cache_cmp_kv, cache_sel_kv, cache_win_kv,
           page_table, w_mod, b_mod, norm_g, conv_w_in, conv_w, conv_w_out, nsa_w_in, nsa_w_out, q_norm_g,
           k_norm_g, cmp_pe, cmp_w1, cmp_b1, cmp_w2, cmp_b2, router_w, router_b, moe_w_gu, moe_b_gu,
           moe_w_down, moe_b_down):
    batch, seq, d = x_prompt.shape
    dec_batch, dec_seq, _ = x_sample.shape
    depth = w_mod.shape[0]
    n_s = dec_batch * dec_seq
    assert dec_seq >= conv_w.shape[1] - 1

    c_all = jnp.concatenate([c_prompt, c_sample], axis=0)
    c_all = jnp.pad(c_all, ((0, -c_all.shape[0] % SUBLANES), (0, 0)))
    mod = _adaln(c_all, w_mod, b_mod)
    slopes = jnp.exp2(-8.0 * jnp.arange(1, N_HEADS + 1, dtype=F32) / N_HEADS)

    xp = x_prompt
    xs = x_sample.reshape(1, n_s, d)
    conv_p, conv_s, cmp_p, cmp_s, sel_p, sel_s, win_p, win_s = [], [], [], [], [], [], [], []
    for i in range(depth):
        j = i // 2
        sh1p, sc1p, g1p, sh2p, sc2p, g2p = [m[:, None, :] for m in jnp.split(mod[i, :batch], 6, axis=-1)]
        sh1s, sc1s, g1s, sh2s, sc2s, g2s = [jnp.repeat(m, dec_seq, axis=0)[None]
                                            for m in jnp.split(mod[i, batch:batch + dec_batch], 6, axis=-1)]
        hp = _norm_mod(xp, norm_g[i, 0], sc1p, sh1p).reshape(batch * seq, d)
        hs = _norm_mod(xs, norm_g[i, 0], sc1s, sh1s).reshape(n_s, d)
        if i % 2 == 0:
            a_p, state_p = _conv_in(hp, conv_w_in[j], conv_w[j], batch=batch, seq=seq)
            a_s, v_s = _conv_in(hs, conv_w_in[j], conv_w[j], batch=dec_batch, seq=dec_seq,
                                prev=_conv_prev_rows(state_conv[j], dec_seq))
            xp = _mm(a_p, conv_w_out[j], col0=0, n=d, epi="res", extra=(xp.reshape(batch * seq, d), g1p),
                     rows_per_batch=seq, name="conv_out").reshape(batch, seq, d)
            xs = _mm(a_s, conv_w_out[j], col0=0, n=d, epi="res", extra=(xs.reshape(n_s, d), g1s[0]),
                     name="conv_out").reshape(1, n_s, d)
            conv_p.append(state_p)
            conv_s.append(v_s.reshape(dec_batch, dec_seq, d)[:, dec_seq - 2:])
        else:
            weights = (nsa_w_in[j], nsa_w_out[j], q_norm_g[j], k_norm_g[j], cmp_pe[j], cmp_w1[j], cmp_b1[j],
                       cmp_w2[j], cmp_b2[j])
            xp, xs, new_p, new_s = _nsa_layer(
                hp, hs, xp, xs, g1p, g1s, (cache_cmp_kv, cache_sel_kv, cache_win_kv), page_table,
                weights, slopes, layer=j, batch=batch, seq=seq, dec_batch=dec_batch, dec_seq=dec_seq)
            cmp_p.append(new_p[0])
            sel_p.append(new_p[1])
            win_p.append(new_p[2])
            cmp_s.append(new_s[0])
            sel_s.append(new_s[1])
            win_s.append(new_s[2])
        xp, xs = _moe(xp, xs, norm_g[i, 1], (sc2p, sh2p, g2p), (sc2s, sh2s, g2s), router_w[i], router_b[i],
                      moe_w_gu, moe_b_gu, moe_w_down, moe_b_down, layer=i)
    return (xp, xs.reshape(dec_batch, dec_seq, d), jnp.stack(conv_p), jnp.stack(conv_s), jnp.stack(cmp_p),
            jnp.stack(cmp_s), jnp.stack(sel_p), jnp.stack(sel_s), jnp.stack(win_p), jnp.stack(win_s))
```

```python
import functools

import jax
import jax.numpy as jnp
from jax import lax
from jax.experimental import pallas as pl
from jax.experimental.pallas import tpu as pltpu

F32 = jnp.float32
BF16 = jnp.bfloat16
I32 = jnp.int32
U32 = jnp.uint32

N_HEADS = 32
N_KV_HEADS = 4
CMP_BLOCK = 32
CMP_STRIDE = 16
SEL_BLOCK = 64
N_SEL = 16
WINDOW = 512
PAGE_SIZE = 128
TOP_K = 4
SWIGLU_LIMIT = 7.0
SWIGLU_ALPHA = 1.702
EPS = 1e-6
FORCE_SCORE = 1e4
MASKED = -1e30
LOG2_E = 1.4426950408889634
MOE_SUB_ROWS = 272
MOE_SUBS_PER_SUPERBLOCK = 5

LANES = 128
SUBLANES = 8
VMEM_PHYSICAL_BYTES = 64 * 1024 * 1024
VMEM_CAP_BYTES = VMEM_PHYSICAL_BYTES - 6 * 1024 * 1024


def _vmem_limit(block_bytes):
    return int(min(VMEM_CAP_BYTES, block_bytes * 5 // 4 + (4 << 20)))


def _params(sem, block_bytes):
    return pltpu.CompilerParams(dimension_semantics=sem, vmem_limit_bytes=_vmem_limit(block_bytes))


def _tile(n, pref):
    if n <= pref:
        return n
    t = pref
    while n % t:
        t //= 2
    return t


def _dot(a, b):
    return jnp.dot(a, b, preferred_element_type=F32)


def _dot_nt(a, b):
    return lax.dot_general(a, b, (((1,), (1,)), ((), ())), preferred_element_type=F32)


def _split3(x):
    hi = x.astype(BF16)
    r = x - hi.astype(F32)
    mid = r.astype(BF16)
    lo = (r - mid.astype(F32)).astype(BF16)
    return hi, mid, lo


def _iota(shape, dim):
    return lax.broadcasted_iota(I32, shape, dim)


def _rms(a):
    return a * lax.rsqrt(jnp.mean(a * a, axis=-1, keepdims=True) + EPS)


def _adaln_kernel(c_ref, w_ref, b_ref, o_ref, *, kc):
    c = c_ref[...]
    a = (c * jax.nn.sigmoid(c)).astype(BF16)
    acc = jnp.zeros(o_ref.shape, F32)
    for k0 in range(0, a.shape[1], kc):
        acc = acc + _dot(a[:, k0:k0 + kc], w_ref[k0:k0 + kc, :].astype(BF16))
    o_ref[...] = acc + b_ref[...]


def _adaln(c_all, w_mod, b_mod):
    n_layers, d, n6 = w_mod.shape
    r = c_all.shape[0]
    tn = _tile(n6, 1024)
    kc = _tile(d, 1024)
    blk = 2 * d * tn * 4 + d * tn * 2 + r * d * 4
    return pl.pallas_call(
        functools.partial(_adaln_kernel, kc=kc),
        out_shape=jax.ShapeDtypeStruct((n_layers, r, n6), F32),
        grid=(n_layers, n6 // tn),
        in_specs=[
            pl.BlockSpec((r, d), lambda l, j: (0, 0)),
            pl.BlockSpec((None, d, tn), lambda l, j: (l, 0, j)),
            pl.BlockSpec((None, 1, tn), lambda l, j: (l, 0, j)),
        ],
        out_specs=pl.BlockSpec((None, r, tn), lambda l, j: (l, 0, j)),
        compiler_params=_params(("arbitrary", "arbitrary"), blk),
        name="adaln_mod",
    )(c_all, w_mod, b_mod.reshape(n_layers, 1, n6))


def _modulated(x_ref, g_ref, sc_ref, sh_ref):
    return _rms(x_ref[...]) * g_ref[...] * (1.0 + sc_ref[...]) + sh_ref[...]


def _norm_mod_kernel(x_ref, g_ref, sc_ref, sh_ref, h_ref):
    h_ref[...] = _modulated(x_ref, g_ref, sc_ref, sh_ref).astype(h_ref.dtype)


def _mod_spec(mod, tm, d):
    if mod.shape[1] == 1:
        return pl.BlockSpec((None, 1, d), lambda b, i: (b, 0, 0))
    return pl.BlockSpec((None, tm, d), lambda b, i: (b, i, 0))


def _norm_mod(x, g, scale, shift):
    b, t, d = x.shape
    tm = _tile(t, 512)
    blk = 2 * tm * d * (4 + 2) + 6 * d * 4 + 2 * tm * d * 4
    return pl.pallas_call(
        _norm_mod_kernel,
        out_shape=jax.ShapeDtypeStruct((b, t, d), BF16),
        grid=(b, t // tm),
        in_specs=[
            pl.BlockSpec((None, tm, d), lambda b_, i: (b_, i, 0)),
            pl.BlockSpec((1, d), lambda b_, i: (0, 0)),
            _mod_spec(scale, tm, d),
            _mod_spec(shift, tm, d),
        ],
        out_specs=pl.BlockSpec((None, tm, d), lambda b_, i: (b_, i, 0)),
        compiler_params=_params(("arbitrary", "arbitrary"), blk),
        name="norm_mod",
    )(x, g.reshape(1, d), scale, shift)


def _norm_router_kernel(x_ref, g_ref, sc_ref, sh_ref, wr_ref, br_ref, hp_ref, ti_ref, gt_ref):
    h = _modulated(x_ref, g_ref, sc_ref, sh_ref)
    tm, d = h.shape
    dh = d // 2
    lo = pltpu.bitcast(h[:, :dh].astype(BF16).astype(F32), U32)
    hi = pltpu.bitcast(h[:, dh:].astype(BF16).astype(F32), U32)
    hp_ref[...] = (lo >> 16) | hi

    h1, h2, h3 = _split3(h)
    w1, w2, w3 = _split3(wr_ref[...])
    logits = (_dot(h1, w1) + (_dot(h1, w2) + _dot(h2, w1))
              + (_dot(h2, w2) + _dot(h1, w3) + _dot(h3, w1))) + br_ref[...]
    n_exp = logits.shape[1]
    lane = _iota(logits.shape, 1).astype(F32)
    work = logits
    vals, idxs = [], []
    for _ in range(TOP_K):
        m = jnp.max(work, axis=-1, keepdims=True)
        idx = jnp.min(jnp.where(work == m, lane, float(n_exp)), axis=-1, keepdims=True)
        vals.append(m)
        idxs.append(idx)
        work = jnp.where(lane == idx, -jnp.inf, work)
    es = [jnp.exp(v - vals[0]) for v in vals]
    den = es[0]
    for e in es[1:]:
        den = den + e
    lane_o = _iota((tm, LANES), 1)
    ti = jnp.zeros((tm, LANES), F32)
    gt = jnp.zeros((tm, LANES), F32)
    for k in range(TOP_K):
        ti = jnp.where(lane_o == k, idxs[k], ti)
        gt = jnp.where(lane_o == k, es[k] / den, gt)
    ti_ref[...] = ti.astype(I32)
    gt_ref[...] = gt


def _norm_router(x, g, scale, shift, w_router, b_router):
    b, t, d = x.shape
    n_exp = w_router.shape[1]
    tm = _tile(t, 256)
    blk = 2 * tm * d * 4 + 2 * tm * d * 2 + 8 * tm * d * 4 + 2 * d * n_exp * 4
    row = lambda b_, i: (b_, i, 0)
    return pl.pallas_call(
        _norm_router_kernel,
        out_shape=(jax.ShapeDtypeStruct((b, t, d // 2), U32),
                   jax.ShapeDtypeStruct((b, t, LANES), I32),
                   jax.ShapeDtypeStruct((b, t, LANES), F32)),
        grid=(b, t // tm),
        in_specs=[
            pl.BlockSpec((None, tm, d), row),
            pl.BlockSpec((1, d), lambda b_, i: (0, 0)),
            _mod_spec(scale, tm, d),
            _mod_spec(shift, tm, d),
            pl.BlockSpec((d, n_exp), lambda b_, i: (0, 0)),
            pl.BlockSpec((1, n_exp), lambda b_, i: (0, 0)),
        ],
        out_specs=(pl.BlockSpec((None, tm, d // 2), row),
                   pl.BlockSpec((None, tm, LANES), row),
                   pl.BlockSpec((None, tm, LANES), row)),
        compiler_params=_params(("arbitrary", "arbitrary"), blk),
        name="norm_router",
    )(x, g.reshape(1, d), scale, shift, w_router, b_router.reshape(1, n_exp))


def _mm_kernel(a_ref, w_ref, *rest, epi, head_dim, scale):
    *ins, o_ref, wb = rest

    @pl.when(pl.program_id(1) == 0)
    def _():
        wb[...] = w_ref[...].astype(BF16)

    acc = _dot(a_ref[...], wb[...])
    tn = acc.shape[1]
    if epi == "res":
        x_ref, g_ref = ins
        o_ref[...] = x_ref[...] + g_ref[...] * acc
    elif epi == "qnorm":
        (gq_ref,) = ins
        for c in range(tn // head_dim):
            a = acc[:, c * head_dim:(c + 1) * head_dim]
            o_ref[:, c * head_dim:(c + 1) * head_dim] = (_rms(a) * gq_ref[...] * scale).astype(o_ref.dtype)
    elif epi == "kvnorm":
        fl_ref, gk_ref = ins
        for c in range(tn // head_dim):
            sl = slice(c * head_dim, (c + 1) * head_dim)
            a = acc[:, sl]
            o_ref[:, sl] = jnp.where(fl_ref[:, sl] > 0.5, _rms(a) * gk_ref[:, sl], a)
    elif epi == "sigmoid":
        o_ref[...] = jax.nn.sigmoid(acc)
    else:
        raise ValueError(epi)


def _mm(a, w, *, col0, n, epi, extra=(), rows_per_batch=None, out_dtype=F32, head_dim=LANES, scale=1.0,
        name="mm"):
    m, k = a.shape
    tm = _tile(m, 512)
    tn = _tile(n, 512)
    assert col0 % tn == 0
    j0 = col0 // tn
    in_specs = [pl.BlockSpec((tm, k), lambda j, i: (i, 0)),
                pl.BlockSpec((k, tn), lambda j, i: (0, j + j0))]
    operands = [a, w]
    if epi == "res":
        x, g = extra
        in_specs.append(pl.BlockSpec((tm, tn), lambda j, i: (i, j)))
        if g.ndim == 3:
            assert rows_per_batch % tm == 0
            in_specs.append(pl.BlockSpec((None, 1, tn), lambda j, i: (i * tm // rows_per_batch, 0, j)))
        else:
            in_specs.append(pl.BlockSpec((tm, tn), lambda j, i: (i, j)))
        operands += [x, g]
    elif epi == "qnorm":
        in_specs.append(pl.BlockSpec((1, head_dim), lambda j, i: (0, 0)))
        operands += list(extra)
    elif epi == "kvnorm":
        in_specs += [pl.BlockSpec((1, tn), lambda j, i: (0, j))] * 2
        operands += list(extra)
    blk = 2 * tm * k * 2 + 2 * k * tn * 4 + k * tn * 2 + 6 * tm * tn * 4
    return pl.pallas_call(
        functools.partial(_mm_kernel, epi=epi, head_dim=head_dim, scale=scale),
        out_shape=jax.ShapeDtypeStruct((m, n), out_dtype),
        grid=(n // tn, m // tm),
        in_specs=in_specs,
        out_specs=pl.BlockSpec((tm, tn), lambda j, i: (i, j)),
        scratch_shapes=[pltpu.VMEM((k, tn), BF16)],
        compiler_params=_params(("arbitrary", "arbitrary"), blk),
        name=name,
    )(*operands)


def _conv_in_kernel(a_ref, wb_ref, wc_ref, wu_ref, cw_ref, *rest, tiles_per_batch, seq, per_token_prev):
    if per_token_prev:
        p1_ref, p2_ref, o_ref, v_ref, wbuf, ext = rest
    else:
        o_ref, st_ref, wbuf, ext = rest
    i = pl.program_id(1)

    @pl.when(i == 0)
    def _():
        wbuf[0] = wb_ref[...].astype(BF16)
        wbuf[1] = wc_ref[...].astype(BF16)
        wbuf[2] = wu_ref[...].astype(BF16)

    a = a_ref[...]
    b_gate = _dot(a, wbuf[0])
    v = _dot(a, wbuf[1]) * _dot(a, wbuf[2])
    tm, tn = v.shape

    if per_token_prev:
        ext[0:SUBLANES, :] = jnp.zeros((SUBLANES, tn), F32)
    else:
        @pl.when(i % tiles_per_batch == 0)
        def _():
            ext[0:SUBLANES, :] = jnp.zeros((SUBLANES, tn), F32)

    ext[SUBLANES:SUBLANES + tm, :] = v
    s1 = ext[SUBLANES - 1:SUBLANES - 1 + tm, :]
    s2 = ext[SUBLANES - 2:SUBLANES - 2 + tm, :]
    if per_token_prev:
        tpos = _iota((tm, 1), 0) % seq
        s1 = jnp.where(tpos >= 1, s1, p1_ref[...])
        s2 = jnp.where(tpos >= 2, s2, p2_ref[...])
    cw = cw_ref[...]
    conv = s2 * cw[0:1, :] + s1 * cw[1:2, :] + v * cw[2:3, :]
    o_ref[...] = (b_gate * conv).astype(o_ref.dtype)

    if per_token_prev:
        v_ref[...] = v
    else:
        ext[0:SUBLANES, :] = ext[tm:tm + SUBLANES, :]

        @pl.when(i % tiles_per_batch == tiles_per_batch - 1)
        def _():
            st_ref[...] = ext[tm + SUBLANES - 2:tm + SUBLANES, :]


def _conv_in(h, w_in, conv_w, *, batch, seq, prev=None):
    m, d = h.shape
    tn = _tile(d, 256)
    nd = d // tn
    per_token_prev = prev is not None
    tm = m if per_token_prev else _tile(seq, 512)
    tiles_per_batch = max(seq // tm, 1)
    in_specs = [pl.BlockSpec((tm, d), lambda j, i: (i, 0)),
                pl.BlockSpec((d, tn), lambda j, i: (0, j)),
                pl.BlockSpec((d, tn), lambda j, i: (0, j + nd)),
                pl.BlockSpec((d, tn), lambda j, i: (0, j + 2 * nd)),
                pl.BlockSpec((3, tn), lambda j, i: (0, j))]
    operands = [h, w_in, w_in, w_in, conv_w]
    tile_spec = pl.BlockSpec((tm, tn), lambda j, i: (i, j))
    if per_token_prev:
        in_specs += [tile_spec, tile_spec]
        operands += list(prev)
        out_shape = (jax.ShapeDtypeStruct((m, d), BF16), jax.ShapeDtypeStruct((m, d), F32))
        out_specs = (tile_spec, tile_spec)
    else:
        out_shape = (jax.ShapeDtypeStruct((m, d), BF16), jax.ShapeDtypeStruct((batch, 2, d), F32))
        out_specs = (tile_spec, pl.BlockSpec((None, 2, tn), lambda j, i: (i // tiles_per_batch, 0, j)))
    blk = 2 * tm * d * 2 + 6 * d * tn * 4 + 3 * d * tn * 2 + 10 * tm * tn * 4
    return pl.pallas_call(
        functools.partial(_conv_in_kernel, tiles_per_batch=tiles_per_batch, seq=seq,
                          per_token_prev=per_token_prev),
        out_shape=out_shape,
        grid=(nd, m // tm),
        in_specs=in_specs,
        out_specs=out_specs,
        scratch_shapes=[pltpu.VMEM((3, d, tn), BF16), pltpu.VMEM((tm + 2 * SUBLANES, tn), F32)],
        compiler_params=_params(("arbitrary", "arbitrary"), blk),
        name="conv_in",
    )(*operands)


def _conv_prev_rows(state, seq):
    b, _, d = state.shape
    zeros = jnp.zeros((b, seq, d), state.dtype)
    p1 = zeros.at[:, 0].set(state[:, 1])
    p2 = zeros.at[:, 0].set(state[:, 0]).at[:, 1].set(state[:, 1])
    return p1.reshape(b * seq, d), p2.reshape(b * seq, d)


def _plan_kernel(ti_ref, pos_ref, cnt_ref, carry):
    @pl.when(pl.program_id(0) == 0)
    def _():
        carry[...] = jnp.zeros(carry.shape, F32)

    ti = ti_ref[...]
    tm = ti.shape[0]
    e_iota = _iota((tm, LANES), 1)
    onehots = [jnp.where(ti[:, k:k + 1] == e_iota, 1.0, 0.0) for k in range(TOP_K)]
    hits = onehots[0]
    for oh in onehots[1:]:
        hits = hits + oh
    strictly_lower = jnp.where(_iota((tm, tm), 0) > _iota((tm, tm), 1), 1.0, 0.0).astype(BF16)
    before = _dot(strictly_lower, hits.astype(BF16)) + carry[...]
    out = jnp.zeros((tm, LANES), F32)
    for k in range(TOP_K):
        out = jnp.where(e_iota == k, jnp.sum(onehots[k] * before, axis=-1, keepdims=True), out)
    pos_ref[...] = out.astype(I32)
    carry[...] = carry[...] + jnp.sum(hits, axis=0, keepdims=True)
    cnt_ref[...] = carry[...]


def _plan(topi):
    n = topi.shape[0]
    tm = _tile(n, 256)
    return pl.pallas_call(
        _plan_kernel,
        out_shape=(jax.ShapeDtypeStruct((n, LANES), I32), jax.ShapeDtypeStruct((1, LANES), F32)),
        grid=(n // tm,),
        in_specs=[pl.BlockSpec((tm, LANES), lambda i: (i, 0))],
        out_specs=(pl.BlockSpec((tm, LANES), lambda i: (i, 0)), pl.BlockSpec((1, LANES), lambda i: (0, 0))),
        scratch_shapes=[pltpu.VMEM((1, LANES), F32)],
        compiler_params=_params(("arbitrary",), 8 * tm * LANES * 4 + tm * tm * 8),
        name="moe_plan",
    )(topi)


def _dispatch_tables(topi, n_exp, sub, rmax):
    n = topi.shape[0]
    ids = topi[:, :TOP_K]
    n_pairs = n * TOP_K
    n_pad = -(-n // 256) * 256
    pos, counts = _plan(jnp.pad(topi, ((0, n_pad - n), (0, 0)), constant_values=-1))
    counts = counts[0, :n_exp].astype(I32)
    padded = (counts + sub - 1) // sub * sub
    pad_start = jnp.cumsum(padded) - padded
    dest = pad_start[ids] + pos[:n, :TOP_K]
    n_rows = -(-(n_pairs + n_exp * (sub - 1)) // sub) * sub
    row_tok = jnp.zeros((n_rows,), I32).at[dest.reshape(-1)].set(jnp.arange(n_pairs, dtype=I32) // TOP_K)
    n_sb = (counts + rmax - 1) // rmax
    cum = jnp.cumsum(n_sb)
    total = cum[-1]
    n_sb_max = n_exp + n_pairs // rmax
    s = jnp.arange(n_sb_max, dtype=I32)
    e_of = jnp.minimum(jnp.searchsorted(cum, s, side="right"), n_exp - 1).astype(I32)
    local = s - (cum - n_sb)[e_of]
    active = s < total
    rows = jnp.where(active, jnp.minimum(counts[e_of] - local * rmax, rmax), 0).astype(I32)
    start = jnp.where(active, pad_start[e_of] + local * rmax, 0).astype(I32)
    sb_exp = jnp.where(active, e_of, e_of[jnp.maximum(total - 1, 0)]).astype(I32)
    n_used = jnp.sum(padded).astype(I32).reshape(1)
    return dest.astype(I32), row_tok, sb_exp, start, rows, n_used, n_rows, n_sb_max


def _expert_kernel(exp_ref, start_ref, rows_ref, tok_ref, used_ref,
                   h_hbm, wg_ref, wu_ref, bg_ref, bu_ref, wd_ref, bd_ref, ys_hbm,
                   xbuf, act, wgu, wdb, ostage, gsem, osem, *, n1, sub, tf, dt):
    sb = pl.program_id(0)
    s = pl.program_id(1)
    rows = rows_ref[sb]
    start = start_ref[sb]
    n_sub = (rows + sub - 1) // sub
    dh = xbuf.shape[1]

    @pl.when(jnp.logical_and(sb == 0, s == 0))
    def _():
        ostage[0] = jnp.zeros(ostage.shape[1:], F32)
        n_tail = (ys_hbm.shape[0] - used_ref[0]) // sub

        def tail_copy(t, col):
            row0 = pl.multiple_of(used_ref[0] + t * sub, sub)
            return pltpu.make_async_copy(ostage.at[0], ys_hbm.at[pl.ds(row0, sub), pl.ds(col * dt, dt)],
                                         osem.at[0])

        def issue(t, c):
            for col in range(ys_hbm.shape[1] // dt):
                tail_copy(t, col).start()
            return c

        def drain(t, c):
            for col in range(ys_hbm.shape[1] // dt):
                tail_copy(t, col).wait()
            return c

        lax.fori_loop(0, n_tail, issue, 0)
        lax.fori_loop(0, n_tail, drain, 0)

    def gather_copy(r):
        tok = tok_ref[start + r]
        return pltpu.make_async_copy(h_hbm.at[pl.ds(tok, 1)], xbuf.at[pl.ds(r, 1)], gsem)

    @pl.when(jnp.logical_and(s == 0, rows > 0))
    def _():
        def issue(r, c):
            gather_copy(r).start()
            return c

        def drain(r, c):
            gather_copy(r).wait()
            return c

        lax.fori_loop(0, n_sub * sub, issue, 0)
        lax.fori_loop(0, n_sub * sub, drain, 0)

    def unpack(i):
        words = xbuf[pl.ds(pl.multiple_of(i * sub, sub), sub), :]
        lo = pltpu.bitcast(words << 16, F32).astype(BF16)
        hi = pltpu.bitcast(words & jnp.uint32(0xFFFF0000), F32).astype(BF16)
        return lo, hi

    def swiglu(g, u):
        g = jnp.minimum(g, SWIGLU_LIMIT)
        u = jnp.clip(u, -SWIGLU_LIMIT, SWIGLU_LIMIT)
        return (g * jax.nn.sigmoid(SWIGLU_ALPHA * g) * (u + 1.0)).astype(BF16)

    @pl.when(jnp.logical_and(s < n1, rows > 0))
    def _():
        lo, hi = unpack(0)
        g = bg_ref[...]
        u = bu_ref[...]
        kc = min(dh, 512)
        for k0 in range(0, 2 * dh, kc):
            xk = lo[:, k0:k0 + kc] if k0 < dh else hi[:, k0 - dh:k0 - dh + kc]
            wgc = wg_ref[k0:k0 + kc, :].astype(BF16)
            wuc = wu_ref[k0:k0 + kc, :].astype(BF16)
            wgu[0, k0:k0 + kc, :] = wgc
            wgu[1, k0:k0 + kc, :] = wuc
            g = g + _dot(xk, wgc)
            u = u + _dot(xk, wuc)
        act[s, 0:sub, :] = swiglu(g, u)

        def sub_block(i, c):
            lo, hi = unpack(i)
            g = _dot(lo, wgu[0, 0:dh, :]) + _dot(hi, wgu[0, dh:2 * dh, :]) + bg_ref[...]
            u = _dot(lo, wgu[1, 0:dh, :]) + _dot(hi, wgu[1, dh:2 * dh, :]) + bu_ref[...]
            act[s, pl.ds(pl.multiple_of(i * sub, sub), sub), :] = swiglu(g, u)
            return c

        lax.fori_loop(1, n_sub, sub_block, 0)

    @pl.when(jnp.logical_and(s >= n1, rows > 0))
    def _():
        col = pl.multiple_of((s - n1) * dt, dt)

        def out_copy(i, slot):
            row0 = pl.multiple_of(start + i * sub, sub)
            return pltpu.make_async_copy(ostage.at[slot], ys_hbm.at[pl.ds(row0, sub), pl.ds(col, dt)],
                                         osem.at[slot])

        y = bd_ref[...]
        for f in range(n1):
            wdc = wd_ref[f * tf:(f + 1) * tf, :].astype(BF16)
            wdb[f * tf:(f + 1) * tf, :] = wdc
            y = y + _dot(act[f, 0:sub, :], wdc)
        ostage[0] = y
        out_copy(0, 0).start()

        def sub_block(i, c):
            slot = i % 2

            @pl.when(i >= 2)
            def _():
                out_copy(i - 2, slot).wait()

            r0 = pl.multiple_of(i * sub, sub)
            y = bd_ref[...] + _dot(act[0, pl.ds(r0, sub), :], wdb[0:tf, :])
            for f in range(1, n1):
                y = y + _dot(act[f, pl.ds(r0, sub), :], wdb[f * tf:(f + 1) * tf, :])
            ostage[slot] = y
            out_copy(i, slot).start()
            return c

        lax.fori_loop(1, n_sub, sub_block, 0)

        @pl.when(n_sub >= 2)
        def _():
            out_copy(n_sub - 2, n_sub % 2).wait()

        out_copy(n_sub - 1, (n_sub - 1) % 2).wait()


def _experts(h_packed, tables, w_gu, b_gu, w_down, b_down, *, layer, sub, rmax):
    _, row_tok, sb_exp, sb_start, sb_rows, n_used, n_rows, n_sb_max = tables
    n_layers, n_exp, d, f2 = w_gu.shape
    f = f2 // 2
    tf = _tile(f, 256)
    dt = _tile(d, 512)
    n1, n2 = f // tf, d // dt

    def f_idx(sb, s, rows_ref):
        return jnp.where(rows_ref[sb] > 0, jnp.minimum(s, n1 - 1), n1 - 1)

    def d_idx(sb, s, rows_ref):
        return jnp.where(rows_ref[sb] > 0, jnp.maximum(s - n1, 0), n2 - 1)

    in_specs = [
        pl.BlockSpec(memory_space=pl.ANY),
        pl.BlockSpec((None, None, d, tf),
                     lambda sb, s, e, st, rw, tk, us: (layer, e[sb], 0, f_idx(sb, s, rw))),
        pl.BlockSpec((None, None, d, tf),
                     lambda sb, s, e, st, rw, tk, us: (layer, e[sb], 0, n1 + f_idx(sb, s, rw))),
        pl.BlockSpec((None, None, 1, tf),
                     lambda sb, s, e, st, rw, tk, us: (layer, e[sb], 0, f_idx(sb, s, rw))),
        pl.BlockSpec((None, None, 1, tf),
                     lambda sb, s, e, st, rw, tk, us: (layer, e[sb], 0, n1 + f_idx(sb, s, rw))),
        pl.BlockSpec((None, None, f, dt),
                     lambda sb, s, e, st, rw, tk, us: (layer, e[sb], 0, d_idx(sb, s, rw))),
        pl.BlockSpec((None, None, 1, dt),
                     lambda sb, s, e, st, rw, tk, us: (layer, e[sb], 0, d_idx(sb, s, rw))),
    ]
    blk = (rmax * (d // 2) * 4 + rmax * f * 2 + 4 * d * tf * 4 + 2 * f * dt * 4 + 2 * d * tf * 2
           + f * dt * 2 + 2 * sub * dt * 4 + 4 * sub * d * 2)
    return pl.pallas_call(
        functools.partial(_expert_kernel, n1=n1, sub=sub, tf=tf, dt=dt),
        out_shape=jax.ShapeDtypeStruct((n_rows, d), F32),
        grid_spec=pltpu.PrefetchScalarGridSpec(
            num_scalar_prefetch=5,
            grid=(n_sb_max, n1 + n2),
            in_specs=in_specs,
            out_specs=pl.BlockSpec(memory_space=pl.ANY),
            scratch_shapes=[
                pltpu.VMEM((rmax, d // 2), U32),
                pltpu.VMEM((n1, rmax, tf), BF16),
                pltpu.VMEM((2, d, tf), BF16),
                pltpu.VMEM((f, dt), BF16),
                pltpu.VMEM((2, sub, dt), F32),
                pltpu.SemaphoreType.DMA(()),
                pltpu.SemaphoreType.DMA((2,)),
            ]),
        compiler_params=_params(("arbitrary", "arbitrary"), blk),
        name="moe_experts",
    )(sb_exp, sb_start, sb_rows, row_tok, n_used, h_packed, w_gu, w_gu,
      b_gu.reshape(n_layers, n_exp, 1, f2), b_gu.reshape(n_layers, n_exp, 1, f2), w_down,
      b_down.reshape(n_layers, n_exp, 1, d))


def _combine_kernel(dest_ref, ys_hbm, x_ref, g_ref, gate_ref, o_ref, buf, sem, *, tiles_per_batch):
    tm = x_ref.shape[0]
    tok0 = (pl.program_id(0) * tiles_per_batch + pl.program_id(1)) * tm

    def row_copy(r, k):
        src = dest_ref[(tok0 + r) * TOP_K + k]
        return pltpu.make_async_copy(ys_hbm.at[pl.ds(src, 1)], buf.at[k, pl.ds(r, 1)], sem)

    def issue(r, c):
        for k in range(TOP_K):
            row_copy(r, k).start()
        return c

    def drain(r, c):
        for k in range(TOP_K):
            row_copy(r, k).wait()
        return c

    lax.fori_loop(0, tm, issue, 0)
    lax.fori_loop(0, tm, drain, 0)
    gate = gate_ref[...]
    y = gate[:, 0:1] * buf[0]
    for k in range(1, TOP_K):
        y = y + gate[:, k:k + 1] * buf[k]
    o_ref[...] = x_ref[...] + g_ref[...] * y


def _combine(ys, dest, x, g, gate):
    b, t, d = x.shape
    tm = _tile(t, 128)
    tiles_per_batch = t // tm
    row = lambda b_, i, dref: (b_, i, 0)
    if g.shape[1] == 1:
        g_spec = pl.BlockSpec((None, 1, d), lambda b_, i, dref: (b_, 0, 0))
    else:
        g_spec = pl.BlockSpec((None, tm, d), row)
    blk = TOP_K * tm * d * 4 + 6 * tm * d * 4
    return pl.pallas_call(
        functools.partial(_combine_kernel, tiles_per_batch=tiles_per_batch),
        out_shape=jax.ShapeDtypeStruct((b, t, d), F32),
        grid_spec=pltpu.PrefetchScalarGridSpec(
            num_scalar_prefetch=1,
            grid=(b, tiles_per_batch),
            in_specs=[pl.BlockSpec(memory_space=pl.ANY),
                      pl.BlockSpec((None, tm, d), row),
                      g_spec,
                      pl.BlockSpec((None, tm, LANES), row)],
            out_specs=pl.BlockSpec((None, tm, d), row),
            scratch_shapes=[pltpu.VMEM((TOP_K, tm, d), F32), pltpu.SemaphoreType.DMA(())]),
        compiler_params=_params(("arbitrary", "arbitrary"), blk),
        name="moe_combine",
    )(dest.reshape(-1), ys, x, g, gate)


def _moe(xp, xs, g, mods_p, mods_s, w_router, b_router, w_gu, b_gu, w_down, b_down, *, layer):
    (sc_p, sh_p, g_p), (sc_s, sh_s, g_s) = mods_p, mods_s
    d = xp.shape[-1]
    n_exp = w_router.shape[1]
    f = w_down.shape[2]
    hp, ti_p, gt_p = _norm_router(xp, g, sc_p, sh_p, w_router, b_router)
    hs, ti_s, gt_s = _norm_router(xs, g, sc_s, sh_s, w_router, b_router)
    n_p = xp.shape[0] * xp.shape[1]
    h_all = jnp.concatenate([hp.reshape(n_p, d // 2), hs.reshape(-1, d // 2)], axis=0)
    topi = jnp.concatenate([ti_p.reshape(n_p, LANES), ti_s.reshape(-1, LANES)], axis=0)
    sub = MOE_SUB_ROWS
    rmax = MOE_SUBS_PER_SUPERBLOCK * sub
    tables = _dispatch_tables(topi, n_exp, sub, rmax)
    ys = _experts(h_all, tables, w_gu, b_gu, w_down, b_down, layer=layer, sub=sub, rmax=rmax)
    dest = tables[0]
    xp = _combine(ys, dest[:n_p], xp, g_p, gt_p)
    xs = _combine(ys, dest[n_p:], xs, g_s, gt_s)
    return xp, xs


def _nsa_project(h, w_in, q_g, k_g):
    d = h.shape[1]
    hd = LANES
    kvh = N_KV_HEADS
    grp = N_HEADS // kvh
    q_dim = N_HEADS * hd
    kv_dim = 2 * kvh * hd
    q = _mm(h, w_in, col0=0, n=q_dim, epi="qnorm", extra=(q_g.reshape(1, hd),), out_dtype=BF16,
            head_dim=hd, scale=hd ** -0.5 * LOG2_E, name="nsa_q")
    ones = jnp.ones((kvh * hd,), F32)
    zeros = jnp.zeros((kvh * hd,), F32)
    flags = jnp.concatenate([zeros, zeros, ones, zeros, ones, zeros]).reshape(1, 3 * kv_dim)
    gains = jnp.concatenate([ones, ones, jnp.tile(k_g[1], kvh), ones, jnp.tile(k_g[2], kvh), ones])
    kv = _mm(h, w_in, col0=q_dim, n=3 * kv_dim, epi="kvnorm", extra=(flags, gains.reshape(1, 3 * kv_dim)),
             head_dim=hd, name="nsa_kv")
    w_gate = w_in[:, q_dim + 3 * kv_dim:].reshape(d, 3, kvh, grp)
    w_gate = jnp.transpose(w_gate, (0, 2, 1, 3)).reshape(d, kvh, 3 * grp)
    w_gate = jnp.pad(w_gate, ((0, 0), (0, 0), (0, LANES - 3 * grp))).reshape(d, kvh * LANES)
    gates = _mm(h, w_gate, col0=0, n=kvh * LANES, epi="sigmoid", name="nsa_gates")
    return q, kv, gates


def _cmp_part_kernel(pt_ref, *refs, n_pages, kvh, row_view):
    n_refs = n_pages if row_view else n_pages * 2 * kvh
    pages = refs[:n_refs]
    w_ref, pe_ref, p_out, pe_out, wbf = refs[n_refs:]
    rpt = 2 * kvh
    chunks = PAGE_SIZE // CMP_STRIDE

    def chunk_rows(j, h, s):
        if row_view:
            return [pg[pl.ds(s * rpt + j * kvh + h, chunks, stride=CMP_STRIDE * rpt), :] for pg in pages]
        return [pg[pl.ds(s, chunks, stride=CMP_STRIDE), :] for pg in pages[j * kvh + h::rpt]]

    @pl.when(pl.program_id(0) == 0)
    def _():
        for j in range(2):
            wbf[j] = w_ref[j].astype(BF16)
            pe_out[j] = _dot(pe_ref[j].astype(BF16), wbf[j])

    rows = n_pages * PAGE_SIZE // CMP_STRIDE
    for j in range(2):
        per_head = []
        for h in range(kvh):
            pieces = [jnp.concatenate(chunk_rows(j, h, s), axis=0) for s in range(CMP_STRIDE)]
            per_head.append(jnp.concatenate(pieces, axis=1))
        x = jnp.concatenate(per_head, axis=0).astype(BF16)
        y = _dot(x, wbf[j])
        for h in range(kvh):
            p_out[j, h] = y[h * rows:(h + 1) * rows, :]


def _cmp_out_kernel(pk_ref, pv_ref, b1_ref, pe_ref, w2_ref, b2_ref, kg_ref, kc_ref, vc_ref, shifted):
    n_ch = pk_ref.shape[0]
    hid_dim = pk_ref.shape[1] // 2
    for j, (p_ref, o_ref) in enumerate(((pk_ref, kc_ref), (pv_ref, vc_ref))):
        shifted[0:n_ch, :] = p_ref[:, hid_dim:2 * hid_dim]
        shifted[n_ch:n_ch + SUBLANES, :] = jnp.zeros((SUBLANES, hid_dim), F32)
        hid = (b1_ref[j:j + 1, :] + pe_ref[j, 0:1, 0:hid_dim] + pe_ref[j, 1:2, hid_dim:2 * hid_dim]
               + p_ref[:, 0:hid_dim] + shifted[1:1 + n_ch, :])
        a = hid * jax.nn.sigmoid(hid)
        out = _dot(a.astype(BF16), w2_ref[j].astype(BF16)) + b2_ref[j:j + 1, :]
        if j == 0:
            out = _rms(out) * kg_ref[...]
        o_ref[...] = out.astype(o_ref.dtype)


def _compress(src_pages, page_table, batch, w1, b1, w2, b2, pe, kg, *, layer=None):
    kvh = N_KV_HEADS
    hd = LANES
    n_log = page_table.shape[0]
    row_view = layer is not None
    n_pages = _tile(n_log, 16 if row_view else 8)
    chunks_per_page = PAGE_SIZE // CMP_STRIDE
    n_chunks = n_log * chunks_per_page
    hid2 = w1.shape[1] * w1.shape[-1]
    kdim = CMP_STRIDE * hd
    w1cat = jnp.transpose(w1, (0, 2, 3, 1, 4)).reshape(2, kdim, hid2)
    pe_rows = jnp.pad(pe.reshape(2, -1, kdim), ((0, 0), (0, SUBLANES - pe.shape[1]), (0, 0)))

    if row_view:
        rpt = 2 * kvh
        src_pages = src_pages.reshape(src_pages.shape[0], src_pages.shape[1], PAGE_SIZE * rpt, hd)
        page_specs = [pl.BlockSpec((None, None, PAGE_SIZE * rpt, hd),
                                   lambda i, pt, r=r: (layer, pt[i * n_pages + r], 0, 0)) for r in range(n_pages)]
    else:
        page_specs = [pl.BlockSpec((None, PAGE_SIZE, hd),
                                   lambda i, pt, r=r, cb=cb: (pt[i * n_pages + r], 0, cb))
                      for r in range(n_pages) for cb in range(2 * kvh)]
    rows = n_pages * chunks_per_page
    blk = (2 * n_pages * PAGE_SIZE * 2 * kvh * hd * 4 + 3 * 2 * kdim * hid2 * 4
           + 4 * kvh * rows * (kdim + hid2) * 4)
    parts, pe_out = pl.pallas_call(
        functools.partial(_cmp_part_kernel, n_pages=n_pages, kvh=kvh, row_view=row_view),
        out_shape=(jax.ShapeDtypeStruct((2, kvh, n_chunks, hid2), F32),
                   jax.ShapeDtypeStruct((2, SUBLANES, hid2), F32)),
        grid_spec=pltpu.PrefetchScalarGridSpec(
            num_scalar_prefetch=1,
            grid=(n_log // n_pages,),
            in_specs=page_specs + [
                pl.BlockSpec((2, kdim, hid2), lambda i, pt: (0, 0, 0)),
                pl.BlockSpec((2, SUBLANES, kdim), lambda i, pt: (0, 0, 0))],
            out_specs=(pl.BlockSpec((2, kvh, rows, hid2), lambda i, pt: (0, 0, i, 0)),
                       pl.BlockSpec((2, SUBLANES, hid2), lambda i, pt: (0, 0, 0))),
            scratch_shapes=[pltpu.VMEM((2, kdim, hid2), BF16)]),
        compiler_params=_params(("arbitrary",), blk),
        name="cmp_part",
    )(page_table, *([src_pages] * len(page_specs)), w1cat, pe_rows)

    n_ch = n_chunks // batch
    hid = hid2 // 2
    part_spec = lambda j: pl.BlockSpec((None, None, n_ch, hid2), lambda b, h: (j, h, b, 0))
    full = lambda *shape: pl.BlockSpec(shape, lambda b, h: (0,) * len(shape))
    out_spec = pl.BlockSpec((None, None, n_ch, hd), lambda b, h: (b, h, 0, 0))
    return pl.pallas_call(
        _cmp_out_kernel,
        out_shape=(jax.ShapeDtypeStruct((batch, kvh, n_ch, hd), BF16),) * 2,
        grid=(batch, kvh),
        in_specs=[part_spec(0), part_spec(1), full(2, hid), full(2, SUBLANES, hid2), full(2, hid, hd),
                  full(2, hd), full(1, hd)],
        out_specs=(out_spec, out_spec),
        scratch_shapes=[pltpu.VMEM((n_ch + SUBLANES, hid), F32)],
        compiler_params=_params(("arbitrary", "arbitrary"), 12 * n_ch * hid2 * 4),
        name="cmp_out",
    )(parts, parts, b1, pe_out, w2, b2, kg.reshape(1, hd))


def _overlap(n_ch, n_blk):
    c0 = _iota((n_ch, n_blk), 0) * CMP_STRIDE
    b0 = _iota((n_ch, n_blk), 1) * SEL_BLOCK
    return jnp.where(c0 < b0 + SEL_BLOCK, jnp.where(c0 + CMP_BLOCK > b0, 1.0, 0.0), 0.0).astype(BF16)


def _cmp_scores(q, kc, slope, tpos, n_cmp):
    n_ch = kc.shape[0]
    c_idx = _iota((1, n_ch), 1)
    d_c = tpos - (c_idx * CMP_STRIDE + (CMP_BLOCK - 1))
    valid = jnp.where(c_idx < n_cmp, d_c, -1) >= 0
    s = _dot_nt(q, kc) - slope * d_c.astype(F32)
    s = jnp.where(valid, s, -jnp.inf)
    m = jnp.max(s, axis=-1, keepdims=True)
    m = jnp.where(m == -jnp.inf, 0.0, m)
    e = jnp.exp2(s - m)
    return e / jnp.maximum(jnp.sum(e, axis=-1, keepdims=True), 1e-30)


def _with_position_lanes(k, pos0):
    rows, hd = k.shape
    pos = pos0 + _iota((rows, hd), 0)
    lane = _iota((rows, hd), 1)
    ext = jnp.where(lane < 3, pos // LANES * LANES, jnp.where(lane < 6, pos % LANES, 0))
    return jnp.concatenate([k, ext.astype(F32).astype(BF16)], axis=1)


def _with_slope_lanes(q, parts):
    lane = _iota(q.shape, 1)
    ext = jnp.zeros(q.shape, F32)
    for c, part in enumerate(parts):
        ext = jnp.where(lane == c, part, jnp.where(lane == c + 3, part, ext))
    return jnp.concatenate([q, ext.astype(BF16)], axis=1)


def _with_ones(v):
    return jnp.concatenate([v, jnp.ones(v.shape, BF16)], axis=1)


def _force_and_mask(imp, tpos):
    blk = _iota((1, imp.shape[1]), 1)
    cur = tpos // SEL_BLOCK
    forced = jnp.logical_or(blk == 0, jnp.logical_or(blk == cur, blk == cur - 1))
    imp = jnp.where(forced, FORCE_SCORE, imp)
    return jnp.where(blk > cur, -jnp.inf, imp)


def _attn_prompt_kernel(slope_ref, q_ref, kc_ref, vc_ref, ks_ref, vs_ref, kw_ref, vw_ref, gt_ref, o_ref,
                        m_s, acc_s, oc_s, qa_s, *, grp, seq, tq, tk, n_cmp, n_blk, n_top, wl):
    hk = pl.program_id(1)
    t0 = pl.program_id(2) * tq
    hd = LANES
    n_heads = slope_ref.shape[0] // 4
    row_t = t0 + _iota((tq, 1), 0)
    heads = [(g, slope_ref[hk * grp + g], slice(g * hd, (g + 1) * hd)) for g in range(grp)]

    kc = kc_ref[...]
    vc = vc_ref[...]
    n_ch = kc.shape[0]
    psum = jnp.zeros((tq, n_ch), F32)
    for g, slope, cols in heads:
        q = q_ref[:, cols]
        p = _cmp_scores(q, kc, slope, row_t, n_cmp)
        psum = psum + p
        oc_s[g] = _dot(p.astype(BF16), vc)
        qa_s[g] = _with_slope_lanes(q, [slope_ref[(1 + c) * n_heads + hk * grp + g] for c in range(3)])
    ov = _overlap(n_ch, n_blk)
    p1, p2, p3 = _split3(psum)
    imp = _force_and_mask(_dot(p1, ov) + _dot(p2, ov) + _dot(p3, ov), row_t)
    blk = _iota((1, n_blk), 1)
    rank = jnp.zeros((tq, n_blk), F32)
    for j in range(n_blk):
        cj = imp[:, j:j + 1]
        tie = jnp.where(blk > j, 1.0, 0.0)
        rank = rank + jnp.where(cj > imp, 1.0, jnp.where(cj == imp, tie, 0.0))
    sel = jnp.where(rank < n_top, 1.0, 0.0).astype(BF16)

    m_s[...] = jnp.full(m_s.shape, MASKED, F32)
    acc_s[...] = jnp.zeros(acc_s.shape, F32)

    def kv_tile(kt, carry):
        k0 = pl.multiple_of(kt * tk, tk)
        ka = _with_position_lanes(ks_ref[pl.ds(k0, tk), :].astype(BF16), k0)
        va = _with_ones(vs_ref[pl.ds(k0, tk), :].astype(BF16))
        expand = jnp.where(_iota((n_blk, tk), 0) == (k0 + _iota((n_blk, tk), 1)) // SEL_BLOCK, 1.0, 0.0)
        picked = _dot(sel, expand.astype(BF16))
        ok = jnp.where(row_t - (k0 + _iota((1, tk), 1)) >= 0, picked, 0.0) > 0.5
        for g, _, _ in heads:
            s = jnp.where(ok, _dot_nt(qa_s[g], ka), MASKED)
            m_old = m_s[g]
            m_new = jnp.maximum(m_old, jnp.max(s, axis=-1, keepdims=True))
            p = jnp.exp2(s - m_new)
            acc_s[g] = jnp.exp2(m_old - m_new) * acc_s[g] + _dot(p.astype(BF16), va)
            m_s[g] = m_new
        return carry

    lax.fori_loop(0, (t0 + tq + tk - 1) // tk, kv_tile, 0)

    ws = pl.multiple_of(jnp.clip(t0 + tq - wl, 0, seq - wl), SUBLANES)
    kwa = _with_position_lanes(kw_ref[pl.ds(ws, wl), :].astype(BF16), ws)
    vwa = _with_ones(vw_ref[pl.ds(ws, wl), :].astype(BF16))
    dw = row_t - (ws + _iota((1, wl), 1))
    okw = jnp.where(dw >= 0, jnp.where(dw < WINDOW, 1.0, 0.0), 0.0) > 0.5
    gates = gt_ref[...]
    for g, _, cols in heads:
        s = jnp.where(okw, _dot_nt(qa_s[g], kwa), MASKED)
        e = jnp.exp2(s - jnp.max(s, axis=-1, keepdims=True))
        win = _dot(e.astype(BF16), vwa)
        o_w = win[:, :hd] / win[:, hd:hd + 1]
        acc = acc_s[g]
        o_s = acc[:, :hd] / acc[:, hd:hd + 1]
        out = (gates[:, g:g + 1] * oc_s[g] + gates[:, grp + g:grp + g + 1] * o_s
               + gates[:, 2 * grp + g:2 * grp + g + 1] * o_w)
        o_ref[:, cols] = out.astype(o_ref.dtype)


def _attend_prompt(q, kc, vc, kv, gates, slopes, *, batch, seq):
    kvh = N_KV_HEADS
    grp = N_HEADS // kvh
    hd = LANES
    tq = _tile(seq, 256)
    tk = _tile(seq, 512)
    nq = seq // tq
    n_ch = kc.shape[2]
    n_blk = -(-seq // SEL_BLOCK)
    wl = min(tq + WINDOW, seq)
    kv3 = kv.reshape(batch, seq, kv.shape[1])
    row = lambda b, h, i: (b * nq + i, h)
    kv_spec = lambda cb: pl.BlockSpec((None, seq, hd), lambda b, h, i: (b, 0, cb * kvh + h))
    cmp_spec = pl.BlockSpec((None, None, n_ch, hd), lambda b, h, i: (b, h, 0, 0))
    blk = (8 * seq * hd * 4 + 4 * tq * grp * hd * 2 + 3 * grp * tq * hd * 4 + 10 * tq * max(tk, wl) * 4
           + 4 * n_ch * hd * 2)
    return pl.pallas_call(
        functools.partial(_attn_prompt_kernel, grp=grp, seq=seq, tq=tq, tk=tk, n_cmp=n_ch - 1, n_blk=n_blk,
                          n_top=min(N_SEL, n_blk), wl=wl),
        out_shape=jax.ShapeDtypeStruct((batch * seq, N_HEADS * hd), BF16),
        grid=(batch, kvh, nq),
        in_specs=[pl.BlockSpec(memory_space=pltpu.SMEM),
                  pl.BlockSpec((tq, grp * hd), row), cmp_spec, cmp_spec,
                  kv_spec(2), kv_spec(3), kv_spec(4), kv_spec(5),
                  pl.BlockSpec((tq, LANES), row)],
        out_specs=pl.BlockSpec((tq, grp * hd), row),
        scratch_shapes=[pltpu.VMEM((grp, tq, 1), F32), pltpu.VMEM((grp, tq, 2 * hd), F32),
                        pltpu.VMEM((grp, tq, hd), F32), pltpu.VMEM((grp, tq, 2 * hd), BF16)],
        compiler_params=_params(("arbitrary", "arbitrary", "arbitrary"), blk),
        name="nsa_attend_prompt",
    )(slopes, q, kc, vc, kv3, kv3, kv3, kv3, gates)


def _attn_sample_kernel(pt_ref, slope_ref, q_ref, kc_ref, vc_ref, *rest, n_pages, kvh, grp, dec_seq, past, n_cmp,
                        n_blk, n_top, nb_pad):
    pages = rest[:n_pages]
    new_ref, win_ref, gt_ref, o_ref, m_s, l_s, acc_s, oc_s, sel_s = rest[n_pages:]
    c = pl.program_id(1)
    rows = q_ref.shape[1]
    hd = LANES
    rpt = 2 * kvh
    row = _iota((rows, 1), 0)
    t_row = row // grp
    g_row = row % grp
    tpos = past + t_row
    tk = n_pages * PAGE_SIZE
    k0 = c * tk
    expand = jnp.where(_iota((nb_pad, tk), 0) == (k0 + _iota((nb_pad, tk), 1)) // SEL_BLOCK, 1.0, 0.0).astype(BF16)
    d_past = tpos - (k0 + _iota((1, tk), 1))
    d_past_f = d_past.astype(F32)

    def head_rows(ref, j, hk, n):
        return ref[pl.ds(j * kvh + hk, n, stride=rpt), :].astype(BF16)

    for hk in range(kvh):
        q = q_ref[hk]
        slope = jnp.zeros((rows, 1), F32)
        for g in range(grp):
            slope = jnp.where(g_row == g, slope_ref[hk * grp + g], slope)

        def online_update(s, v, hk=hk):
            m_old = m_s[hk]
            m_new = jnp.maximum(m_old, jnp.max(s, axis=-1, keepdims=True))
            alpha = jnp.exp2(m_old - m_new)
            p = jnp.exp2(s - m_new)
            l_s[hk] = alpha * l_s[hk] + jnp.sum(p, axis=-1, keepdims=True)
            acc_s[hk] = alpha * acc_s[hk] + _dot(p.astype(BF16), v)
            m_s[hk] = m_new

        @pl.when(c == 0)
        def _(hk=hk, q=q, slope=slope):
            kc = kc_ref[hk]
            n_ch = kc.shape[0]
            p = _cmp_scores(q, kc, slope, tpos, n_cmp)
            oc_s[hk] = _dot(p.astype(BF16), vc_ref[hk])
            ov = _overlap(n_ch, nb_pad)
            p1, p2, p3 = _split3(p)
            per_head = _dot(p1, ov) + _dot(p2, ov) + _dot(p3, ov)
            same_tok = jnp.where(_iota((rows, rows), 0) // grp == _iota((rows, rows), 1) // grp, 1.0, 0.0)
            same_tok = same_tok.astype(BF16)
            a1, a2, a3 = _split3(per_head)
            imp = _force_and_mask(_dot(same_tok, a1) + _dot(same_tok, a2) + _dot(same_tok, a3), tpos)
            blk_f = _iota((1, nb_pad), 1).astype(F32)
            taken = jnp.where(blk_f >= n_blk, 1.0, 0.0) + jnp.zeros((rows, nb_pad), F32)
            sel = jnp.zeros((rows, nb_pad), F32)
            for _ in range(n_top):
                avail = jnp.where(taken > 0.5, -jnp.inf, imp)
                best = jnp.max(avail, axis=-1, keepdims=True)
                cand = jnp.where(taken > 0.5, 0.0, jnp.where(avail == best, 1.0, 0.0))
                idx = jnp.min(jnp.where(cand > 0.5, blk_f, float(nb_pad)), axis=-1, keepdims=True)
                pick = blk_f == idx
                taken = jnp.where(pick, 1.0, taken)
                sel = jnp.where(pick, 1.0, sel)
            sel_s[hk] = sel.astype(BF16)
            m_s[hk] = jnp.full((rows, 1), MASKED, F32)
            l_s[hk] = jnp.zeros((rows, 1), F32)
            acc_s[hk] = jnp.zeros((rows, hd), F32)

        kk = jnp.concatenate([head_rows(pg, 0, hk, PAGE_SIZE) for pg in pages], axis=0)
        vv = jnp.concatenate([head_rows(pg, 1, hk, PAGE_SIZE) for pg in pages], axis=0)
        picked = _dot(sel_s[hk], expand)
        ok = jnp.where(d_past >= 0, picked, 0.0) > 0.5
        online_update(jnp.where(ok, _dot_nt(q, kk) - slope * d_past_f, MASKED), vv)

        @pl.when(c == pl.num_programs(1) - 1)
        def _(hk=hk, q=q, slope=slope, online_update=online_update):
            new_cols = lambda cb: slice((cb * kvh + hk) * hd, (cb * kvh + hk + 1) * hd)
            t_new = _iota((1, new_ref.shape[0]), 1)
            dn = t_row - t_new
            causal_new = jnp.where(dn >= 0, jnp.where(t_new < dec_seq, 1.0, 0.0), 0.0)
            dnf = dn.astype(F32)
            online_update(jnp.where(causal_new > 0.5, _dot_nt(q, new_ref[:, new_cols(2)].astype(BF16)) - slope * dnf,
                                    MASKED), new_ref[:, new_cols(3)].astype(BF16))
            o_s = acc_s[hk] / l_s[hk]
            n_buf = win_ref.shape[0] // rpt
            dc = tpos - (past - n_buf + _iota((1, n_buf), 1))
            ok_c = jnp.where(dc >= 0, jnp.where(dc < WINDOW, 1.0, 0.0), 0.0) > 0.5
            s_c = jnp.where(ok_c, _dot_nt(q, head_rows(win_ref, 0, hk, n_buf)) - slope * dc.astype(F32), MASKED)
            ok_n = jnp.where(dn < WINDOW, causal_new, 0.0) > 0.5
            s_n = jnp.where(ok_n, _dot_nt(q, new_ref[:, new_cols(4)].astype(BF16)) - slope * dnf, MASKED)
            m = jnp.maximum(jnp.max(s_c, axis=-1, keepdims=True), jnp.max(s_n, axis=-1, keepdims=True))
            e_c = jnp.exp2(s_c - m)
            e_n = jnp.exp2(s_n - m)
            den = jnp.sum(e_c, axis=-1, keepdims=True) + jnp.sum(e_n, axis=-1, keepdims=True)
            o_w = (_dot(e_c.astype(BF16), head_rows(win_ref, 1, hk, n_buf))
                   + _dot(e_n.astype(BF16), new_ref[:, new_cols(5)].astype(BF16))) / den
            gates = gt_ref[hk]
            o_ref[hk] = (gates[:, 0:1] * oc_s[hk] + gates[:, 1:2] * o_s + gates[:, 2:3] * o_w).astype(o_ref.dtype)


def _attend_sample(q_rows, kc, vc, cache_sel, page_table, kv_new, cache_win, gate_rows, slopes, *, layer, past,
                   dec_seq):
    batch, kvh, rows, hd = q_rows.shape
    grp = N_HEADS // kvh
    assert past % SEL_BLOCK == 0 and dec_seq <= SEL_BLOCK and past % PAGE_SIZE == 0
    pages_per_batch = past // PAGE_SIZE
    n_pages = _tile(pages_per_batch, 8)
    n_ch = kc.shape[2]
    n_blk = -(-(past + dec_seq) // SEL_BLOCK)
    nb_pad = -(-n_blk // LANES) * LANES
    n_layers, n_phys = cache_sel.shape[:2]
    n_buf = cache_win.shape[2]
    rpt = 2 * kvh
    sel_rows = cache_sel.reshape(n_layers, n_phys, PAGE_SIZE * rpt, hd)
    win_rows = cache_win.reshape(n_layers, batch, n_buf * rpt, hd)

    def page_spec(r):
        return pl.BlockSpec((None, None, PAGE_SIZE * rpt, hd),
                            lambda b, c, pt: (layer, pt[b * pages_per_batch + c * n_pages + r], 0, 0))

    per_batch = lambda *shape: pl.BlockSpec((None,) + shape, lambda b, c, pt: (b,) + (0,) * len(shape))
    row_spec = per_batch(kvh, rows, hd)
    tk = n_pages * PAGE_SIZE
    blk = (2 * n_pages * PAGE_SIZE * rpt * hd * 4 + 4 * kvh * n_ch * hd * 2 + 2 * n_buf * rpt * hd * 4
           + 16 * rows * max(tk, n_ch, nb_pad) * 4 + 3 * nb_pad * max(tk, n_ch) * 4)
    return pl.pallas_call(
        functools.partial(_attn_sample_kernel, n_pages=n_pages, kvh=kvh, grp=grp, dec_seq=dec_seq, past=past,
                          n_cmp=n_ch - 1, n_blk=n_blk, n_top=min(N_SEL, n_blk), nb_pad=nb_pad),
        out_shape=jax.ShapeDtypeStruct((batch, kvh, rows, hd), BF16),
        grid_spec=pltpu.PrefetchScalarGridSpec(
            num_scalar_prefetch=1,
            grid=(batch, pages_per_batch // n_pages),
            in_specs=[pl.BlockSpec(memory_space=pltpu.SMEM), row_spec, per_batch(kvh, n_ch, hd),
                      per_batch(kvh, n_ch, hd)]
            + [page_spec(r) for r in range(n_pages)]
            + [per_batch(kv_new.shape[1], kv_new.shape[2]),
               pl.BlockSpec((None, None, n_buf * rpt, hd), lambda b, c, pt: (layer, b, 0, 0)), row_spec],
            out_specs=row_spec,
            scratch_shapes=[pltpu.VMEM((kvh, rows, 1), F32), pltpu.VMEM((kvh, rows, 1), F32),
                            pltpu.VMEM((kvh, rows, hd), F32), pltpu.VMEM((kvh, rows, hd), F32),
                            pltpu.VMEM((kvh, rows, nb_pad), BF16)]),
        compiler_params=_params(("arbitrary", "arbitrary"), blk),
        name="nsa_attend_sample",
    )(page_table, slopes, q_rows, kc, vc, *([sel_rows] * n_pages), kv_new, win_rows, gate_rows)


def _nsa_layer(hp, hs, xp, xs, g1p, g1s, caches, page_table, weights, slopes, *, layer, batch, seq, dec_batch,
               dec_seq):
    (w_in, w_out, q_g, k_g, pe, w1, b1, w2, b2) = weights
    cache_cmp, cache_sel, cache_win = caches
    d = xp.shape[-1]
    kvh = N_KV_HEADS
    grp = N_HEADS // kvh
    hd = LANES
    cols = 2 * kvh * hd
    n_s = dec_batch * dec_seq
    past = page_table.shape[1] * PAGE_SIZE
    assert seq % PAGE_SIZE == 0 and past % CMP_STRIDE == 0 and dec_seq < CMP_STRIDE
    cmp_w = (w1, b1, w2, b2, pe, k_g[0])

    q, kv, gates = _nsa_project(hp, w_in, q_g, k_g)
    n_pages_p = batch * seq // PAGE_SIZE
    kc, vc = _compress(kv.reshape(n_pages_p, PAGE_SIZE, 3 * cols), jnp.arange(n_pages_p, dtype=I32), batch, *cmp_w)
    o = _attend_prompt(q, kc, vc, kv, gates, slopes, batch=batch, seq=seq)
    xp = _mm(o, w_out, col0=0, n=d, epi="res", extra=(xp.reshape(batch * seq, d), g1p), rows_per_batch=seq,
             name="nsa_out").reshape(batch, seq, d)
    kv_p = kv.reshape(batch, seq, 3, 2, kvh, hd)
    new_p = (kv_p[:, :, 0], kv_p[:, :, 1], kv_p[:, seq - min(WINDOW, seq):, 2])

    qs, kvs, gs = _nsa_project(hs, w_in, q_g, k_g)
    pt = page_table.reshape(-1)
    kcs, vcs = _compress(cache_cmp, pt, dec_batch, *cmp_w, layer=layer)
    q_rows = jnp.transpose(qs.reshape(dec_batch, dec_seq, kvh, grp, hd), (0, 2, 1, 3, 4))
    q_rows = q_rows.reshape(dec_batch, kvh, dec_seq * grp, hd)
    gate_rows = gs.reshape(dec_batch, dec_seq, kvh, LANES)[..., :3 * grp].reshape(dec_batch, dec_seq, kvh, 3, grp)
    gate_rows = jnp.transpose(gate_rows, (0, 2, 1, 4, 3)).reshape(dec_batch, kvh, dec_seq * grp, 3)
    gate_rows = jnp.pad(gate_rows, ((0, 0), (0, 0), (0, 0), (0, LANES - 3)))
    kv_new = jnp.pad(kvs.reshape(dec_batch, dec_seq, 3 * cols), ((0, 0), (0, SUBLANES - dec_seq), (0, 0)))
    n_buf = cache_win.shape[2]
    o_rows = _attend_sample(q_rows, kcs, vcs, cache_sel, pt, kv_new, cache_win, gate_rows, slopes, layer=layer,
                            past=past, dec_seq=dec_seq)
    o_s = jnp.transpose(o_rows.reshape(dec_batch, kvh, dec_seq, grp, hd), (0, 2, 1, 3, 4)).reshape(n_s, N_HEADS * hd)
    xs = _mm(o_s, w_out, col0=0, n=d, epi="res", extra=(xs.reshape(n_s, d), g1s[0]),
             name="nsa_out").reshape(1, n_s, d)
    kv_s = kvs.reshape(dec_batch, dec_seq, 3, 2, kvh, hd)
    win_s = jnp.concatenate([cache_win[layer], kv_s[:, :, 2]], axis=1)[:, -n_buf:]
    new_s = (kv_s[:, :, 0], kv_s[:, :, 1], win_s)
    return xp, xs, new_p, new_s


def kernel(x_prompt, x_sample, c_prompt, c_sample, state_conv, cache_cmp_kv, cache_sel_kv, cache_win_kv,
           page_table, w_mod, b_mod, norm_g, conv_w_in, conv_w, conv_w_out, nsa_w_in, nsa_w_out, q_norm_g,
           k_norm_g, cmp_pe, cmp_w1, cmp_b1, cmp_w2, cmp_b2, router_w, router_b, moe_w_gu, moe_b_gu,
           moe_w_down, moe_b_down):
    batch, seq, d = x_prompt.shape
    dec_batch, dec_seq, _ = x_sample.shape
    depth = w_mod.shape[0]
    n_s = dec_batch * dec_seq
    assert dec_seq >= conv_w.shape[1] - 1

    c_all = jnp.concatenate([c_prompt, c_sample], axis=0)
    c_all = jnp.pad(c_all, ((0, -c_all.shape[0] % SUBLANES), (0, 0)))
    mod = _adaln(c_all, w_mod, b_mod)
    slope2 = jnp.exp2(-8.0 * jnp.arange(1, N_HEADS + 1, dtype=F32) / N_HEADS) * LOG2_E
    slopes = jnp.concatenate([slope2] + [p.astype(F32) for p in _split3(slope2)])

    xp = x_prompt
    xs = x_sample.reshape(1, n_s, d)
    conv_p, conv_s, cmp_p, cmp_s, sel_p, sel_s, win_p, win_s = [], [], [], [], [], [], [], []
    for i in range(depth):
        j = i // 2
        sh1p, sc1p, g1p, sh2p, sc2p, g2p = [m[:, None, :] for m in jnp.split(mod[i, :batch], 6, axis=-1)]
        sh1s, sc1s, g1s, sh2s, sc2s, g2s = [jnp.repeat(m, dec_seq, axis=0)[None]
                                            for m in jnp.split(mod[i, batch:batch + dec_batch], 6, axis=-1)]
        hp = _norm_mod(xp, norm_g[i, 0], sc1p, sh1p).reshape(batch * seq, d)
        hs = _norm_mod(xs, norm_g[i, 0], sc1s, sh1s).reshape(n_s, d)
        if i % 2 == 0:
            a_p, state_p = _conv_in(hp, conv_w_in[j], conv_w[j], batch=batch, seq=seq)
            a_s, v_s = _conv_in(hs, conv_w_in[j], conv_w[j], batch=dec_batch, seq=dec_seq,
                                prev=_conv_prev_rows(state_conv[j], dec_seq))
            xp = _mm(a_p, conv_w_out[j], col0=0, n=d, epi="res", extra=(xp.reshape(batch * seq, d), g1p),
                     rows_per_batch=seq, name="conv_out").reshape(batch, seq, d)
            xs = _mm(a_s, conv_w_out[j], col0=0, n=d, epi="res", extra=(xs.reshape(n_s, d), g1s[0]),
                     name="conv_out").reshape(1, n_s, d)
            conv_p.append(state_p)
            conv_s.append(v_s.reshape(dec_batch, dec_seq, d)[:, dec_seq - 2:])
        else:
            weights = (nsa_w_in[j], nsa_w_out[j], q_norm_g[j], k_norm_g[j], cmp_pe[j], cmp_w1[j], cmp_b1[j],
                       cmp_w2[j], cmp_b2[j])
            xp, xs, new_p, new_s = _nsa_layer(
                hp, hs, xp, xs, g1p, g1s, (cache_cmp_kv, cache_sel_kv, cache_win_kv), page_table,
                weights, slopes, layer=j, batch=batch, seq=seq, dec_batch=dec_batch, dec_seq=dec_seq)
            cmp_p.append(new_p[0])
            sel_p.append(new_p[1])
            win_p.append(new_p[2])
            cmp_s.append(new_s[0])
            sel_s.append(new_s[1])
            win_s.append(new_s[2])
        xp, xs = _moe(xp, xs, norm_g[i, 1], (sc2p, sh2p, g2p), (sc2s, sh2s, g2s), router_w[i], router_b[i],
                      moe_w_gu, moe_b_gu, moe_w_down, moe_b_down, layer=i)
    return (xp, xs.reshape(dec_batch, dec_seq, d), jnp.stack(conv_p), jnp.stack(conv_s), jnp.stack(cmp_p),
            jnp.stack(cmp_s), jnp.stack(sel_p), jnp.stack(sel_s), jnp.stack(win_p), jnp.stack(win_s))
```

```python
import functools

import jax
import jax.numpy as jnp
from jax import lax
from jax.experimental import pallas as pl
from jax.experimental.pallas import tpu as pltpu

F32 = jnp.float32
BF16 = jnp.bfloat16
I32 = jnp.int32
U32 = jnp.uint32

N_HEADS = 32
N_KV_HEADS = 4
CMP_BLOCK = 32
CMP_STRIDE = 16
SEL_BLOCK = 64
N_SEL = 16
WINDOW = 512
PAGE_SIZE = 128
TOP_K = 4
SWIGLU_LIMIT = 7.0
SWIGLU_ALPHA = 1.702
EPS = 1e-6
FORCE_SCORE = 1e4
MASKED = -1e30
LOG2_E = 1.4426950408889634
MOE_SUB_ROWS = 272
MOE_SUBS_PER_SUPERBLOCK = 4

LANES = 128
SUBLANES = 8
VMEM_PHYSICAL_BYTES = 64 * 1024 * 1024
VMEM_CAP_BYTES = VMEM_PHYSICAL_BYTES - 6 * 1024 * 1024


def _vmem_limit(block_bytes):
    return int(min(VMEM_CAP_BYTES, block_bytes * 5 // 4 + (4 << 20)))


def _params(sem, block_bytes):
    return pltpu.CompilerParams(dimension_semantics=sem, vmem_limit_bytes=_vmem_limit(block_bytes))


def _tile(n, pref):
    if n <= pref:
        return n
    t = pref
    while n % t:
        t //= 2
    return t


def _dot(a, b):
    return jnp.dot(a, b, preferred_element_type=F32)


def _dot_nt(a, b):
    return lax.dot_general(a, b, (((1,), (1,)), ((), ())), preferred_element_type=F32)


def _split3(x):
    hi = x.astype(BF16)
    r = x - hi.astype(F32)
    mid = r.astype(BF16)
    lo = (r - mid.astype(F32)).astype(BF16)
    return hi, mid, lo


def _iota(shape, dim):
    return lax.broadcasted_iota(I32, shape, dim)


def _rms(a):
    return a * lax.rsqrt(jnp.mean(a * a, axis=-1, keepdims=True) + EPS)


def _adaln_kernel(c_ref, w_ref, b_ref, o_ref, *, kc):
    c = c_ref[...]
    a = (c * jax.nn.sigmoid(c)).astype(BF16)
    acc = jnp.zeros(o_ref.shape, F32)
    for k0 in range(0, a.shape[1], kc):
        acc = acc + _dot(a[:, k0:k0 + kc], w_ref[k0:k0 + kc, :].astype(BF16))
    o_ref[...] = acc + b_ref[...]


def _adaln(c_all, w_mod, b_mod):
    n_layers, d, n6 = w_mod.shape
    r = c_all.shape[0]
    tn = _tile(n6, 1024)
    kc = _tile(d, 1024)
    blk = 2 * d * tn * 4 + d * tn * 2 + r * d * 4
    return pl.pallas_call(
        functools.partial(_adaln_kernel, kc=kc),
        out_shape=jax.ShapeDtypeStruct((n_layers, r, n6), F32),
        grid=(n_layers, n6 // tn),
        in_specs=[
            pl.BlockSpec((r, d), lambda l, j: (0, 0)),
            pl.BlockSpec((None, d, tn), lambda l, j: (l, 0, j)),
            pl.BlockSpec((None, 1, tn), lambda l, j: (l, 0, j)),
        ],
        out_specs=pl.BlockSpec((None, r, tn), lambda l, j: (l, 0, j)),
        compiler_params=_params(("arbitrary", "arbitrary"), blk),
        name="adaln_mod",
    )(c_all, w_mod, b_mod.reshape(n_layers, 1, n6))


def _modulated(x_ref, g_ref, sc_ref, sh_ref):
    return _rms(x_ref[...]) * g_ref[...] * (1.0 + sc_ref[...]) + sh_ref[...]


def _norm_mod_kernel(x_ref, g_ref, sc_ref, sh_ref, h_ref):
    h_ref[...] = _modulated(x_ref, g_ref, sc_ref, sh_ref).astype(h_ref.dtype)


def _mod_spec(mod, tm, d):
    if mod.shape[1] == 1:
        return pl.BlockSpec((None, 1, d), lambda b, i: (b, 0, 0))
    return pl.BlockSpec((None, tm, d), lambda b, i: (b, i, 0))


def _norm_mod(x, g, scale, shift):
    b, t, d = x.shape
    tm = _tile(t, 512)
    blk = 2 * tm * d * (4 + 2) + 6 * d * 4 + 2 * tm * d * 4
    return pl.pallas_call(
        _norm_mod_kernel,
        out_shape=jax.ShapeDtypeStruct((b, t, d), BF16),
        grid=(b, t // tm),
        in_specs=[
            pl.BlockSpec((None, tm, d), lambda b_, i: (b_, i, 0)),
            pl.BlockSpec((1, d), lambda b_, i: (0, 0)),
            _mod_spec(scale, tm, d),
            _mod_spec(shift, tm, d),
        ],
        out_specs=pl.BlockSpec((None, tm, d), lambda b_, i: (b_, i, 0)),
        compiler_params=_params(("arbitrary", "arbitrary"), blk),
        name="norm_mod",
    )(x, g.reshape(1, d), scale, shift)


def _norm_router_kernel(x_ref, g_ref, sc_ref, sh_ref, wr_ref, br_ref, hp_ref, ti_ref, gt_ref):
    h = _modulated(x_ref, g_ref, sc_ref, sh_ref)
    tm, d = h.shape
    dh = d // 2
    lo = pltpu.bitcast(h[:, :dh].astype(BF16).astype(F32), U32)
    hi = pltpu.bitcast(h[:, dh:].astype(BF16).astype(F32), U32)
    hp_ref[...] = (lo >> 16) | hi

    h1, h2, h3 = _split3(h)
    w1, w2, w3 = _split3(wr_ref[...])
    logits = (_dot(h1, w1) + (_dot(h1, w2) + _dot(h2, w1))
              + (_dot(h2, w2) + _dot(h1, w3) + _dot(h3, w1))) + br_ref[...]
    n_exp = logits.shape[1]
    lane = _iota(logits.shape, 1).astype(F32)
    work = logits
    vals, idxs = [], []
    for _ in range(TOP_K):
        m = jnp.max(work, axis=-1, keepdims=True)
        idx = jnp.min(jnp.where(work == m, lane, float(n_exp)), axis=-1, keepdims=True)
        vals.append(m)
        idxs.append(idx)
        work = jnp.where(lane == idx, -jnp.inf, work)
    es = [jnp.exp(v - vals[0]) for v in vals]
    den = es[0]
    for e in es[1:]:
        den = den + e
    lane_o = _iota((tm, LANES), 1)
    ti = jnp.zeros((tm, LANES), F32)
    gt = jnp.zeros((tm, LANES), F32)
    for k in range(TOP_K):
        ti = jnp.where(lane_o == k, idxs[k], ti)
        gt = jnp.where(lane_o == k, es[k] / den, gt)
    ti_ref[...] = ti.astype(I32)
    gt_ref[...] = gt


def _norm_router(x, g, scale, shift, w_router, b_router):
    b, t, d = x.shape
    n_exp = w_router.shape[1]
    tm = _tile(t, 256)
    blk = 2 * tm * d * 4 + 2 * tm * d * 2 + 8 * tm * d * 4 + 2 * d * n_exp * 4
    row = lambda b_, i: (b_, i, 0)
    return pl.pallas_call(
        _norm_router_kernel,
        out_shape=(jax.ShapeDtypeStruct((b, t, d // 2), U32),
                   jax.ShapeDtypeStruct((b, t, LANES), I32),
                   jax.ShapeDtypeStruct((b, t, LANES), F32)),
        grid=(b, t // tm),
        in_specs=[
            pl.BlockSpec((None, tm, d), row),
            pl.BlockSpec((1, d), lambda b_, i: (0, 0)),
            _mod_spec(scale, tm, d),
            _mod_spec(shift, tm, d),
            pl.BlockSpec((d, n_exp), lambda b_, i: (0, 0)),
            pl.BlockSpec((1, n_exp), lambda b_, i: (0, 0)),
        ],
        out_specs=(pl.BlockSpec((None, tm, d // 2), row),
                   pl.BlockSpec((None, tm, LANES), row),
                   pl.BlockSpec((None, tm, LANES), row)),
        compiler_params=_params(("arbitrary", "arbitrary"), blk),
        name="norm_router",
    )(x, g.reshape(1, d), scale, shift, w_router, b_router.reshape(1, n_exp))


def _mm_kernel(a_ref, w_ref, *rest, epi, head_dim, scale):
    *ins, o_ref, wb = rest

    @pl.when(pl.program_id(1) == 0)
    def _():
        wb[...] = w_ref[...].astype(BF16)

    acc = _dot(a_ref[...], wb[...])
    tn = acc.shape[1]
    if epi == "res":
        x_ref, g_ref = ins
        o_ref[...] = x_ref[...] + g_ref[...] * acc
    elif epi == "qnorm":
        (gq_ref,) = ins
        for c in range(tn // head_dim):
            a = acc[:, c * head_dim:(c + 1) * head_dim]
            o_ref[:, c * head_dim:(c + 1) * head_dim] = (_rms(a) * gq_ref[...] * scale).astype(o_ref.dtype)
    elif epi == "kvnorm":
        fl_ref, gk_ref = ins
        for c in range(tn // head_dim):
            sl = slice(c * head_dim, (c + 1) * head_dim)
            a = acc[:, sl]
            o_ref[:, sl] = jnp.where(fl_ref[:, sl] > 0.5, _rms(a) * gk_ref[:, sl], a)
    elif epi == "sigmoid":
        o_ref[...] = jax.nn.sigmoid(acc)
    else:
        raise ValueError(epi)


def _mm(a, w, *, col0, n, epi, extra=(), rows_per_batch=None, out_dtype=F32, head_dim=LANES, scale=1.0,
        name="mm"):
    m, k = a.shape
    tm = _tile(m, 512)
    tn = _tile(n, 512)
    assert col0 % tn == 0
    j0 = col0 // tn
    in_specs = [pl.BlockSpec((tm, k), lambda j, i: (i, 0)),
                pl.BlockSpec((k, tn), lambda j, i: (0, j + j0))]
    operands = [a, w]
    if epi == "res":
        x, g = extra
        in_specs.append(pl.BlockSpec((tm, tn), lambda j, i: (i, j)))
        if g.ndim == 3:
            assert rows_per_batch % tm == 0
            in_specs.append(pl.BlockSpec((None, 1, tn), lambda j, i: (i * tm // rows_per_batch, 0, j)))
        else:
            in_specs.append(pl.BlockSpec((tm, tn), lambda j, i: (i, j)))
        operands += [x, g]
    elif epi == "qnorm":
        in_specs.append(pl.BlockSpec((1, head_dim), lambda j, i: (0, 0)))
        operands += list(extra)
    elif epi == "kvnorm":
        in_specs += [pl.BlockSpec((1, tn), lambda j, i: (0, j))] * 2
        operands += list(extra)
    blk = 2 * tm * k * 2 + 2 * k * tn * 4 + k * tn * 2 + 6 * tm * tn * 4
    return pl.pallas_call(
        functools.partial(_mm_kernel, epi=epi, head_dim=head_dim, scale=scale),
        out_shape=jax.ShapeDtypeStruct((m, n), out_dtype),
        grid=(n // tn, m // tm),
        in_specs=in_specs,
        out_specs=pl.BlockSpec((tm, tn), lambda j, i: (i, j)),
        scratch_shapes=[pltpu.VMEM((k, tn), BF16)],
        compiler_params=_params(("arbitrary", "arbitrary"), blk),
        name=name,
    )(*operands)


def _conv_in_kernel(a_ref, wb_ref, wc_ref, wu_ref, cw_ref, *rest, tiles_per_batch, seq, per_token_prev):
    if per_token_prev:
        p1_ref, p2_ref, o_ref, v_ref, wbuf, ext = rest
    else:
        o_ref, st_ref, wbuf, ext = rest
    i = pl.program_id(1)

    @pl.when(i == 0)
    def _():
        wbuf[0] = wb_ref[...].astype(BF16)
        wbuf[1] = wc_ref[...].astype(BF16)
        wbuf[2] = wu_ref[...].astype(BF16)

    a = a_ref[...]
    b_gate = _dot(a, wbuf[0])
    v = _dot(a, wbuf[1]) * _dot(a, wbuf[2])
    tm, tn = v.shape

    if per_token_prev:
        ext[0:SUBLANES, :] = jnp.zeros((SUBLANES, tn), F32)
    else:
        @pl.when(i % tiles_per_batch == 0)
        def _():
            ext[0:SUBLANES, :] = jnp.zeros((SUBLANES, tn), F32)

    ext[SUBLANES:SUBLANES + tm, :] = v
    s1 = ext[SUBLANES - 1:SUBLANES - 1 + tm, :]
    s2 = ext[SUBLANES - 2:SUBLANES - 2 + tm, :]
    if per_token_prev:
        tpos = _iota((tm, 1), 0) % seq
        s1 = jnp.where(tpos >= 1, s1, p1_ref[...])
        s2 = jnp.where(tpos >= 2, s2, p2_ref[...])
    cw = cw_ref[...]
    conv = s2 * cw[0:1, :] + s1 * cw[1:2, :] + v * cw[2:3, :]
    o_ref[...] = (b_gate * conv).astype(o_ref.dtype)

    if per_token_prev:
        v_ref[...] = v
    else:
        ext[0:SUBLANES, :] = ext[tm:tm + SUBLANES, :]

        @pl.when(i % tiles_per_batch == tiles_per_batch - 1)
        def _():
            st_ref[...] = ext[tm + SUBLANES - 2:tm + SUBLANES, :]


def _conv_in(h, w_in, conv_w, *, batch, seq, prev=None):
    m, d = h.shape
    tn = _tile(d, 256)
    nd = d // tn
    per_token_prev = prev is not None
    tm = m if per_token_prev else _tile(seq, 512)
    tiles_per_batch = max(seq // tm, 1)
    in_specs = [pl.BlockSpec((tm, d), lambda j, i: (i, 0)),
                pl.BlockSpec((d, tn), lambda j, i: (0, j)),
                pl.BlockSpec((d, tn), lambda j, i: (0, j + nd)),
                pl.BlockSpec((d, tn), lambda j, i: (0, j + 2 * nd)),
                pl.BlockSpec((3, tn), lambda j, i: (0, j))]
    operands = [h, w_in, w_in, w_in, conv_w]
    tile_spec = pl.BlockSpec((tm, tn), lambda j, i: (i, j))
    if per_token_prev:
        in_specs += [tile_spec, tile_spec]
        operands += list(prev)
        out_shape = (jax.ShapeDtypeStruct((m, d), BF16), jax.ShapeDtypeStruct((m, d), F32))
        out_specs = (tile_spec, tile_spec)
    else:
        out_shape = (jax.ShapeDtypeStruct((m, d), BF16), jax.ShapeDtypeStruct((batch, 2, d), F32))
        out_specs = (tile_spec, pl.BlockSpec((None, 2, tn), lambda j, i: (i // tiles_per_batch, 0, j)))
    blk = 2 * tm * d * 2 + 6 * d * tn * 4 + 3 * d * tn * 2 + 10 * tm * tn * 4
    return pl.pallas_call(
        functools.partial(_conv_in_kernel, tiles_per_batch=tiles_per_batch, seq=seq,
                          per_token_prev=per_token_prev),
        out_shape=out_shape,
        grid=(nd, m // tm),
        in_specs=in_specs,
        out_specs=out_specs,
        scratch_shapes=[pltpu.VMEM((3, d, tn), BF16), pltpu.VMEM((tm + 2 * SUBLANES, tn), F32)],
        compiler_params=_params(("arbitrary", "arbitrary"), blk),
        name="conv_in",
    )(*operands)


def _conv_prev_rows(state, seq):
    b, _, d = state.shape
    zeros = jnp.zeros((b, seq, d), state.dtype)
    p1 = zeros.at[:, 0].set(state[:, 1])
    p2 = zeros.at[:, 0].set(state[:, 0]).at[:, 1].set(state[:, 1])
    return p1.reshape(b * seq, d), p2.reshape(b * seq, d)


def _plan_kernel(ti_ref, pos_ref, cnt_ref, carry):
    @pl.when(pl.program_id(0) == 0)
    def _():
        carry[...] = jnp.zeros(carry.shape, F32)

    ti = ti_ref[...]
    tm = ti.shape[0]
    e_iota = _iota((tm, LANES), 1)
    onehots = [jnp.where(ti[:, k:k + 1] == e_iota, 1.0, 0.0) for k in range(TOP_K)]
    hits = onehots[0]
    for oh in onehots[1:]:
        hits = hits + oh
    strictly_lower = jnp.where(_iota((tm, tm), 0) > _iota((tm, tm), 1), 1.0, 0.0).astype(BF16)
    before = _dot(strictly_lower, hits.astype(BF16)) + carry[...]
    out = jnp.zeros((tm, LANES), F32)
    for k in range(TOP_K):
        out = jnp.where(e_iota == k, jnp.sum(onehots[k] * before, axis=-1, keepdims=True), out)
    pos_ref[...] = out.astype(I32)
    carry[...] = carry[...] + jnp.sum(hits, axis=0, keepdims=True)
    cnt_ref[...] = carry[...]


def _plan(topi):
    n = topi.shape[0]
    tm = _tile(n, 256)
    return pl.pallas_call(
        _plan_kernel,
        out_shape=(jax.ShapeDtypeStruct((n, LANES), I32), jax.ShapeDtypeStruct((1, LANES), F32)),
        grid=(n // tm,),
        in_specs=[pl.BlockSpec((tm, LANES), lambda i: (i, 0))],
        out_specs=(pl.BlockSpec((tm, LANES), lambda i: (i, 0)), pl.BlockSpec((1, LANES), lambda i: (0, 0))),
        scratch_shapes=[pltpu.VMEM((1, LANES), F32)],
        compiler_params=_params(("arbitrary",), 8 * tm * LANES * 4 + tm * tm * 8),
        name="moe_plan",
    )(topi)


def _dispatch_tables(topi, n_exp, sub, rmax):
    n = topi.shape[0]
    ids = topi[:, :TOP_K]
    n_pairs = n * TOP_K
    n_pad = -(-n // 256) * 256
    pos, counts = _plan(jnp.pad(topi, ((0, n_pad - n), (0, 0)), constant_values=-1))
    counts = counts[0, :n_exp].astype(I32)
    padded = (counts + sub - 1) // sub * sub
    pad_start = jnp.cumsum(padded) - padded
    dest = pad_start[ids] + pos[:n, :TOP_K]
    n_rows = -(-(n_pairs + n_exp * (sub - 1)) // sub) * sub
    row_tok = jnp.zeros((n_rows,), I32).at[dest.reshape(-1)].set(jnp.arange(n_pairs, dtype=I32) // TOP_K)
    n_sb = (counts + rmax - 1) // rmax
    cum = jnp.cumsum(n_sb)
    total = cum[-1]
    n_sb_max = n_exp + n_pairs // rmax
    s = jnp.arange(n_sb_max, dtype=I32)
    e_of = jnp.minimum(jnp.searchsorted(cum, s, side="right"), n_exp - 1).astype(I32)
    local = s - (cum - n_sb)[e_of]
    active = s < total
    rows = jnp.where(active, jnp.minimum(counts[e_of] - local * rmax, rmax), 0).astype(I32)
    start = jnp.where(active, pad_start[e_of] + local * rmax, 0).astype(I32)
    sb_exp = jnp.where(active, e_of, e_of[jnp.maximum(total - 1, 0)]).astype(I32)
    n_used = jnp.sum(padded).astype(I32).reshape(1)
    return dest.astype(I32), row_tok, sb_exp, start, rows, n_used, n_rows, n_sb_max


def _expert_kernel(exp_ref, start_ref, rows_ref, tok_ref, used_ref,
                   h_hbm, wg_ref, wu_ref, bg_ref, bu_ref, wd_ref, bd_ref, ys_hbm,
                   xbuf, act, accg, accu, ostage, gsem, osem, *, n1, sub, dt):
    sb = pl.program_id(0)
    s = pl.program_id(1)
    rows = rows_ref[sb]
    start = start_ref[sb]
    n_sub = (rows + sub - 1) // sub
    dh = xbuf.shape[1]

    @pl.when(jnp.logical_and(sb == 0, s == 0))
    def _():
        ostage[0] = jnp.zeros(ostage.shape[1:], F32)
        n_tail = (ys_hbm.shape[0] - used_ref[0]) // sub

        def tail_copy(t, col):
            row0 = pl.multiple_of(used_ref[0] + t * sub, sub)
            return pltpu.make_async_copy(ostage.at[0], ys_hbm.at[pl.ds(row0, sub), pl.ds(col * dt, dt)],
                                         osem.at[0])

        def issue(t, c):
            for col in range(ys_hbm.shape[1] // dt):
                tail_copy(t, col).start()
            return c

        def drain(t, c):
            for col in range(ys_hbm.shape[1] // dt):
                tail_copy(t, col).wait()
            return c

        lax.fori_loop(0, n_tail, issue, 0)
        lax.fori_loop(0, n_tail, drain, 0)

    def gather_copy(r):
        tok = tok_ref[start + r]
        return pltpu.make_async_copy(h_hbm.at[pl.ds(tok, 1)], xbuf.at[pl.ds(r, 1)], gsem)

    @pl.when(jnp.logical_and(s == 0, rows > 0))
    def _():
        def issue(r, c):
            gather_copy(r).start()
            return c

        def drain(r, c):
            gather_copy(r).wait()
            return c

        lax.fori_loop(0, n_sub * sub, issue, 0)
        lax.fori_loop(0, n_sub * sub, drain, 0)

    wk, wn = wg_ref.shape
    nk = (2 * dh) // wk
    ck = min(wk, 1024)

    @pl.when(jnp.logical_and(sb == 0, s == 0))
    def _():
        accg[...] = jnp.zeros(accg.shape, F32)
        accu[...] = jnp.zeros(accu.shape, F32)

    @pl.when(jnp.logical_and(s < n1, rows > 0))
    def _():
        nh = s // nk
        kq = s % nk
        off = pl.multiple_of((kq % (nk // 2)) * wk, wk)
        shift = jnp.where(kq < nk // 2, 16, 0).astype(U32)
        first = kq == 0

        def sub_block(i, c):
            r = pl.ds(pl.multiple_of(i * sub, sub), sub)
            words = xbuf[r, pl.ds(off, wk)]
            xk = pltpu.bitcast((words << shift) & jnp.uint32(0xFFFF0000), F32).astype(BF16)
            g = jnp.where(first, 0.0, accg[r, :])
            u = jnp.where(first, 0.0, accu[r, :])
            for k0 in range(0, wk, ck):
                g = g + _dot(xk[:, k0:k0 + ck], wg_ref[k0:k0 + ck, :].astype(BF16))
                u = u + _dot(xk[:, k0:k0 + ck], wu_ref[k0:k0 + ck, :].astype(BF16))
            accg[r, :] = g
            accu[r, :] = u
            g = jnp.minimum(g + bg_ref[...], SWIGLU_LIMIT)
            u = jnp.clip(u + bu_ref[...], -SWIGLU_LIMIT, SWIGLU_LIMIT)
            act[nh, r, :] = (g * jax.nn.sigmoid(SWIGLU_ALPHA * g) * (u + 1.0)).astype(BF16)
            return c

        lax.fori_loop(0, n_sub, sub_block, 0)

    @pl.when(jnp.logical_and(s >= n1, rows > 0))
    def _():
        col = pl.multiple_of((s - n1) * dt, dt)

        def out_copy(i, slot):
            row0 = pl.multiple_of(start + i * sub, sub)
            return pltpu.make_async_copy(ostage.at[slot], ys_hbm.at[pl.ds(row0, sub), pl.ds(col, dt)],
                                         osem.at[slot])

        def sub_block(i, c):
            slot = i % 2

            @pl.when(i >= 2)
            def _():
                out_copy(i - 2, slot).wait()

            r = pl.ds(pl.multiple_of(i * sub, sub), sub)
            y = bd_ref[...]
            for f in range(act.shape[0]):
                a = act[f, r, :]
                for k0 in range(0, wn, ck):
                    y = y + _dot(a[:, k0:k0 + ck], wd_ref[f * wn + k0:f * wn + k0 + ck, :].astype(BF16))
            ostage[slot] = y
            out_copy(i, slot).start()
            return c

        lax.fori_loop(0, n_sub, sub_block, 0)

        @pl.when(n_sub >= 2)
        def _():
            out_copy(n_sub - 2, n_sub % 2).wait()

        out_copy(n_sub - 1, (n_sub - 1) % 2).wait()


def _experts(h_packed, tables, w_gu, b_gu, w_down, b_down, *, layer, sub, rmax):
    _, row_tok, sb_exp, sb_start, sb_rows, n_used, n_rows, n_sb_max = tables
    n_layers, n_exp, d, f2 = w_gu.shape
    f = f2 // 2
    wk = _tile(d // 2, 1024)
    wn = _tile(f, 1024)
    nk, nn = d // wk, f // wn
    dt = _tile(d, 512)
    n1, n2 = nn * nk, d // dt

    def gu_step(sb, s, rows_ref):
        return jnp.where(rows_ref[sb] > 0, jnp.minimum(s, n1 - 1), n1 - 1)

    def d_idx(sb, s, rows_ref):
        return jnp.where(rows_ref[sb] > 0, jnp.maximum(s - n1, 0), n2 - 1)

    in_specs = [
        pl.BlockSpec(memory_space=pl.ANY),
        pl.BlockSpec((None, None, wk, wn), lambda sb, s, e, st, rw, tk, us: (
            layer, e[sb], gu_step(sb, s, rw) % nk, gu_step(sb, s, rw) // nk)),
        pl.BlockSpec((None, None, wk, wn), lambda sb, s, e, st, rw, tk, us: (
            layer, e[sb], gu_step(sb, s, rw) % nk, nn + gu_step(sb, s, rw) // nk)),
        pl.BlockSpec((None, None, 1, wn),
                     lambda sb, s, e, st, rw, tk, us: (layer, e[sb], 0, gu_step(sb, s, rw) // nk)),
        pl.BlockSpec((None, None, 1, wn),
                     lambda sb, s, e, st, rw, tk, us: (layer, e[sb], 0, nn + gu_step(sb, s, rw) // nk)),
        pl.BlockSpec((None, None, f, dt),
                     lambda sb, s, e, st, rw, tk, us: (layer, e[sb], 0, d_idx(sb, s, rw))),
        pl.BlockSpec((None, None, 1, dt),
                     lambda sb, s, e, st, rw, tk, us: (layer, e[sb], 0, d_idx(sb, s, rw))),
    ]
    blk = (rmax * (d // 2) * 4 + rmax * f * 2 + 2 * rmax * wn * 4 + 4 * wk * wn * 4 + 2 * f * dt * 4
           + 2 * sub * dt * 4 + 8 * sub * wn * 4)
    return pl.pallas_call(
        functools.partial(_expert_kernel, n1=n1, sub=sub, dt=dt),
        out_shape=jax.ShapeDtypeStruct((n_rows, d), F32),
        grid_spec=pltpu.PrefetchScalarGridSpec(
            num_scalar_prefetch=5,
            grid=(n_sb_max, n1 + n2),
            in_specs=in_specs,
            out_specs=pl.BlockSpec(memory_space=pl.ANY),
            scratch_shapes=[
                pltpu.VMEM((rmax, d // 2), U32),
                pltpu.VMEM((nn, rmax, wn), BF16),
                pltpu.VMEM((rmax, wn), F32),
                pltpu.VMEM((rmax, wn), F32),
                pltpu.VMEM((2, sub, dt), F32),
                pltpu.SemaphoreType.DMA(()),
                pltpu.SemaphoreType.DMA((2,)),
            ]),
        compiler_params=_params(("arbitrary", "arbitrary"), blk),
        name="moe_experts",
    )(sb_exp, sb_start, sb_rows, row_tok, n_used, h_packed, w_gu, w_gu,
      b_gu.reshape(n_layers, n_exp, 1, f2), b_gu.reshape(n_layers, n_exp, 1, f2), w_down,
      b_down.reshape(n_layers, n_exp, 1, d))


def _combine_kernel(dest_ref, ys_hbm, x_ref, g_ref, gate_ref, o_ref, buf, sem, *, tiles_per_batch):
    tm = x_ref.shape[0]
    tok0 = (pl.program_id(0) * tiles_per_batch + pl.program_id(1)) * tm

    def row_copy(r, k):
        src = dest_ref[(tok0 + r) * TOP_K + k]
        return pltpu.make_async_copy(ys_hbm.at[pl.ds(src, 1)], buf.at[k, pl.ds(r, 1)], sem)

    def issue(r, c):
        for k in range(TOP_K):
            row_copy(r, k).start()
        return c

    def drain(r, c):
        for k in range(TOP_K):
            row_copy(r, k).wait()
        return c

    lax.fori_loop(0, tm, issue, 0)
    lax.fori_loop(0, tm, drain, 0)
    gate = gate_ref[...]
    y = gate[:, 0:1] * buf[0]
    for k in range(1, TOP_K):
        y = y + gate[:, k:k + 1] * buf[k]
    o_ref[...] = x_ref[...] + g_ref[...] * y


def _combine(ys, dest, x, g, gate):
    b, t, d = x.shape
    tm = _tile(t, 128)
    tiles_per_batch = t // tm
    row = lambda b_, i, dref: (b_, i, 0)
    if g.shape[1] == 1:
        g_spec = pl.BlockSpec((None, 1, d), lambda b_, i, dref: (b_, 0, 0))
    else:
        g_spec = pl.BlockSpec((None, tm, d), row)
    blk = TOP_K * tm * d * 4 + 6 * tm * d * 4
    return pl.pallas_call(
        functools.partial(_combine_kernel, tiles_per_batch=tiles_per_batch),
        out_shape=jax.ShapeDtypeStruct((b, t, d), F32),
        grid_spec=pltpu.PrefetchScalarGridSpec(
            num_scalar_prefetch=1,
            grid=(b, tiles_per_batch),
            in_specs=[pl.BlockSpec(memory_space=pl.ANY),
                      pl.BlockSpec((None, tm, d), row),
                      g_spec,
                      pl.BlockSpec((None, tm, LANES), row)],
            out_specs=pl.BlockSpec((None, tm, d), row),
            scratch_shapes=[pltpu.VMEM((TOP_K, tm, d), F32), pltpu.SemaphoreType.DMA(())]),
        compiler_params=_params(("arbitrary", "arbitrary"), blk),
        name="moe_combine",
    )(dest.reshape(-1), ys, x, g, gate)


def _moe(xp, xs, g, mods_p, mods_s, w_router, b_router, w_gu, b_gu, w_down, b_down, *, layer):
    (sc_p, sh_p, g_p), (sc_s, sh_s, g_s) = mods_p, mods_s
    d = xp.shape[-1]
    n_exp = w_router.shape[1]
    f = w_down.shape[2]
    hp, ti_p, gt_p = _norm_router(xp, g, sc_p, sh_p, w_router, b_router)
    hs, ti_s, gt_s = _norm_router(xs, g, sc_s, sh_s, w_router, b_router)
    n_p = xp.shape[0] * xp.shape[1]
    h_all = jnp.concatenate([hp.reshape(n_p, d // 2), hs.reshape(-1, d // 2)], axis=0)
    topi = jnp.concatenate([ti_p.reshape(n_p, LANES), ti_s.reshape(-1, LANES)], axis=0)
    sub = MOE_SUB_ROWS
    rmax = MOE_SUBS_PER_SUPERBLOCK * sub
    tables = _dispatch_tables(topi, n_exp, sub, rmax)
    ys = _experts(h_all, tables, w_gu, b_gu, w_down, b_down, layer=layer, sub=sub, rmax=rmax)
    dest = tables[0]
    xp = _combine(ys, dest[:n_p], xp, g_p, gt_p)
    xs = _combine(ys, dest[n_p:], xs, g_s, gt_s)
    return xp, xs


def _nsa_project(h, w_in, q_g, k_g):
    d = h.shape[1]
    hd = LANES
    kvh = N_KV_HEADS
    grp = N_HEADS // kvh
    q_dim = N_HEADS * hd
    kv_dim = 2 * kvh * hd
    q = _mm(h, w_in, col0=0, n=q_dim, epi="qnorm", extra=(q_g.reshape(1, hd),), out_dtype=BF16,
            head_dim=hd, scale=hd ** -0.5 * LOG2_E, name="nsa_q")
    ones = jnp.ones((kvh * hd,), F32)
    zeros = jnp.zeros((kvh * hd,), F32)
    flags = jnp.concatenate([zeros, zeros, ones, zeros, ones, zeros]).reshape(1, 3 * kv_dim)
    gains = jnp.concatenate([ones, ones, jnp.tile(k_g[1], kvh), ones, jnp.tile(k_g[2], kvh), ones])
    kv = _mm(h, w_in, col0=q_dim, n=3 * kv_dim, epi="kvnorm", extra=(flags, gains.reshape(1, 3 * kv_dim)),
             head_dim=hd, name="nsa_kv")
    w_gate = w_in[:, q_dim + 3 * kv_dim:].reshape(d, 3, kvh, grp)
    w_gate = jnp.transpose(w_gate, (0, 2, 1, 3)).reshape(d, kvh, 3 * grp)
    w_gate = jnp.pad(w_gate, ((0, 0), (0, 0), (0, LANES - 3 * grp))).reshape(d, kvh * LANES)
    gates = _mm(h, w_gate, col0=0, n=kvh * LANES, epi="sigmoid", name="nsa_gates")
    return q, kv, gates


def _cmp_part_kernel(pt_ref, *refs, n_pages, kvh, row_view):
    n_refs = n_pages if row_view else n_pages * 2 * kvh
    pages = refs[:n_refs]
    w_ref, pe_ref, p_out, pe_out, wbf = refs[n_refs:]
    rpt = 2 * kvh
    chunks = PAGE_SIZE // CMP_STRIDE

    def chunk_rows(j, h, s):
        if row_view:
            return [pg[pl.ds(s * rpt + j * kvh + h, chunks, stride=CMP_STRIDE * rpt), :] for pg in pages]
        return [pg[pl.ds(s, chunks, stride=CMP_STRIDE), :] for pg in pages[j * kvh + h::rpt]]

    @pl.when(pl.program_id(0) == 0)
    def _():
        for j in range(2):
            wbf[j] = w_ref[j].astype(BF16)
            pe_out[j] = _dot(pe_ref[j].astype(BF16), wbf[j])

    rows = n_pages * PAGE_SIZE // CMP_STRIDE
    for j in range(2):
        per_head = []
        for h in range(kvh):
            pieces = [jnp.concatenate(chunk_rows(j, h, s), axis=0) for s in range(CMP_STRIDE)]
            per_head.append(jnp.concatenate(pieces, axis=1))
        x = jnp.concatenate(per_head, axis=0).astype(BF16)
        y = _dot(x, wbf[j])
        for h in range(kvh):
            p_out[j, h] = y[h * rows:(h + 1) * rows, :]


def _cmp_out_kernel(pk_ref, pv_ref, b1_ref, pe_ref, w2_ref, b2_ref, kg_ref, kc_ref, vc_ref, shifted):
    n_ch = pk_ref.shape[0]
    hid_dim = pk_ref.shape[1] // 2
    for j, (p_ref, o_ref) in enumerate(((pk_ref, kc_ref), (pv_ref, vc_ref))):
        shifted[0:n_ch, :] = p_ref[:, hid_dim:2 * hid_dim]
        shifted[n_ch:n_ch + SUBLANES, :] = jnp.zeros((SUBLANES, hid_dim), F32)
        hid = (b1_ref[j:j + 1, :] + pe_ref[j, 0:1, 0:hid_dim] + pe_ref[j, 1:2, hid_dim:2 * hid_dim]
               + p_ref[:, 0:hid_dim] + shifted[1:1 + n_ch, :])
        a = hid * jax.nn.sigmoid(hid)
        out = _dot(a.astype(BF16), w2_ref[j].astype(BF16)) + b2_ref[j:j + 1, :]
        if j == 0:
            out = _rms(out) * kg_ref[...]
        o_ref[...] = out.astype(o_ref.dtype)


def _compress(src_pages, page_table, batch, w1, b1, w2, b2, pe, kg, *, layer=None):
    kvh = N_KV_HEADS
    hd = LANES
    n_log = page_table.shape[0]
    row_view = layer is not None
    n_pages = _tile(n_log, 16 if row_view else 8)
    chunks_per_page = PAGE_SIZE // CMP_STRIDE
    n_chunks = n_log * chunks_per_page
    hid2 = w1.shape[1] * w1.shape[-1]
    kdim = CMP_STRIDE * hd
    w1cat = jnp.transpose(w1, (0, 2, 3, 1, 4)).reshape(2, kdim, hid2)
    pe_rows = jnp.pad(pe.reshape(2, -1, kdim), ((0, 0), (0, SUBLANES - pe.shape[1]), (0, 0)))

    if row_view:
        rpt = 2 * kvh
        src_pages = src_pages.reshape(src_pages.shape[0], src_pages.shape[1], PAGE_SIZE * rpt, hd)
        page_specs = [pl.BlockSpec((None, None, PAGE_SIZE * rpt, hd),
                                   lambda i, pt, r=r: (layer, pt[i * n_pages + r], 0, 0)) for r in range(n_pages)]
    else:
        page_specs = [pl.BlockSpec((None, PAGE_SIZE, hd),
                                   lambda i, pt, r=r, cb=cb: (pt[i * n_pages + r], 0, cb))
                      for r in range(n_pages) for cb in range(2 * kvh)]
    rows = n_pages * chunks_per_page
    blk = (2 * n_pages * PAGE_SIZE * 2 * kvh * hd * 4 + 3 * 2 * kdim * hid2 * 4
           + 4 * kvh * rows * (kdim + hid2) * 4)
    parts, pe_out = pl.pallas_call(
        functools.partial(_cmp_part_kernel, n_pages=n_pages, kvh=kvh, row_view=row_view),
        out_shape=(jax.ShapeDtypeStruct((2, kvh, n_chunks, hid2), F32),
                   jax.ShapeDtypeStruct((2, SUBLANES, hid2), F32)),
        grid_spec=pltpu.PrefetchScalarGridSpec(
            num_scalar_prefetch=1,
            grid=(n_log // n_pages,),
            in_specs=page_specs + [
                pl.BlockSpec((2, kdim, hid2), lambda i, pt: (0, 0, 0)),
                pl.BlockSpec((2, SUBLANES, kdim), lambda i, pt: (0, 0, 0))],
            out_specs=(pl.BlockSpec((2, kvh, rows, hid2), lambda i, pt: (0, 0, i, 0)),
                       pl.BlockSpec((2, SUBLANES, hid2), lambda i, pt: (0, 0, 0))),
            scratch_shapes=[pltpu.VMEM((2, kdim, hid2), BF16)]),
        compiler_params=_params(("arbitrary",), blk),
        name="cmp_part",
    )(page_table, *([src_pages] * len(page_specs)), w1cat, pe_rows)

    n_ch = n_chunks // batch
    hid = hid2 // 2
    part_spec = lambda j: pl.BlockSpec((None, None, n_ch, hid2), lambda b, h: (j, h, b, 0))
    full = lambda *shape: pl.BlockSpec(shape, lambda b, h: (0,) * len(shape))
    out_spec = pl.BlockSpec((None, None, n_ch, hd), lambda b, h: (b, h, 0, 0))
    return pl.pallas_call(
        _cmp_out_kernel,
        out_shape=(jax.ShapeDtypeStruct((batch, kvh, n_ch, hd), BF16),) * 2,
        grid=(batch, kvh),
        in_specs=[part_spec(0), part_spec(1), full(2, hid), full(2, SUBLANES, hid2), full(2, hid, hd),
                  full(2, hd), full(1, hd)],
        out_specs=(out_spec, out_spec),
        scratch_shapes=[pltpu.VMEM((n_ch + SUBLANES, hid), F32)],
        compiler_params=_params(("arbitrary", "arbitrary"), 12 * n_ch * hid2 * 4),
        name="cmp_out",
    )(parts, parts, b1, pe_out, w2, b2, kg.reshape(1, hd))


def _overlap(n_ch, n_blk):
    c0 = _iota((n_ch, n_blk), 0) * CMP_STRIDE
    b0 = _iota((n_ch, n_blk), 1) * SEL_BLOCK
    return jnp.where(c0 < b0 + SEL_BLOCK, jnp.where(c0 + CMP_BLOCK > b0, 1.0, 0.0), 0.0).astype(BF16)


def _cmp_scores(q, kc, slope, tpos, n_cmp):
    n_ch = kc.shape[0]
    c_idx = _iota((1, n_ch), 1)
    d_c = tpos - (c_idx * CMP_STRIDE + (CMP_BLOCK - 1))
    valid = jnp.where(c_idx < n_cmp, d_c, -1) >= 0
    s = _dot_nt(q, kc) - slope * d_c.astype(F32)
    s = jnp.where(valid, s, -jnp.inf)
    m = jnp.max(s, axis=-1, keepdims=True)
    m = jnp.where(m == -jnp.inf, 0.0, m)
    e = jnp.exp2(s - m)
    return e / jnp.maximum(jnp.sum(e, axis=-1, keepdims=True), 1e-30)


def _with_position_lanes(k, pos0):
    rows, hd = k.shape
    pos = pos0 + _iota((rows, hd), 0)
    lane = _iota((rows, hd), 1)
    ext = jnp.where(lane < 3, pos // LANES * LANES, jnp.where(lane < 6, pos % LANES, 0))
    return jnp.concatenate([k, ext.astype(F32).astype(BF16)], axis=1)


def _with_slope_lanes(q, parts):
    lane = _iota(q.shape, 1)
    ext = jnp.zeros(q.shape, F32)
    for c, part in enumerate(parts):
        ext = jnp.where(lane == c, part, jnp.where(lane == c + 3, part, ext))
    return jnp.concatenate([q, ext.astype(BF16)], axis=1)


def _with_ones(v):
    return jnp.concatenate([v, jnp.ones(v.shape, BF16)], axis=1)


def _force_and_mask(imp, tpos):
    blk = _iota((1, imp.shape[1]), 1)
    cur = tpos // SEL_BLOCK
    forced = jnp.logical_or(blk == 0, jnp.logical_or(blk == cur, blk == cur - 1))
    imp = jnp.where(forced, FORCE_SCORE, imp)
    return jnp.where(blk > cur, -jnp.inf, imp)


def _attn_prompt_kernel(slope_ref, q_ref, kc_ref, vc_ref, ks_ref, vs_ref, kw_ref, vw_ref, gt_ref, o_ref,
                        m_s, acc_s, oc_s, qa_s, *, grp, seq, tq, tk, n_cmp, n_blk, n_top, wl):
    hk = pl.program_id(1)
    t0 = pl.program_id(2) * tq
    hd = LANES
    n_heads = slope_ref.shape[0] // 4
    row_t = t0 + _iota((tq, 1), 0)
    heads = [(g, slope_ref[hk * grp + g], slice(g * hd, (g + 1) * hd)) for g in range(grp)]

    kc = kc_ref[...]
    vc = vc_ref[...]
    n_ch = kc.shape[0]
    psum = jnp.zeros((tq, n_ch), F32)
    for g, slope, cols in heads:
        q = q_ref[:, cols]
        p = _cmp_scores(q, kc, slope, row_t, n_cmp)
        psum = psum + p
        oc_s[g] = _dot(p.astype(BF16), vc)
        qa_s[g] = _with_slope_lanes(q, [slope_ref[(1 + c) * n_heads + hk * grp + g] for c in range(3)])
    ov = _overlap(n_ch, n_blk)
    p1, p2, p3 = _split3(psum)
    imp = _force_and_mask(_dot(p1, ov) + _dot(p2, ov) + _dot(p3, ov), row_t)
    blk = _iota((1, n_blk), 1)
    rank = jnp.zeros((tq, n_blk), F32)
    for j in range(n_blk):
        cj = imp[:, j:j + 1]
        tie = jnp.where(blk > j, 1.0, 0.0)
        rank = rank + jnp.where(cj > imp, 1.0, jnp.where(cj == imp, tie, 0.0))
    sel = jnp.where(rank < n_top, 1.0, 0.0).astype(BF16)

    m_s[...] = jnp.full(m_s.shape, MASKED, F32)
    acc_s[...] = jnp.zeros(acc_s.shape, F32)

    def kv_tile(kt, carry):
        k0 = pl.multiple_of(kt * tk, tk)
        ka = _with_position_lanes(ks_ref[pl.ds(k0, tk), :].astype(BF16), k0)
        va = _with_ones(vs_ref[pl.ds(k0, tk), :].astype(BF16))
        expand = jnp.where(_iota((n_blk, tk), 0) == (k0 + _iota((n_blk, tk), 1)) // SEL_BLOCK, 1.0, 0.0)
        picked = _dot(sel, expand.astype(BF16))
        ok = jnp.where(row_t - (k0 + _iota((1, tk), 1)) >= 0, picked, 0.0) > 0.5
        for g, _, _ in heads:
            s = jnp.where(ok, _dot_nt(qa_s[g], ka), MASKED)
            m_old = m_s[g]
            m_new = jnp.maximum(m_old, jnp.max(s, axis=-1, keepdims=True))
            p = jnp.exp2(s - m_new)
            acc_s[g] = jnp.exp2(m_old - m_new) * acc_s[g] + _dot(p.astype(BF16), va)
            m_s[g] = m_new
        return carry

    lax.fori_loop(0, (t0 + tq + tk - 1) // tk, kv_tile, 0)

    ws = pl.multiple_of(jnp.clip(t0 + tq - wl, 0, seq - wl), SUBLANES)
    kwa = _with_position_lanes(kw_ref[pl.ds(ws, wl), :].astype(BF16), ws)
    vwa = _with_ones(vw_ref[pl.ds(ws, wl), :].astype(BF16))
    dw = row_t - (ws + _iota((1, wl), 1))
    okw = jnp.where(dw >= 0, jnp.where(dw < WINDOW, 1.0, 0.0), 0.0) > 0.5
    gates = gt_ref[...]
    for g, _, cols in heads:
        s = jnp.where(okw, _dot_nt(qa_s[g], kwa), MASKED)
        e = jnp.exp2(s - jnp.max(s, axis=-1, keepdims=True))
        win = _dot(e.astype(BF16), vwa)
        o_w = win[:, :hd] / win[:, hd:hd + 1]
        acc = acc_s[g]
        o_s = acc[:, :hd] / acc[:, hd:hd + 1]
        out = (gates[:, g:g + 1] * oc_s[g] + gates[:, grp + g:grp + g + 1] * o_s
               + gates[:, 2 * grp + g:2 * grp + g + 1] * o_w)
        o_ref[:, cols] = out.astype(o_ref.dtype)


def _attend_prompt(q, kc, vc, kv, gates, slopes, *, batch, seq):
    kvh = N_KV_HEADS
    grp = N_HEADS // kvh
    hd = LANES
    tq = _tile(seq, 256)
    tk = _tile(seq, 512)
    nq = seq // tq
    n_ch = kc.shape[2]
    n_blk = -(-seq // SEL_BLOCK)
    wl = min(tq + WINDOW, seq)
    kv3 = kv.reshape(batch, seq, kv.shape[1])
    row = lambda b, h, i: (b * nq + i, h)
    kv_spec = lambda cb: pl.BlockSpec((None, seq, hd), lambda b, h, i: (b, 0, cb * kvh + h))
    cmp_spec = pl.BlockSpec((None, None, n_ch, hd), lambda b, h, i: (b, h, 0, 0))
    blk = (8 * seq * hd * 4 + 4 * tq * grp * hd * 2 + 3 * grp * tq * hd * 4 + 10 * tq * max(tk, wl) * 4
           + 4 * n_ch * hd * 2)
    return pl.pallas_call(
        functools.partial(_attn_prompt_kernel, grp=grp, seq=seq, tq=tq, tk=tk, n_cmp=n_ch - 1, n_blk=n_blk,
                          n_top=min(N_SEL, n_blk), wl=wl),
        out_shape=jax.ShapeDtypeStruct((batch * seq, N_HEADS * hd), BF16),
        grid=(batch, kvh, nq),
        in_specs=[pl.BlockSpec(memory_space=pltpu.SMEM),
                  pl.BlockSpec((tq, grp * hd), row), cmp_spec, cmp_spec,
                  kv_spec(2), kv_spec(3), kv_spec(4), kv_spec(5),
                  pl.BlockSpec((tq, LANES), row)],
        out_specs=pl.BlockSpec((tq, grp * hd), row),
        scratch_shapes=[pltpu.VMEM((grp, tq, 1), F32), pltpu.VMEM((grp, tq, 2 * hd), F32),
                        pltpu.VMEM((grp, tq, hd), F32), pltpu.VMEM((grp, tq, 2 * hd), BF16)],
        compiler_params=_params(("arbitrary", "arbitrary", "arbitrary"), blk),
        name="nsa_attend_prompt",
    )(slopes, q, kc, vc, kv3, kv3, kv3, kv3, gates)


def _attn_sample_kernel(pt_ref, slope_ref, q_ref, kc_ref, vc_ref, *rest, n_pages, kvh, grp, dec_seq, past, n_cmp,
                        n_blk, n_top, nb_pad):
    pages = rest[:n_pages]
    new_ref, win_ref, gt_ref, o_ref, m_s, l_s, acc_s, oc_s, sel_s = rest[n_pages:]
    c = pl.program_id(1)
    rows = q_ref.shape[1]
    hd = LANES
    rpt = 2 * kvh
    row = _iota((rows, 1), 0)
    t_row = row // grp
    g_row = row % grp
    tpos = past + t_row
    tk = n_pages * PAGE_SIZE
    k0 = c * tk
    expand = jnp.where(_iota((nb_pad, tk), 0) == (k0 + _iota((nb_pad, tk), 1)) // SEL_BLOCK, 1.0, 0.0).astype(BF16)
    d_past = tpos - (k0 + _iota((1, tk), 1))
    d_past_f = d_past.astype(F32)

    def head_rows(ref, j, hk, n):
        return ref[pl.ds(j * kvh + hk, n, stride=rpt), :].astype(BF16)

    for hk in range(kvh):
        q = q_ref[hk]
        slope = jnp.zeros((rows, 1), F32)
        for g in range(grp):
            slope = jnp.where(g_row == g, slope_ref[hk * grp + g], slope)

        def online_update(s, v, hk=hk):
            m_old = m_s[hk]
            m_new = jnp.maximum(m_old, jnp.max(s, axis=-1, keepdims=True))
            alpha = jnp.exp2(m_old - m_new)
            p = jnp.exp2(s - m_new)
            l_s[hk] = alpha * l_s[hk] + jnp.sum(p, axis=-1, keepdims=True)
            acc_s[hk] = alpha * acc_s[hk] + _dot(p.astype(BF16), v)
            m_s[hk] = m_new

        @pl.when(c == 0)
        def _(hk=hk, q=q, slope=slope):
            kc = kc_ref[hk]
            n_ch = kc.shape[0]
            p = _cmp_scores(q, kc, slope, tpos, n_cmp)
            oc_s[hk] = _dot(p.astype(BF16), vc_ref[hk])
            ov = _overlap(n_ch, nb_pad)
            p1, p2, p3 = _split3(p)
            per_head = _dot(p1, ov) + _dot(p2, ov) + _dot(p3, ov)
            same_tok = jnp.where(_iota((rows, rows), 0) // grp == _iota((rows, rows), 1) // grp, 1.0, 0.0)
            same_tok = same_tok.astype(BF16)
            a1, a2, a3 = _split3(per_head)
            imp = _force_and_mask(_dot(same_tok, a1) + _dot(same_tok, a2) + _dot(same_tok, a3), tpos)
            blk_f = _iota((1, nb_pad), 1).astype(F32)
            taken = jnp.where(blk_f >= n_blk, 1.0, 0.0) + jnp.zeros((rows, nb_pad), F32)
            sel = jnp.zeros((rows, nb_pad), F32)
            for _ in range(n_top):
                avail = jnp.where(taken > 0.5, -jnp.inf, imp)
                best = jnp.max(avail, axis=-1, keepdims=True)
                cand = jnp.where(taken > 0.5, 0.0, jnp.where(avail == best, 1.0, 0.0))
                idx = jnp.min(jnp.where(cand > 0.5, blk_f, float(nb_pad)), axis=-1, keepdims=True)
                pick = blk_f == idx
                taken = jnp.where(pick, 1.0, taken)
                sel = jnp.where(pick, 1.0, sel)
            sel_s[hk] = sel.astype(BF16)
            m_s[hk] = jnp.full((rows, 1), MASKED, F32)
            l_s[hk] = jnp.zeros((rows, 1), F32)
            acc_s[hk] = jnp.zeros((rows, hd), F32)

        kk = jnp.concatenate([head_rows(pg, 0, hk, PAGE_SIZE) for pg in pages], axis=0)
        vv = jnp.concatenate([head_rows(pg, 1, hk, PAGE_SIZE) for pg in pages], axis=0)
        picked = _dot(sel_s[hk], expand)
        ok = jnp.where(d_past >= 0, picked, 0.0) > 0.5
        online_update(jnp.where(ok, _dot_nt(q, kk) - slope * d_past_f, MASKED), vv)

        @pl.when(c == pl.num_programs(1) - 1)
        def _(hk=hk, q=q, slope=slope, online_update=online_update):
            new_cols = lambda cb: slice((cb * kvh + hk) * hd, (cb * kvh + hk + 1) * hd)
            t_new = _iota((1, new_ref.shape[0]), 1)
            dn = t_row - t_new
            causal_new = jnp.where(dn >= 0, jnp.where(t_new < dec_seq, 1.0, 0.0), 0.0)
            dnf = dn.astype(F32)
            online_update(jnp.where(causal_new > 0.5, _dot_nt(q, new_ref[:, new_cols(2)].astype(BF16)) - slope * dnf,
                                    MASKED), new_ref[:, new_cols(3)].astype(BF16))
            o_s = acc_s[hk] / l_s[hk]
            n_buf = win_ref.shape[0] // rpt
            dc = tpos - (past - n_buf + _iota((1, n_buf), 1))
            ok_c = jnp.where(dc >= 0, jnp.where(dc < WINDOW, 1.0, 0.0), 0.0) > 0.5
            s_c = jnp.where(ok_c, _dot_nt(q, head_rows(win_ref, 0, hk, n_buf)) - slope * dc.astype(F32), MASKED)
            ok_n = jnp.where(dn < WINDOW, causal_new, 0.0) > 0.5
            s_n = jnp.where(ok_n, _dot_nt(q, new_ref[:, new_cols(4)].astype(BF16)) - slope * dnf, MASKED)
            m = jnp.maximum(jnp.max(s_c, axis=-1, keepdims=True), jnp.max(s_n, axis=-1, keepdims=True))
            e_c = jnp.exp2(s_c - m)
            e_n = jnp.exp2(s_n - m)
            den = jnp.sum(e_c, axis=-1, keepdims=True) + jnp.sum(e_n, axis=-1, keepdims=True)
            o_w = (_dot(e_c.astype(BF16), head_rows(win_ref, 1, hk, n_buf))
                   + _dot(e_n.astype(BF16), new_ref[:, new_cols(5)].astype(BF16))) / den
            gates = gt_ref[hk]
            o_ref[hk] = (gates[:, 0:1] * oc_s[hk] + gates[:, 1:2] * o_s + gates[:, 2:3] * o_w).astype(o_ref.dtype)


def _attend_sample(q_rows, kc, vc, cache_sel, page_table, kv_new, cache_win, gate_rows, slopes, *, layer, past,
                   dec_seq):
    batch, kvh, rows, hd = q_rows.shape
    grp = N_HEADS // kvh
    assert past % SEL_BLOCK == 0 and dec_seq <= SEL_BLOCK and past % PAGE_SIZE == 0
    pages_per_batch = past // PAGE_SIZE
    n_pages = _tile(pages_per_batch, 8)
    n_ch = kc.shape[2]
    n_blk = -(-(past + dec_seq) // SEL_BLOCK)
    nb_pad = -(-n_blk // LANES) * LANES
    n_layers, n_phys = cache_sel.shape[:2]
    n_buf = cache_win.shape[2]
    rpt = 2 * kvh
    sel_rows = cache_sel.reshape(n_layers, n_phys, PAGE_SIZE * rpt, hd)
    win_rows = cache_win.reshape(n_layers, batch, n_buf * rpt, hd)

    def page_spec(r):
        return pl.BlockSpec((None, None, PAGE_SIZE * rpt, hd),
                            lambda b, c, pt: (layer, pt[b * pages_per_batch + c * n_pages + r], 0, 0))

    per_batch = lambda *shape: pl.BlockSpec((None,) + shape, lambda b, c, pt: (b,) + (0,) * len(shape))
    row_spec = per_batch(kvh, rows, hd)
    tk = n_pages * PAGE_SIZE
    blk = (2 * n_pages * PAGE_SIZE * rpt * hd * 4 + 4 * kvh * n_ch * hd * 2 + 2 * n_buf * rpt * hd * 4
           + 16 * rows * max(tk, n_ch, nb_pad) * 4 + 3 * nb_pad * max(tk, n_ch) * 4)
    return pl.pallas_call(
        functools.partial(_attn_sample_kernel, n_pages=n_pages, kvh=kvh, grp=grp, dec_seq=dec_seq, past=past,
                          n_cmp=n_ch - 1, n_blk=n_blk, n_top=min(N_SEL, n_blk), nb_pad=nb_pad),
        out_shape=jax.ShapeDtypeStruct((batch, kvh, rows, hd), BF16),
        grid_spec=pltpu.PrefetchScalarGridSpec(
            num_scalar_prefetch=1,
            grid=(batch, pages_per_batch // n_pages),
            in_specs=[pl.BlockSpec(memory_space=pltpu.SMEM), row_spec, per_batch(kvh, n_ch, hd),
                      per_batch(kvh, n_ch, hd)]
            + [page_spec(r) for r in range(n_pages)]
            + [per_batch(kv_new.shape[1], kv_new.shape[2]),
               pl.BlockSpec((None, None, n_buf * rpt, hd), lambda b, c, pt: (layer, b, 0, 0)), row_spec],
            out_specs=row_spec,
            scratch_shapes=[pltpu.VMEM((kvh, rows, 1), F32), pltpu.VMEM((kvh, rows, 1), F32),
                            pltpu.VMEM((kvh, rows, hd), F32), pltpu.VMEM((kvh, rows, hd), F32),
                            pltpu.VMEM((kvh, rows, nb_pad), BF16)]),
        compiler_params=_params(("arbitrary", "arbitrary"), blk),
        name="nsa_attend_sample",
    )(page_table, slopes, q_rows, kc, vc, *([sel_rows] * n_pages), kv_new, win_rows, gate_rows)


def _nsa_layer(hp, hs, xp, xs, g1p, g1s, caches, page_table, weights, slopes, *, layer, batch, seq, dec_batch,
               dec_seq):
    (w_in, w_out, q_g, k_g, pe, w1, b1, w2, b2) = weights
    cache_cmp, cache_sel, cache_win = caches
    d = xp.shape[-1]
    kvh = N_KV_HEADS
    grp = N_HEADS // kvh
    hd = LANES
    cols = 2 * kvh * hd
    n_s = dec_batch * dec_seq
    past = page_table.shape[1] * PAGE_SIZE
    assert seq % PAGE_SIZE == 0 and past % CMP_STRIDE == 0 and dec_seq < CMP_STRIDE
    cmp_w = (w1, b1, w2, b2, pe, k_g[0])

    q, kv, gates = _nsa_project(hp, w_in, q_g, k_g)
    n_pages_p = batch * seq // PAGE_SIZE
    kc, vc = _compress(kv.reshape(n_pages_p, PAGE_SIZE, 3 * cols), jnp.arange(n_pages_p, dtype=I32), batch, *cmp_w)
    o = _attend_prompt(q, kc, vc, kv, gates, slopes, batch=batch, seq=seq)
    xp = _mm(o, w_out, col0=0, n=d, epi="res", extra=(xp.reshape(batch * seq, d), g1p), rows_per_batch=seq,
             name="nsa_out").reshape(batch, seq, d)
    kv_p = kv.reshape(batch, seq, 3, 2, kvh, hd)
    new_p = (kv_p[:, :, 0], kv_p[:, :, 1], kv_p[:, seq - min(WINDOW, seq):, 2])

    qs, kvs, gs = _nsa_project(hs, w_in, q_g, k_g)
    pt = page_table.reshape(-1)
    kcs, vcs = _compress(cache_cmp, pt, dec_batch, *cmp_w, layer=layer)
    q_rows = jnp.transpose(qs.reshape(dec_batch, dec_seq, kvh, grp, hd), (0, 2, 1, 3, 4))
    q_rows = q_rows.reshape(dec_batch, kvh, dec_seq * grp, hd)
    gate_rows = gs.reshape(dec_batch, dec_seq, kvh, LANES)[..., :3 * grp].reshape(dec_batch, dec_seq, kvh, 3, grp)
    gate_rows = jnp.transpose(gate_rows, (0, 2, 1, 4, 3)).reshape(dec_batch, kvh, dec_seq * grp, 3)
    gate_rows = jnp.pad(gate_rows, ((0, 0), (0, 0), (0, 0), (0, LANES - 3)))
    kv_new = jnp.pad(kvs.reshape(dec_batch, dec_seq, 3 * cols), ((0, 0), (0, SUBLANES - dec_seq), (0, 0)))
    n_buf = cache_win.shape[2]
    o_rows = _attend_sample(q_rows, kcs, vcs, cache_sel, pt, kv_new, cache_win, gate_rows, slopes, layer=layer,
                            past=past, dec_seq=dec_seq)
    o_s = jnp.transpose(o_rows.reshape(dec_batch, kvh, dec_seq, grp, hd), (0, 2, 1, 3, 4)).reshape(n_s, N_HEADS * hd)
    xs = _mm(o_s, w_out, col0=0, n=d, epi="res", extra=(xs.reshape(n_s, d), g1s[0]),
             name="nsa_out").reshape(1, n_s, d)
    kv_s = kvs.reshape(dec_batch, dec_seq, 3, 2, kvh, hd)
    win_s = jnp.concatenate([cache_win[layer], kv_s[:, :, 2]], axis=1)[:, -n_buf:]
    new_s = (kv_s[:, :, 0], kv_s[:, :, 1], win_s)
    return xp, xs, new_p, new_s


def kernel(x_prompt, x_sample, c_prompt, c_sample, state_conv, cache_cmp_kv, cache_sel_kv, cache_win_kv,
           page_table, w_mod, b_mod, norm_g, conv_w_in, conv_w, conv_w_out, nsa_w_in, nsa_w_out, q_norm_g,
           k_norm_g, cmp_pe, cmp_w1, cmp_b1, cmp_w2, cmp_b2, router_w, router_b, moe_w_gu, moe_b_gu,
           moe_w_down, moe_b_down):
    batch, seq, d = x_prompt.shape
    dec_batch, dec_seq, _ = x_sample.shape
    depth = w_mod.shape[0]
    n_s = dec_batch * dec_seq
    assert dec_seq >= conv_w.shape[1] - 1

    c_all = jnp.concatenate([c_prompt, c_sample], axis=0)
    c_all = jnp.pad(c_all, ((0, -c_all.shape[0] % SUBLANES), (0, 0)))
    mod = _adaln(c_all, w_mod, b_mod)
    slope2 = jnp.exp2(-8.0 * jnp.arange(1, N_HEADS + 1, dtype=F32) / N_HEADS) * LOG2_E
    slopes = jnp.concatenate([slope2] + [p.astype(F32) for p in _split3(slope2)])

    xp = x_prompt
    xs = x_sample.reshape(1, n_s, d)
    conv_p, conv_s, cmp_p, cmp_s, sel_p, sel_s, win_p, win_s = [], [], [], [], [], [], [], []
    for i in range(depth):
        j = i // 2
        sh1p, sc1p, g1p, sh2p, sc2p, g2p = [m[:, None, :] for m in jnp.split(mod[i, :batch], 6, axis=-1)]
        sh1s, sc1s, g1s, sh2s, sc2s, g2s = [jnp.repeat(m, dec_seq, axis=0)[None]
                                            for m in jnp.split(mod[i, batch:batch + dec_batch], 6, axis=-1)]
        hp = _norm_mod(xp, norm_g[i, 0], sc1p, sh1p).reshape(batch * seq, d)
        hs = _norm_mod(xs, norm_g[i, 0], sc1s, sh1s).reshape(n_s, d)
        if i % 2 == 0:
            a_p, state_p = _conv_in(hp, conv_w_in[j], conv_w[j], batch=batch, seq=seq)
            a_s, v_s = _conv_in(hs, conv_w_in[j], conv_w[j], batch=dec_batch, seq=dec_seq,
                                prev=_conv_prev_rows(state_conv[j], dec_seq))
            xp = _mm(a_p, conv_w_out[j], col0=0, n=d, epi="res", extra=(xp.reshape(batch * seq, d), g1p),
                     rows_per_batch=seq, name="conv_out").reshape(batch, seq, d)
            xs = _mm(a_s, conv_w_out[j], col0=0, n=d, epi="res", extra=(xs.reshape(n_s, d), g1s[0]),
                     name="conv_out").reshape(1, n_s, d)
            conv_p.append(state_p)
            conv_s.append(v_s.reshape(dec_batch, dec_seq, d)[:, dec_seq - 2:])
        else:
            weights = (nsa_w_in[j], nsa_w_out[j], q_norm_g[j], k_norm_g[j], cmp_pe[j], cmp_w1[j], cmp_b1[j],
                       cmp_w2[j], cmp_b2[j])
            xp, xs, new_p, new_s = _nsa_layer(
                hp, hs, xp, xs, g1p, g1s, (cache_cmp_kv, cache_sel_kv, cache_win_kv), page_table,
                weights, slopes, layer=j, batch=batch, seq=seq, dec_batch=dec_batch, dec_seq=dec_seq)
            cmp_p.append(new_p[0])
            sel_p.append(new_p[1])
            win_p.append(new_p[2])
            cmp_s.append(new_s[0])
            sel_s.append(new_s[1])
            win_s.append(new_s[2])
        xp, xs = _moe(xp, xs, norm_g[i, 1], (sc2p, sh2p, g2p), (sc2s, sh2s, g2s), router_w[i], router_b[i],
                      moe_w_gu, moe_b_gu, moe_w_down, moe_b_down, layer=i)
    return (xp, xs.reshape(dec_batch, dec_seq, d), jnp.stack(conv_p), jnp.stack(conv_s), jnp.stack(cmp_p),
            jnp.stack(cmp_s), jnp.stack(sel_p), jnp.stack(sel_s), jnp.stack(win_p), jnp.stack(win_s))
```

```python
import functools

import jax
import jax.numpy as jnp
from jax import lax
from jax.experimental import pallas as pl
from jax.experimental.pallas import tpu as pltpu

F32 = jnp.float32
BF16 = jnp.bfloat16
I32 = jnp.int32
U32 = jnp.uint32

N_HEADS = 32
N_KV_HEADS = 4
CMP_BLOCK = 32
CMP_STRIDE = 16
SEL_BLOCK = 64
N_SEL = 16
WINDOW = 512
PAGE_SIZE = 128
TOP_K = 4
SWIGLU_LIMIT = 7.0
SWIGLU_ALPHA = 1.702
EPS = 1e-6
FORCE_SCORE = 1e4
MASKED = -1e30
LOG2_E = 1.4426950408889634
MOE_SUB_ROWS = 272
MOE_SUBS_PER_SUPERBLOCK = 4

LANES = 128
SUBLANES = 8
VMEM_PHYSICAL_BYTES = 64 * 1024 * 1024
VMEM_CAP_BYTES = VMEM_PHYSICAL_BYTES - 6 * 1024 * 1024


def _vmem_limit(block_bytes):
    return int(min(VMEM_CAP_BYTES, block_bytes * 5 // 4 + (4 << 20)))


def _params(sem, block_bytes):
    return pltpu.CompilerParams(dimension_semantics=sem, vmem_limit_bytes=_vmem_limit(block_bytes))


def _tile(n, pref):
    if n <= pref:
        return n
    t = pref
    while n % t:
        t //= 2
    return t


def _dot(a, b):
    return jnp.dot(a, b, preferred_element_type=F32)


def _dot_nt(a, b):
    return lax.dot_general(a, b, (((1,), (1,)), ((), ())), preferred_element_type=F32)


def _split3(x):
    hi = x.astype(BF16)
    r = x - hi.astype(F32)
    mid = r.astype(BF16)
    lo = (r - mid.astype(F32)).astype(BF16)
    return hi, mid, lo


def _iota(shape, dim):
    return lax.broadcasted_iota(I32, shape, dim)


def _rms(a):
    return a * lax.rsqrt(jnp.mean(a * a, axis=-1, keepdims=True) + EPS)


def _adaln_kernel(c_ref, w_ref, b_ref, o_ref, *, kc):
    c = c_ref[...]
    a = (c * jax.nn.sigmoid(c)).astype(BF16)
    acc = jnp.zeros(o_ref.shape, F32)
    for k0 in range(0, a.shape[1], kc):
        acc = acc + _dot(a[:, k0:k0 + kc], w_ref[k0:k0 + kc, :].astype(BF16))
    o_ref[...] = acc + b_ref[...]


def _adaln(c_all, w_mod, b_mod):
    n_layers, d, n6 = w_mod.shape
    r = c_all.shape[0]
    tn = _tile(n6, 1024)
    kc = _tile(d, 1024)
    blk = 2 * d * tn * 4 + d * tn * 2 + r * d * 4
    return pl.pallas_call(
        functools.partial(_adaln_kernel, kc=kc),
        out_shape=jax.ShapeDtypeStruct((n_layers, r, n6), F32),
        grid=(n_layers, n6 // tn),
        in_specs=[
            pl.BlockSpec((r, d), lambda l, j: (0, 0)),
            pl.BlockSpec((None, d, tn), lambda l, j: (l, 0, j)),
            pl.BlockSpec((None, 1, tn), lambda l, j: (l, 0, j)),
        ],
        out_specs=pl.BlockSpec((None, r, tn), lambda l, j: (l, 0, j)),
        compiler_params=_params(("arbitrary", "arbitrary"), blk),
        name="adaln_mod",
    )(c_all, w_mod, b_mod.reshape(n_layers, 1, n6))


def _modulated(x_ref, g_ref, sc_ref, sh_ref):
    return _rms(x_ref[...]) * g_ref[...] * (1.0 + sc_ref[...]) + sh_ref[...]


def _norm_mod_kernel(x_ref, g_ref, sc_ref, sh_ref, h_ref):
    h_ref[...] = _modulated(x_ref, g_ref, sc_ref, sh_ref).astype(h_ref.dtype)


def _mod_spec(mod, tm, d):
    if mod.shape[1] == 1:
        return pl.BlockSpec((None, 1, d), lambda b, i: (b, 0, 0))
    return pl.BlockSpec((None, tm, d), lambda b, i: (b, i, 0))


def _norm_mod(x, g, scale, shift):
    b, t, d = x.shape
    tm = _tile(t, 512)
    blk = 2 * tm * d * (4 + 2) + 6 * d * 4 + 2 * tm * d * 4
    return pl.pallas_call(
        _norm_mod_kernel,
        out_shape=jax.ShapeDtypeStruct((b, t, d), BF16),
        grid=(b, t // tm),
        in_specs=[
            pl.BlockSpec((None, tm, d), lambda b_, i: (b_, i, 0)),
            pl.BlockSpec((1, d), lambda b_, i: (0, 0)),
            _mod_spec(scale, tm, d),
            _mod_spec(shift, tm, d),
        ],
        out_specs=pl.BlockSpec((None, tm, d), lambda b_, i: (b_, i, 0)),
        compiler_params=_params(("arbitrary", "arbitrary"), blk),
        name="norm_mod",
    )(x, g.reshape(1, d), scale, shift)


def _norm_router_kernel(x_ref, g_ref, sc_ref, sh_ref, wr_ref, br_ref, hp_ref, ti_ref, gt_ref):
    h = _modulated(x_ref, g_ref, sc_ref, sh_ref)
    tm, d = h.shape
    dh = d // 2
    lo = pltpu.bitcast(h[:, :dh].astype(BF16).astype(F32), U32)
    hi = pltpu.bitcast(h[:, dh:].astype(BF16).astype(F32), U32)
    hp_ref[...] = (lo >> 16) | hi

    h1, h2, h3 = _split3(h)
    w1, w2, w3 = _split3(wr_ref[...])
    logits = (_dot(h1, w1) + (_dot(h1, w2) + _dot(h2, w1))
              + (_dot(h2, w2) + _dot(h1, w3) + _dot(h3, w1))) + br_ref[...]
    n_exp = logits.shape[1]
    lane = _iota(logits.shape, 1).astype(F32)
    work = logits
    vals, idxs = [], []
    for _ in range(TOP_K):
        m = jnp.max(work, axis=-1, keepdims=True)
        idx = jnp.min(jnp.where(work == m, lane, float(n_exp)), axis=-1, keepdims=True)
        vals.append(m)
        idxs.append(idx)
        work = jnp.where(lane == idx, -jnp.inf, work)
    es = [jnp.exp(v - vals[0]) for v in vals]
    den = es[0]
    for e in es[1:]:
        den = den + e
    lane_o = _iota((tm, LANES), 1)
    ti = jnp.zeros((tm, LANES), F32)
    gt = jnp.zeros((tm, LANES), F32)
    for k in range(TOP_K):
        ti = jnp.where(lane_o == k, idxs[k], ti)
        gt = jnp.where(lane_o == k, es[k] / den, gt)
    ti_ref[...] = ti.astype(I32)
    gt_ref[...] = gt


def _norm_router(x, g, scale, shift, w_router, b_router):
    b, t, d = x.shape
    n_exp = w_router.shape[1]
    tm = _tile(t, 256)
    blk = 2 * tm * d * 4 + 2 * tm * d * 2 + 8 * tm * d * 4 + 2 * d * n_exp * 4
    row = lambda b_, i: (b_, i, 0)
    return pl.pallas_call(
        _norm_router_kernel,
        out_shape=(jax.ShapeDtypeStruct((b, t, d // 2), U32),
                   jax.ShapeDtypeStruct((b, t, LANES), I32),
                   jax.ShapeDtypeStruct((b, t, LANES), F32)),
        grid=(b, t // tm),
        in_specs=[
            pl.BlockSpec((None, tm, d), row),
            pl.BlockSpec((1, d), lambda b_, i: (0, 0)),
            _mod_spec(scale, tm, d),
            _mod_spec(shift, tm, d),
            pl.BlockSpec((d, n_exp), lambda b_, i: (0, 0)),
            pl.BlockSpec((1, n_exp), lambda b_, i: (0, 0)),
        ],
        out_specs=(pl.BlockSpec((None, tm, d // 2), row),
                   pl.BlockSpec((None, tm, LANES), row),
                   pl.BlockSpec((None, tm, LANES), row)),
        compiler_params=_params(("arbitrary", "arbitrary"), blk),
        name="norm_router",
    )(x, g.reshape(1, d), scale, shift, w_router, b_router.reshape(1, n_exp))


def _mm_kernel(a_ref, w_ref, *rest, epi, head_dim, scale):
    *ins, o_ref, wb = rest

    @pl.when(pl.program_id(1) == 0)
    def _():
        wb[...] = w_ref[...].astype(BF16)

    acc = _dot(a_ref[...], wb[...])
    tn = acc.shape[1]
    if epi == "res":
        x_ref, g_ref = ins
        o_ref[...] = x_ref[...] + g_ref[...] * acc
    elif epi == "qnorm":
        (gq_ref,) = ins
        for c in range(tn // head_dim):
            a = acc[:, c * head_dim:(c + 1) * head_dim]
            o_ref[:, c * head_dim:(c + 1) * head_dim] = (_rms(a) * gq_ref[...] * scale).astype(o_ref.dtype)
    elif epi == "kvnorm":
        fl_ref, gk_ref = ins
        for c in range(tn // head_dim):
            sl = slice(c * head_dim, (c + 1) * head_dim)
            a = acc[:, sl]
            o_ref[:, sl] = jnp.where(fl_ref[:, sl] > 0.5, _rms(a) * gk_ref[:, sl], a)
    elif epi == "sigmoid":
        o_ref[...] = jax.nn.sigmoid(acc)
    else:
        raise ValueError(epi)


def _mm(a, w, *, col0, n, epi, extra=(), rows_per_batch=None, out_dtype=F32, head_dim=LANES, scale=1.0,
        name="mm"):
    m, k = a.shape
    tm = _tile(m, 1024)
    tn = _tile(n, 512)
    assert col0 % tn == 0
    j0 = col0 // tn
    in_specs = [pl.BlockSpec((tm, k), lambda j, i: (i, 0)),
                pl.BlockSpec((k, tn), lambda j, i: (0, j + j0))]
    operands = [a, w]
    if epi == "res":
        x, g = extra
        in_specs.append(pl.BlockSpec((tm, tn), lambda j, i: (i, j)))
        if g.ndim == 3:
            assert rows_per_batch % tm == 0
            in_specs.append(pl.BlockSpec((None, 1, tn), lambda j, i: (i * tm // rows_per_batch, 0, j)))
        else:
            in_specs.append(pl.BlockSpec((tm, tn), lambda j, i: (i, j)))
        operands += [x, g]
    elif epi == "qnorm":
        in_specs.append(pl.BlockSpec((1, head_dim), lambda j, i: (0, 0)))
        operands += list(extra)
    elif epi == "kvnorm":
        in_specs += [pl.BlockSpec((1, tn), lambda j, i: (0, j))] * 2
        operands += list(extra)
    blk = 2 * tm * k * 2 + 2 * k * tn * 4 + k * tn * 2 + 6 * tm * tn * 4
    return pl.pallas_call(
        functools.partial(_mm_kernel, epi=epi, head_dim=head_dim, scale=scale),
        out_shape=jax.ShapeDtypeStruct((m, n), out_dtype),
        grid=(n // tn, m // tm),
        in_specs=in_specs,
        out_specs=pl.BlockSpec((tm, tn), lambda j, i: (i, j)),
        scratch_shapes=[pltpu.VMEM((k, tn), BF16)],
        compiler_params=_params(("arbitrary", "arbitrary"), blk),
        name=name,
    )(*operands)


def _conv_in_kernel(a_ref, wb_ref, wc_ref, wu_ref, cw_ref, *rest, tiles_per_batch, seq, per_token_prev):
    if per_token_prev:
        p1_ref, p2_ref, o_ref, v_ref, wbuf, ext = rest
    else:
        o_ref, st_ref, wbuf, ext = rest
    i = pl.program_id(1)

    @pl.when(i == 0)
    def _():
        wbuf[0] = wb_ref[...].astype(BF16)
        wbuf[1] = wc_ref[...].astype(BF16)
        wbuf[2] = wu_ref[...].astype(BF16)

    a = a_ref[...]
    b_gate = _dot(a, wbuf[0])
    v = _dot(a, wbuf[1]) * _dot(a, wbuf[2])
    tm, tn = v.shape

    if per_token_prev:
        ext[0:SUBLANES, :] = jnp.zeros((SUBLANES, tn), F32)
    else:
        @pl.when(i % tiles_per_batch == 0)
        def _():
            ext[0:SUBLANES, :] = jnp.zeros((SUBLANES, tn), F32)

    ext[SUBLANES:SUBLANES + tm, :] = v
    s1 = ext[SUBLANES - 1:SUBLANES - 1 + tm, :]
    s2 = ext[SUBLANES - 2:SUBLANES - 2 + tm, :]
    if per_token_prev:
        tpos = _iota((tm, 1), 0) % seq
        s1 = jnp.where(tpos >= 1, s1, p1_ref[...])
        s2 = jnp.where(tpos >= 2, s2, p2_ref[...])
    cw = cw_ref[...]
    conv = s2 * cw[0:1, :] + s1 * cw[1:2, :] + v * cw[2:3, :]
    o_ref[...] = (b_gate * conv).astype(o_ref.dtype)

    if per_token_prev:
        v_ref[...] = v
    else:
        ext[0:SUBLANES, :] = ext[tm:tm + SUBLANES, :]

        @pl.when(i % tiles_per_batch == tiles_per_batch - 1)
        def _():
            st_ref[...] = ext[tm + SUBLANES - 2:tm + SUBLANES, :]


def _conv_in(h, w_in, conv_w, *, batch, seq, prev=None):
    m, d = h.shape
    tn = _tile(d, 256)
    nd = d // tn
    per_token_prev = prev is not None
    tm = m if per_token_prev else _tile(seq, 512)
    tiles_per_batch = max(seq // tm, 1)
    in_specs = [pl.BlockSpec((tm, d), lambda j, i: (i, 0)),
                pl.BlockSpec((d, tn), lambda j, i: (0, j)),
                pl.BlockSpec((d, tn), lambda j, i: (0, j + nd)),
                pl.BlockSpec((d, tn), lambda j, i: (0, j + 2 * nd)),
                pl.BlockSpec((3, tn), lambda j, i: (0, j))]
    operands = [h, w_in, w_in, w_in, conv_w]
    tile_spec = pl.BlockSpec((tm, tn), lambda j, i: (i, j))
    if per_token_prev:
        in_specs += [tile_spec, tile_spec]
        operands += list(prev)
        out_shape = (jax.ShapeDtypeStruct((m, d), BF16), jax.ShapeDtypeStruct((m, d), F32))
        out_specs = (tile_spec, tile_spec)
    else:
        out_shape = (jax.ShapeDtypeStruct((m, d), BF16), jax.ShapeDtypeStruct((batch, 2, d), F32))
        out_specs = (tile_spec, pl.BlockSpec((None, 2, tn), lambda j, i: (i // tiles_per_batch, 0, j)))
    blk = 2 * tm * d * 2 + 6 * d * tn * 4 + 3 * d * tn * 2 + 10 * tm * tn * 4
    return pl.pallas_call(
        functools.partial(_conv_in_kernel, tiles_per_batch=tiles_per_batch, seq=seq,
                          per_token_prev=per_token_prev),
        out_shape=out_shape,
        grid=(nd, m // tm),
        in_specs=in_specs,
        out_specs=out_specs,
        scratch_shapes=[pltpu.VMEM((3, d, tn), BF16), pltpu.VMEM((tm + 2 * SUBLANES, tn), F32)],
        compiler_params=_params(("arbitrary", "arbitrary"), blk),
        name="conv_in",
    )(*operands)


def _conv_prev_rows(state, seq):
    b, _, d = state.shape
    zeros = jnp.zeros((b, seq, d), state.dtype)
    p1 = zeros.at[:, 0].set(state[:, 1])
    p2 = zeros.at[:, 0].set(state[:, 0]).at[:, 1].set(state[:, 1])
    return p1.reshape(b * seq, d), p2.reshape(b * seq, d)


def _plan_kernel(ti_ref, pos_ref, cnt_ref, carry):
    @pl.when(pl.program_id(0) == 0)
    def _():
        carry[...] = jnp.zeros(carry.shape, F32)

    ti = ti_ref[...]
    tm = ti.shape[0]
    e_iota = _iota((tm, LANES), 1)
    onehots = [jnp.where(ti[:, k:k + 1] == e_iota, 1.0, 0.0) for k in range(TOP_K)]
    hits = onehots[0]
    for oh in onehots[1:]:
        hits = hits + oh
    strictly_lower = jnp.where(_iota((tm, tm), 0) > _iota((tm, tm), 1), 1.0, 0.0).astype(BF16)
    before = _dot(strictly_lower, hits.astype(BF16)) + carry[...]
    out = jnp.zeros((tm, LANES), F32)
    for k in range(TOP_K):
        out = jnp.where(e_iota == k, jnp.sum(onehots[k] * before, axis=-1, keepdims=True), out)
    pos_ref[...] = out.astype(I32)
    carry[...] = carry[...] + jnp.sum(hits, axis=0, keepdims=True)
    cnt_ref[...] = carry[...]


def _plan(topi):
    n = topi.shape[0]
    tm = _tile(n, 256)
    return pl.pallas_call(
        _plan_kernel,
        out_shape=(jax.ShapeDtypeStruct((n, LANES), I32), jax.ShapeDtypeStruct((1, LANES), F32)),
        grid=(n // tm,),
        in_specs=[pl.BlockSpec((tm, LANES), lambda i: (i, 0))],
        out_specs=(pl.BlockSpec((tm, LANES), lambda i: (i, 0)), pl.BlockSpec((1, LANES), lambda i: (0, 0))),
        scratch_shapes=[pltpu.VMEM((1, LANES), F32)],
        compiler_params=_params(("arbitrary",), 8 * tm * LANES * 4 + tm * tm * 8),
        name="moe_plan",
    )(topi)


def _dispatch_tables(topi, n_exp, sub, rmax):
    n = topi.shape[0]
    ids = topi[:, :TOP_K]
    n_pairs = n * TOP_K
    n_pad = -(-n // 256) * 256
    pos, counts = _plan(jnp.pad(topi, ((0, n_pad - n), (0, 0)), constant_values=-1))
    counts = counts[0, :n_exp].astype(I32)
    padded = (counts + sub - 1) // sub * sub
    pad_start = jnp.cumsum(padded) - padded
    dest = pad_start[ids] + pos[:n, :TOP_K]
    n_rows = -(-(n_pairs + n_exp * (sub - 1)) // sub) * sub
    row_tok = jnp.zeros((n_rows,), I32).at[dest.reshape(-1)].set(jnp.arange(n_pairs, dtype=I32) // TOP_K)
    n_sb = (counts + rmax - 1) // rmax
    cum = jnp.cumsum(n_sb)
    total = cum[-1]
    n_sb_max = n_exp + n_pairs // rmax + 1
    s = jnp.arange(n_sb_max, dtype=I32)
    e_of = jnp.minimum(jnp.searchsorted(cum, s, side="right"), n_exp - 1).astype(I32)
    local = s - (cum - n_sb)[e_of]
    active = s < total
    rows = jnp.where(active, jnp.minimum(counts[e_of] - local * rmax, rmax), 0).astype(I32)
    start = jnp.where(active, pad_start[e_of] + local * rmax, 0).astype(I32)
    sb_exp = jnp.where(active, e_of, e_of[jnp.maximum(total - 1, 0)]).astype(I32)
    n_used = jnp.sum(padded).astype(I32).reshape(1)
    return dest.astype(I32), row_tok, sb_exp, start, rows, n_used, n_rows, n_sb_max


def _expert_kernel(exp_ref, start_ref, rows_ref, tok_ref, used_ref,
                   h_hbm, wg_ref, wu_ref, bg_ref, bu_ref, wd_ref, bd_ref, ys_hbm,
                   xbuf, act, wgu, wdb, ostage, gsem, osem, *, n1, sub, tf, dt):
    sb = pl.program_id(0)
    s = pl.program_id(1)
    rows = rows_ref[sb]
    start = start_ref[sb]
    n_sub = (rows + sub - 1) // sub
    dh = xbuf.shape[2]

    @pl.when(jnp.logical_and(sb == 0, s == 0))
    def _():
        ostage[0] = jnp.zeros(ostage.shape[1:], F32)
        n_tail = (ys_hbm.shape[0] - used_ref[0]) // sub

        def tail_copy(t, col):
            row0 = pl.multiple_of(used_ref[0] + t * sub, sub)
            return pltpu.make_async_copy(ostage.at[0], ys_hbm.at[pl.ds(row0, sub), pl.ds(col * dt, dt)],
                                         osem.at[0])

        def issue(t, c):
            for col in range(ys_hbm.shape[1] // dt):
                tail_copy(t, col).start()
            return c

        def drain(t, c):
            for col in range(ys_hbm.shape[1] // dt):
                tail_copy(t, col).wait()
            return c

        lax.fori_loop(0, n_tail, issue, 0)
        lax.fori_loop(0, n_tail, drain, 0)

    rmax = xbuf.shape[1]
    slot = sb % 2
    last_tok = tok_ref.shape[0] - 1

    def gather_copy(b, r):
        tok = tok_ref[jnp.minimum(start_ref[b] + r, last_tok)]
        return pltpu.make_async_copy(h_hbm.at[pl.ds(tok, 1)], xbuf.at[b % 2, pl.ds(r, 1)], gsem)

    def drain_gather(b):
        def drain(r, c):
            gather_copy(b, r).wait()
            return c

        lax.fori_loop(0, rmax, drain, 0)

    @pl.when(jnp.logical_and(s == 0, sb == 0))
    def _():
        def issue(r, c):
            gather_copy(0, r).start()
            return c

        lax.fori_loop(0, rmax, issue, 0)
        drain_gather(0)

    @pl.when(jnp.logical_and(s == 0, sb > 0))
    def _():
        @pl.when(rows_ref[jnp.maximum(sb - 1, 0)] > 0)
        def _():
            drain_gather(sb)

    per_step = rmax // (n1 + ys_hbm.shape[1] // dt)
    nxt = sb + 1

    def prefetch_slice():
        for j in range(per_step):
            gather_copy(nxt, s * per_step + j).start()

    def unpack(i):
        words = xbuf[slot, pl.ds(pl.multiple_of(i * sub, sub), sub), :]
        lo = pltpu.bitcast(words << 16, F32).astype(BF16)
        hi = pltpu.bitcast(words & jnp.uint32(0xFFFF0000), F32).astype(BF16)
        return lo, hi

    def swiglu(g, u):
        g = jnp.minimum(g, SWIGLU_LIMIT)
        u = jnp.clip(u, -SWIGLU_LIMIT, SWIGLU_LIMIT)
        return (g * jax.nn.sigmoid(SWIGLU_ALPHA * g) * (u + 1.0)).astype(BF16)

    @pl.when(jnp.logical_and(s < n1, rows > 0))
    def _():
        prefetch_slice()
        lo, hi = unpack(0)
        g = bg_ref[...]
        u = bu_ref[...]
        kc = min(dh, 512)
        for k0 in range(0, 2 * dh, kc):
            xk = lo[:, k0:k0 + kc] if k0 < dh else hi[:, k0 - dh:k0 - dh + kc]
            wgc = wg_ref[k0:k0 + kc, :].astype(BF16)
            wuc = wu_ref[k0:k0 + kc, :].astype(BF16)
            wgu[0, k0:k0 + kc, :] = wgc
            wgu[1, k0:k0 + kc, :] = wuc
            g = g + _dot(xk, wgc)
            u = u + _dot(xk, wuc)
        act[s, 0:sub, :] = swiglu(g, u)

        def sub_block(i, c):
            lo, hi = unpack(i)
            g = _dot(lo, wgu[0, 0:dh, :]) + _dot(hi, wgu[0, dh:2 * dh, :]) + bg_ref[...]
            u = _dot(lo, wgu[1, 0:dh, :]) + _dot(hi, wgu[1, dh:2 * dh, :]) + bu_ref[...]
            act[s, pl.ds(pl.multiple_of(i * sub, sub), sub), :] = swiglu(g, u)
            return c

        lax.fori_loop(1, n_sub, sub_block, 0)

    @pl.when(jnp.logical_and(s >= n1, rows > 0))
    def _():
        col = pl.multiple_of((s - n1) * dt, dt)

        def out_copy(i, slot):
            row0 = pl.multiple_of(start + i * sub, sub)
            return pltpu.make_async_copy(ostage.at[slot], ys_hbm.at[pl.ds(row0, sub), pl.ds(col, dt)],
                                         osem.at[slot])

        prefetch_slice()
        y = bd_ref[...]
        for f in range(n1):
            wdc = wd_ref[f * tf:(f + 1) * tf, :].astype(BF16)
            wdb[f * tf:(f + 1) * tf, :] = wdc
            y = y + _dot(act[f, 0:sub, :], wdc)
        ostage[0] = y
        out_copy(0, 0).start()

        def sub_block(i, c):
            slot = i % 2

            @pl.when(i >= 2)
            def _():
                out_copy(i - 2, slot).wait()

            r0 = pl.multiple_of(i * sub, sub)
            y = bd_ref[...] + _dot(act[0, pl.ds(r0, sub), :], wdb[0:tf, :])
            for f in range(1, n1):
                y = y + _dot(act[f, pl.ds(r0, sub), :], wdb[f * tf:(f + 1) * tf, :])
            ostage[slot] = y
            out_copy(i, slot).start()
            return c

        lax.fori_loop(1, n_sub, sub_block, 0)

        @pl.when(n_sub >= 2)
        def _():
            out_copy(n_sub - 2, n_sub % 2).wait()

        out_copy(n_sub - 1, (n_sub - 1) % 2).wait()


def _experts(h_packed, tables, w_gu, b_gu, w_down, b_down, *, layer, sub, rmax):
    _, row_tok, sb_exp, sb_start, sb_rows, n_used, n_rows, n_sb_max = tables
    n_layers, n_exp, d, f2 = w_gu.shape
    f = f2 // 2
    tf = _tile(f, 256)
    dt = _tile(d, 512)
    n1, n2 = f // tf, d // dt
    assert rmax % (n1 + n2) == 0

    def f_idx(sb, s, rows_ref):
        return jnp.where(rows_ref[sb] > 0, jnp.minimum(s, n1 - 1), n1 - 1)

    def d_idx(sb, s, rows_ref):
        return jnp.where(rows_ref[sb] > 0, jnp.maximum(s - n1, 0), n2 - 1)

    in_specs = [
        pl.BlockSpec(memory_space=pl.ANY),
        pl.BlockSpec((None, None, d, tf),
                     lambda sb, s, e, st, rw, tk, us: (layer, e[sb], 0, f_idx(sb, s, rw))),
        pl.BlockSpec((None, None, d, tf),
                     lambda sb, s, e, st, rw, tk, us: (layer, e[sb], 0, n1 + f_idx(sb, s, rw))),
        pl.BlockSpec((None, None, 1, tf),
                     lambda sb, s, e, st, rw, tk, us: (layer, e[sb], 0, f_idx(sb, s, rw))),
        pl.BlockSpec((None, None, 1, tf),
                     lambda sb, s, e, st, rw, tk, us: (layer, e[sb], 0, n1 + f_idx(sb, s, rw))),
        pl.BlockSpec((None, None, f, dt),
                     lambda sb, s, e, st, rw, tk, us: (layer, e[sb], 0, d_idx(sb, s, rw))),
        pl.BlockSpec((None, None, 1, dt),
                     lambda sb, s, e, st, rw, tk, us: (layer, e[sb], 0, d_idx(sb, s, rw))),
    ]
    blk = (2 * rmax * (d // 2) * 4 + rmax * f * 2 + 4 * d * tf * 4 + 2 * f * dt * 4 + 2 * d * tf * 2
           + f * dt * 2 + 2 * sub * dt * 4 + 2 * sub * d * 2)
    return pl.pallas_call(
        functools.partial(_expert_kernel, n1=n1, sub=sub, tf=tf, dt=dt),
        out_shape=jax.ShapeDtypeStruct((n_rows, d), F32),
        grid_spec=pltpu.PrefetchScalarGridSpec(
            num_scalar_prefetch=5,
            grid=(n_sb_max, n1 + n2),
            in_specs=in_specs,
            out_specs=pl.BlockSpec(memory_space=pl.ANY),
            scratch_shapes=[
                pltpu.VMEM((2, rmax, d // 2), U32),
                pltpu.VMEM((n1, rmax, tf), BF16),
                pltpu.VMEM((2, d, tf), BF16),
                pltpu.VMEM((f, dt), BF16),
                pltpu.VMEM((2, sub, dt), F32),
                pltpu.SemaphoreType.DMA(()),
                pltpu.SemaphoreType.DMA((2,)),
            ]),
        compiler_params=_params(("arbitrary", "arbitrary"), blk),
        name="moe_experts",
    )(sb_exp, sb_start, sb_rows, row_tok, n_used, h_packed, w_gu, w_gu,
      b_gu.reshape(n_layers, n_exp, 1, f2), b_gu.reshape(n_layers, n_exp, 1, f2), w_down,
      b_down.reshape(n_layers, n_exp, 1, d))


def _combine_kernel(dest_ref, ys_hbm, x_ref, g_ref, gate_ref, o_ref, buf, sem, *, tiles_per_batch):
    tm = x_ref.shape[0]
    tok0 = (pl.program_id(0) * tiles_per_batch + pl.program_id(1)) * tm

    def row_copy(r, k):
        src = dest_ref[(tok0 + r) * TOP_K + k]
        return pltpu.make_async_copy(ys_hbm.at[pl.ds(src, 1)], buf.at[k, pl.ds(r, 1)], sem)

    def issue(r, c):
        for k in range(TOP_K):
            row_copy(r, k).start()
        return c

    def drain(r, c):
        for k in range(TOP_K):
            row_copy(r, k).wait()
        return c

    lax.fori_loop(0, tm, issue, 0)
    lax.fori_loop(0, tm, drain, 0)
    gate = gate_ref[...]
    y = gate[:, 0:1] * buf[0]
    for k in range(1, TOP_K):
        y = y + gate[:, k:k + 1] * buf[k]
    o_ref[...] = x_ref[...] + g_ref[...] * y


def _combine(ys, dest, x, g, gate):
    b, t, d = x.shape
    tm = _tile(t, 128)
    tiles_per_batch = t // tm
    row = lambda b_, i, dref: (b_, i, 0)
    if g.shape[1] == 1:
        g_spec = pl.BlockSpec((None, 1, d), lambda b_, i, dref: (b_, 0, 0))
    else:
        g_spec = pl.BlockSpec((None, tm, d), row)
    blk = TOP_K * tm * d * 4 + 6 * tm * d * 4
    return pl.pallas_call(
        functools.partial(_combine_kernel, tiles_per_batch=tiles_per_batch),
        out_shape=jax.ShapeDtypeStruct((b, t, d), F32),
        grid_spec=pltpu.PrefetchScalarGridSpec(
            num_scalar_prefetch=1,
            grid=(b, tiles_per_batch),
            in_specs=[pl.BlockSpec(memory_space=pl.ANY),
                      pl.BlockSpec((None, tm, d), row),
                      g_spec,
                      pl.BlockSpec((None, tm, LANES), row)],
            out_specs=pl.BlockSpec((None, tm, d), row),
            scratch_shapes=[pltpu.VMEM((TOP_K, tm, d), F32), pltpu.SemaphoreType.DMA(())]),
        compiler_params=_params(("arbitrary", "arbitrary"), blk),
        name="moe_combine",
    )(dest.reshape(-1), ys, x, g, gate)


def _moe(xp, xs, g, mods_p, mods_s, w_router, b_router, w_gu, b_gu, w_down, b_down, *, layer):
    (sc_p, sh_p, g_p), (sc_s, sh_s, g_s) = mods_p, mods_s
    d = xp.shape[-1]
    n_exp = w_router.shape[1]
    f = w_down.shape[2]
    hp, ti_p, gt_p = _norm_router(xp, g, sc_p, sh_p, w_router, b_router)
    hs, ti_s, gt_s = _norm_router(xs, g, sc_s, sh_s, w_router, b_router)
    n_p = xp.shape[0] * xp.shape[1]
    h_all = jnp.concatenate([hp.reshape(n_p, d // 2), hs.reshape(-1, d // 2)], axis=0)
    topi = jnp.concatenate([ti_p.reshape(n_p, LANES), ti_s.reshape(-1, LANES)], axis=0)
    sub = MOE_SUB_ROWS
    rmax = MOE_SUBS_PER_SUPERBLOCK * sub
    tables = _dispatch_tables(topi, n_exp, sub, rmax)
    ys = _experts(h_all, tables, w_gu, b_gu, w_down, b_down, layer=layer, sub=sub, rmax=rmax)
    dest = tables[0]
    xp = _combine(ys, dest[:n_p], xp, g_p, gt_p)
    xs = _combine(ys, dest[n_p:], xs, g_s, gt_s)
    return xp, xs


def _nsa_project(h, w_in, q_g, k_g):
    d = h.shape[1]
    hd = LANES
    kvh = N_KV_HEADS
    grp = N_HEADS // kvh
    q_dim = N_HEADS * hd
    kv_dim = 2 * kvh * hd
    q = _mm(h, w_in, col0=0, n=q_dim, epi="qnorm", extra=(q_g.reshape(1, hd),), out_dtype=BF16,
            head_dim=hd, scale=hd ** -0.5 * LOG2_E, name="nsa_q")
    ones = jnp.ones((kvh * hd,), F32)
    zeros = jnp.zeros((kvh * hd,), F32)
    flags = jnp.concatenate([zeros, zeros, ones, zeros, ones, zeros]).reshape(1, 3 * kv_dim)
    gains = jnp.concatenate([ones, ones, jnp.tile(k_g[1], kvh), ones, jnp.tile(k_g[2], kvh), ones])
    kv = _mm(h, w_in, col0=q_dim, n=3 * kv_dim, epi="kvnorm", extra=(flags, gains.reshape(1, 3 * kv_dim)),
             head_dim=hd, name="nsa_kv")
    w_gate = w_in[:, q_dim + 3 * kv_dim:].reshape(d, 3, kvh, grp)
    w_gate = jnp.transpose(w_gate, (0, 2, 1, 3)).reshape(d, kvh, 3 * grp)
    w_gate = jnp.pad(w_gate, ((0, 0), (0, 0), (0, LANES - 3 * grp))).reshape(d, kvh * LANES)
    gates = _mm(h, w_gate, col0=0, n=kvh * LANES, epi="sigmoid", name="nsa_gates")
    return q, kv, gates


def _cmp_part_kernel(pt_ref, *refs, n_pages, kvh, row_view):
    n_refs = n_pages if row_view else n_pages * 2 * kvh
    pages = refs[:n_refs]
    w_ref, pe_ref, p_out, pe_out, wbf = refs[n_refs:]
    rpt = 2 * kvh
    chunks = PAGE_SIZE // CMP_STRIDE

    def chunk_rows(j, h, s):
        if row_view:
            return [pg[pl.ds(s * rpt + j * kvh + h, chunks, stride=CMP_STRIDE * rpt), :] for pg in pages]
        return [pg[pl.ds(s, chunks, stride=CMP_STRIDE), :] for pg in pages[j * kvh + h::rpt]]

    @pl.when(pl.program_id(0) == 0)
    def _():
        for j in range(2):
            wbf[j] = w_ref[j].astype(BF16)
            pe_out[j] = _dot(pe_ref[j].astype(BF16), wbf[j])

    rows = n_pages * PAGE_SIZE // CMP_STRIDE
    for j in range(2):
        per_head = []
        for h in range(kvh):
            pieces = [jnp.concatenate(chunk_rows(j, h, s), axis=0) for s in range(CMP_STRIDE)]
            per_head.append(jnp.concatenate(pieces, axis=1))
        x = jnp.concatenate(per_head, axis=0).astype(BF16)
        y = _dot(x, wbf[j])
        for h in range(kvh):
            p_out[j, h] = y[h * rows:(h + 1) * rows, :]


def _cmp_out_kernel(pk_ref, pv_ref, b1_ref, pe_ref, w2_ref, b2_ref, kg_ref, kc_ref, vc_ref, shifted):
    n_ch = pk_ref.shape[0]
    hid_dim = pk_ref.shape[1] // 2
    for j, (p_ref, o_ref) in enumerate(((pk_ref, kc_ref), (pv_ref, vc_ref))):
        shifted[0:n_ch, :] = p_ref[:, hid_dim:2 * hid_dim]
        shifted[n_ch:n_ch + SUBLANES, :] = jnp.zeros((SUBLANES, hid_dim), F32)
        hid = (b1_ref[j:j + 1, :] + pe_ref[j, 0:1, 0:hid_dim] + pe_ref[j, 1:2, hid_dim:2 * hid_dim]
               + p_ref[:, 0:hid_dim] + shifted[1:1 + n_ch, :])
        a = hid * jax.nn.sigmoid(hid)
        out = _dot(a.astype(BF16), w2_ref[j].astype(BF16)) + b2_ref[j:j + 1, :]
        if j == 0:
            out = _rms(out) * kg_ref[...]
        o_ref[...] = out.astype(o_ref.dtype)


def _compress(src_pages, page_table, batch, w1, b1, w2, b2, pe, kg, *, layer=None):
    kvh = N_KV_HEADS
    hd = LANES
    n_log = page_table.shape[0]
    row_view = layer is not None
    n_pages = _tile(n_log, 16 if row_view else 8)
    chunks_per_page = PAGE_SIZE // CMP_STRIDE
    n_chunks = n_log * chunks_per_page
    hid2 = w1.shape[1] * w1.shape[-1]
    kdim = CMP_STRIDE * hd
    w1cat = jnp.transpose(w1, (0, 2, 3, 1, 4)).reshape(2, kdim, hid2)
    pe_rows = jnp.pad(pe.reshape(2, -1, kdim), ((0, 0), (0, SUBLANES - pe.shape[1]), (0, 0)))

    if row_view:
        rpt = 2 * kvh
        src_pages = src_pages.reshape(src_pages.shape[0], src_pages.shape[1], PAGE_SIZE * rpt, hd)
        page_specs = [pl.BlockSpec((None, None, PAGE_SIZE * rpt, hd),
                                   lambda i, pt, r=r: (layer, pt[i * n_pages + r], 0, 0)) for r in range(n_pages)]
    else:
        page_specs = [pl.BlockSpec((None, PAGE_SIZE, hd),
                                   lambda i, pt, r=r, cb=cb: (pt[i * n_pages + r], 0, cb))
                      for r in range(n_pages) for cb in range(2 * kvh)]
    rows = n_pages * chunks_per_page
    blk = (2 * n_pages * PAGE_SIZE * 2 * kvh * hd * 4 + 3 * 2 * kdim * hid2 * 4
           + 4 * kvh * rows * (kdim + hid2) * 4)
    parts, pe_out = pl.pallas_call(
        functools.partial(_cmp_part_kernel, n_pages=n_pages, kvh=kvh, row_view=row_view),
        out_shape=(jax.ShapeDtypeStruct((2, kvh, n_chunks, hid2), F32),
                   jax.ShapeDtypeStruct((2, SUBLANES, hid2), F32)),
        grid_spec=pltpu.PrefetchScalarGridSpec(
            num_scalar_prefetch=1,
            grid=(n_log // n_pages,),
            in_specs=page_specs + [
                pl.BlockSpec((2, kdim, hid2), lambda i, pt: (0, 0, 0)),
                pl.BlockSpec((2, SUBLANES, kdim), lambda i, pt: (0, 0, 0))],
            out_specs=(pl.BlockSpec((2, kvh, rows, hid2), lambda i, pt: (0, 0, i, 0)),
                       pl.BlockSpec((2, SUBLANES, hid2), lambda i, pt: (0, 0, 0))),
            scratch_shapes=[pltpu.VMEM((2, kdim, hid2), BF16)]),
        compiler_params=_params(("arbitrary",), blk),
        name="cmp_part",
    )(page_table, *([src_pages] * len(page_specs)), w1cat, pe_rows)

    n_ch = n_chunks // batch
    hid = hid2 // 2
    part_spec = lambda j: pl.BlockSpec((None, None, n_ch, hid2), lambda b, h: (j, h, b, 0))
    full = lambda *shape: pl.BlockSpec(shape, lambda b, h: (0,) * len(shape))
    out_spec = pl.BlockSpec((None, None, n_ch, hd), lambda b, h: (b, h, 0, 0))
    return pl.pallas_call(
        _cmp_out_kernel,
        out_shape=(jax.ShapeDtypeStruct((batch, kvh, n_ch, hd), BF16),) * 2,
        grid=(batch, kvh),
        in_specs=[part_spec(0), part_spec(1), full(2, hid), full(2, SUBLANES, hid2), full(2, hid, hd),
                  full(2, hd), full(1, hd)],
        out_specs=(out_spec, out_spec),
        scratch_shapes=[pltpu.VMEM((n_ch + SUBLANES, hid), F32)],
        compiler_params=_params(("arbitrary", "arbitrary"), 12 * n_ch * hid2 * 4),
        name="cmp_out",
    )(parts, parts, b1, pe_out, w2, b2, kg.reshape(1, hd))


def _overlap(n_ch, n_blk):
    c0 = _iota((n_ch, n_blk), 0) * CMP_STRIDE
    b0 = _iota((n_ch, n_blk), 1) * SEL_BLOCK
    return jnp.where(c0 < b0 + SEL_BLOCK, jnp.where(c0 + CMP_BLOCK > b0, 1.0, 0.0), 0.0).astype(BF16)


def _cmp_scores(q, kc, slope, tpos, n_cmp):
    n_ch = kc.shape[0]
    c_idx = _iota((1, n_ch), 1)
    d_c = tpos - (c_idx * CMP_STRIDE + (CMP_BLOCK - 1))
    valid = jnp.where(c_idx < n_cmp, d_c, -1) >= 0
    s = _dot_nt(q, kc) - slope * d_c.astype(F32)
    s = jnp.where(valid, s, -jnp.inf)
    m = jnp.max(s, axis=-1, keepdims=True)
    m = jnp.where(m == -jnp.inf, 0.0, m)
    e = jnp.exp2(s - m)
    return e / jnp.maximum(jnp.sum(e, axis=-1, keepdims=True), 1e-30)


def _with_position_lanes(k, pos0):
    rows, hd = k.shape
    pos = pos0 + _iota((rows, hd), 0)
    lane = _iota((rows, hd), 1)
    ext = jnp.where(lane < 3, pos // LANES * LANES, jnp.where(lane < 6, pos % LANES, 0))
    return jnp.concatenate([k, ext.astype(F32).astype(BF16)], axis=1)


def _with_slope_lanes(q, parts):
    lane = _iota(q.shape, 1)
    ext = jnp.zeros(q.shape, F32)
    for c, part in enumerate(parts):
        ext = jnp.where(lane == c, part, jnp.where(lane == c + 3, part, ext))
    return jnp.concatenate([q, ext.astype(BF16)], axis=1)


def _with_ones(v):
    return jnp.concatenate([v, jnp.ones(v.shape, BF16)], axis=1)


def _force_and_mask(imp, tpos):
    blk = _iota((1, imp.shape[1]), 1)
    cur = tpos // SEL_BLOCK
    forced = jnp.logical_or(blk == 0, jnp.logical_or(blk == cur, blk == cur - 1))
    imp = jnp.where(forced, FORCE_SCORE, imp)
    return jnp.where(blk > cur, -jnp.inf, imp)


def _attn_prompt_kernel(slope_ref, q_ref, kc_ref, vc_ref, ks_ref, vs_ref, kw_ref, vw_ref, gt_ref, o_ref,
                        m_s, acc_s, oc_s, qa_s, *, grp, seq, tq, tk, n_cmp, n_blk, n_top, wl):
    hk = pl.program_id(1)
    t0 = pl.program_id(2) * tq
    hd = LANES
    n_heads = slope_ref.shape[0] // 4
    row_t = t0 + _iota((tq, 1), 0)
    heads = [(g, slope_ref[hk * grp + g], slice(g * hd, (g + 1) * hd)) for g in range(grp)]

    kc = kc_ref[...]
    vc = vc_ref[...]
    n_ch = kc.shape[0]
    psum = jnp.zeros((tq, n_ch), F32)
    for g, slope, cols in heads:
        q = q_ref[:, cols]
        p = _cmp_scores(q, kc, slope, row_t, n_cmp)
        psum = psum + p
        oc_s[g] = _dot(p.astype(BF16), vc)
        qa_s[g] = _with_slope_lanes(q, [slope_ref[(1 + c) * n_heads + hk * grp + g] for c in range(3)])
    c0 = _iota((n_blk, n_ch), 1) * CMP_STRIDE
    b0 = _iota((n_blk, n_ch), 0) * SEL_BLOCK
    ov_t = jnp.where(c0 < b0 + SEL_BLOCK, jnp.where(c0 + CMP_BLOCK > b0, 1.0, 0.0), 0.0).astype(BF16)
    p1, p2, p3 = _split3(psum)
    imp = _dot_nt(ov_t, p1) + _dot_nt(ov_t, p2) + _dot_nt(ov_t, p3)
    blk = _iota((n_blk, 1), 0)
    cur = (t0 + _iota((1, tq), 1)) // SEL_BLOCK
    forced = jnp.logical_or(blk == 0, jnp.logical_or(blk == cur, blk == cur - 1))
    imp = jnp.where(blk > cur, -jnp.inf, jnp.where(forced, FORCE_SCORE, imp))
    rank = jnp.zeros((n_blk, tq), F32)
    for j in range(n_blk):
        rj = imp[j:j + 1, :]
        tie = jnp.where(blk > j, 1.0, 0.0)
        rank = rank + jnp.where(rj > imp, 1.0, jnp.where(rj == imp, tie, 0.0))
    sel_t = jnp.where(rank < n_top, 1.0, 0.0)
    nb_pad = -(-n_blk // LANES) * LANES
    sel_t = jnp.concatenate([sel_t, jnp.zeros((nb_pad - n_blk, tq), F32)], axis=0)
    sel = sel_t.T.astype(BF16)

    m_s[...] = jnp.full(m_s.shape, MASKED, F32)
    acc_s[...] = jnp.zeros(acc_s.shape, F32)

    def kv_tile(kt, carry):
        k0 = pl.multiple_of(kt * tk, tk)
        ka = _with_position_lanes(ks_ref[pl.ds(k0, tk), :].astype(BF16), k0)
        va = _with_ones(vs_ref[pl.ds(k0, tk), :].astype(BF16))
        expand = jnp.where(_iota((nb_pad, tk), 0) == (k0 + _iota((nb_pad, tk), 1)) // SEL_BLOCK, 1.0, 0.0)
        picked = _dot(sel, expand.astype(BF16))
        ok = jnp.where(row_t - (k0 + _iota((1, tk), 1)) >= 0, picked, 0.0) > 0.5
        for g, _, _ in heads:
            s = jnp.where(ok, _dot_nt(qa_s[g], ka), MASKED)
            m_old = m_s[g]
            m_new = jnp.maximum(m_old, jnp.max(s, axis=-1, keepdims=True))
            p = jnp.exp2(s - m_new)
            acc_s[g] = jnp.exp2(m_old - m_new) * acc_s[g] + _dot(p.astype(BF16), va)
            m_s[g] = m_new
        return carry

    lax.fori_loop(0, (t0 + tq + tk - 1) // tk, kv_tile, 0)

    ws = pl.multiple_of(jnp.clip(t0 + tq - wl, 0, seq - wl), SUBLANES)
    kwa = _with_position_lanes(kw_ref[pl.ds(ws, wl), :].astype(BF16), ws)
    vwa = _with_ones(vw_ref[pl.ds(ws, wl), :].astype(BF16))
    dw = row_t - (ws + _iota((1, wl), 1))
    okw = jnp.where(dw >= 0, jnp.where(dw < WINDOW, 1.0, 0.0), 0.0) > 0.5
    gates = gt_ref[...]
    for g, _, cols in heads:
        s = jnp.where(okw, _dot_nt(qa_s[g], kwa), MASKED)
        e = jnp.exp2(s - jnp.max(s, axis=-1, keepdims=True))
        win = _dot(e.astype(BF16), vwa)
        o_w = win[:, :hd] / win[:, hd:hd + 1]
        acc = acc_s[g]
        o_s = acc[:, :hd] / acc[:, hd:hd + 1]
        out = (gates[:, g:g + 1] * oc_s[g] + gates[:, grp + g:grp + g + 1] * o_s
               + gates[:, 2 * grp + g:2 * grp + g + 1] * o_w)
        o_ref[:, cols] = out.astype(o_ref.dtype)


def _attend_prompt(q, kc, vc, kv, gates, slopes, *, batch, seq):
    kvh = N_KV_HEADS
    grp = N_HEADS // kvh
    hd = LANES
    tq = _tile(seq, 256)
    tk = _tile(seq, 512)
    nq = seq // tq
    n_ch = kc.shape[2]
    n_blk = -(-seq // SEL_BLOCK)
    wl = min(tq + WINDOW, seq)
    kv3 = kv.reshape(batch, seq, kv.shape[1])
    row = lambda b, h, i: (b * nq + i, h)
    kv_spec = lambda cb: pl.BlockSpec((None, seq, hd), lambda b, h, i: (b, 0, cb * kvh + h))
    cmp_spec = pl.BlockSpec((None, None, n_ch, hd), lambda b, h, i: (b, h, 0, 0))
    blk = (8 * seq * hd * 4 + 4 * tq * grp * hd * 2 + 3 * grp * tq * hd * 4 + 10 * tq * max(tk, wl) * 4
           + 4 * n_ch * hd * 2)
    return pl.pallas_call(
        functools.partial(_attn_prompt_kernel, grp=grp, seq=seq, tq=tq, tk=tk, n_cmp=n_ch - 1, n_blk=n_blk,
                          n_top=min(N_SEL, n_blk), wl=wl),
        out_shape=jax.ShapeDtypeStruct((batch * seq, N_HEADS * hd), BF16),
        grid=(batch, kvh, nq),
        in_specs=[pl.BlockSpec(memory_space=pltpu.SMEM),
                  pl.BlockSpec((tq, grp * hd), row), cmp_spec, cmp_spec,
                  kv_spec(2), kv_spec(3), kv_spec(4), kv_spec(5),
                  pl.BlockSpec((tq, LANES), row)],
        out_specs=pl.BlockSpec((tq, grp * hd), row),
        scratch_shapes=[pltpu.VMEM((grp, tq, 1), F32), pltpu.VMEM((grp, tq, 2 * hd), F32),
                        pltpu.VMEM((grp, tq, hd), F32), pltpu.VMEM((grp, tq, 2 * hd), BF16)],
        compiler_params=_params(("arbitrary", "arbitrary", "arbitrary"), blk),
        name="nsa_attend_prompt",
    )(slopes, q, kc, vc, kv3, kv3, kv3, kv3, gates)


def _attn_sample_kernel(pt_ref, slope_ref, q_ref, kc_ref, vc_ref, *rest, n_pages, kvh, grp, dec_seq, past, n_cmp,
                        n_blk, n_top, nb_pad):
    pages = rest[:n_pages]
    new_ref, win_ref, gt_ref, o_ref, m_s, l_s, acc_s, oc_s, sel_s = rest[n_pages:]
    c = pl.program_id(1)
    rows = q_ref.shape[1]
    hd = LANES
    rpt = 2 * kvh
    row = _iota((rows, 1), 0)
    t_row = row // grp
    g_row = row % grp
    tpos = past + t_row
    tk = n_pages * PAGE_SIZE
    k0 = c * tk
    expand = jnp.where(_iota((nb_pad, tk), 0) == (k0 + _iota((nb_pad, tk), 1)) // SEL_BLOCK, 1.0, 0.0).astype(BF16)
    d_past = tpos - (k0 + _iota((1, tk), 1))
    d_past_f = d_past.astype(F32)

    def head_rows(ref, j, hk, n):
        return ref[pl.ds(j * kvh + hk, n, stride=rpt), :].astype(BF16)

    for hk in range(kvh):
        q = q_ref[hk]
        slope = jnp.zeros((rows, 1), F32)
        for g in range(grp):
            slope = jnp.where(g_row == g, slope_ref[hk * grp + g], slope)

        def online_update(s, v, hk=hk):
            m_old = m_s[hk]
            m_new = jnp.maximum(m_old, jnp.max(s, axis=-1, keepdims=True))
            alpha = jnp.exp2(m_old - m_new)
            p = jnp.exp2(s - m_new)
            l_s[hk] = alpha * l_s[hk] + jnp.sum(p, axis=-1, keepdims=True)
            acc_s[hk] = alpha * acc_s[hk] + _dot(p.astype(BF16), v)
            m_s[hk] = m_new

        @pl.when(c == 0)
        def _(hk=hk, q=q, slope=slope):
            kc = kc_ref[hk]
            n_ch = kc.shape[0]
            p = _cmp_scores(q, kc, slope, tpos, n_cmp)
            oc_s[hk] = _dot(p.astype(BF16), vc_ref[hk])
            ov = _overlap(n_ch, nb_pad)
            p1, p2, p3 = _split3(p)
            per_head = _dot(p1, ov) + _dot(p2, ov) + _dot(p3, ov)
            same_tok = jnp.where(_iota((rows, rows), 0) // grp == _iota((rows, rows), 1) // grp, 1.0, 0.0)
            same_tok = same_tok.astype(BF16)
            a1, a2, a3 = _split3(per_head)
            imp = _force_and_mask(_dot(same_tok, a1) + _dot(same_tok, a2) + _dot(same_tok, a3), tpos)
            blk_f = _iota((1, nb_pad), 1).astype(F32)
            taken = jnp.where(blk_f >= n_blk, 1.0, 0.0) + jnp.zeros((rows, nb_pad), F32)
            sel = jnp.zeros((rows, nb_pad), F32)
            for _ in range(n_top):
                avail = jnp.where(taken > 0.5, -jnp.inf, imp)
                best = jnp.max(avail, axis=-1, keepdims=True)
                cand = jnp.where(taken > 0.5, 0.0, jnp.where(avail == best, 1.0, 0.0))
                idx = jnp.min(jnp.where(cand > 0.5, blk_f, float(nb_pad)), axis=-1, keepdims=True)
                pick = blk_f == idx
                taken = jnp.where(pick, 1.0, taken)
                sel = jnp.where(pick, 1.0, sel)
            sel_s[hk] = sel.astype(BF16)
            m_s[hk] = jnp.full((rows, 1), MASKED, F32)
            l_s[hk] = jnp.zeros((rows, 1), F32)
            acc_s[hk] = jnp.zeros((rows, hd), F32)

        kk = jnp.concatenate([head_rows(pg, 0, hk, PAGE_SIZE) for pg in pages], axis=0)
        vv = jnp.concatenate([head_rows(pg, 1, hk, PAGE_SIZE) for pg in pages], axis=0)
        picked = _dot(sel_s[hk], expand)
        ok = jnp.where(d_past >= 0, picked, 0.0) > 0.5
        online_update(jnp.where(ok, _dot_nt(q, kk) - slope * d_past_f, MASKED), vv)

        @pl.when(c == pl.num_programs(1) - 1)
        def _(hk=hk, q=q, slope=slope, online_update=online_update):
            new_cols = lambda cb: slice((cb * kvh + hk) * hd, (cb * kvh + hk + 1) * hd)
            t_new = _iota((1, new_ref.shape[0]), 1)
            dn = t_row - t_new
            causal_new = jnp.where(dn >= 0, jnp.where(t_new < dec_seq, 1.0, 0.0), 0.0)
            dnf = dn.astype(F32)
            online_update(jnp.where(causal_new > 0.5, _dot_nt(q, new_ref[:, new_cols(2)].astype(BF16)) - slope * dnf,
                                    MASKED), new_ref[:, new_cols(3)].astype(BF16))
            o_s = acc_s[hk] / l_s[hk]
            n_buf = win_ref.shape[0] // rpt
            dc = tpos - (past - n_buf + _iota((1, n_buf), 1))
            ok_c = jnp.where(dc >= 0, jnp.where(dc < WINDOW, 1.0, 0.0), 0.0) > 0.5
            s_c = jnp.where(ok_c, _dot_nt(q, head_rows(win_ref, 0, hk, n_buf)) - slope * dc.astype(F32), MASKED)
            ok_n = jnp.where(dn < WINDOW, causal_new, 0.0) > 0.5
            s_n = jnp.where(ok_n, _dot_nt(q, new_ref[:, new_cols(4)].astype(BF16)) - slope * dnf, MASKED)
            m = jnp.maximum(jnp.max(s_c, axis=-1, keepdims=True), jnp.max(s_n, axis=-1, keepdims=True))
            e_c = jnp.exp2(s_c - m)
            e_n = jnp.exp2(s_n - m)
            den = jnp.sum(e_c, axis=-1, keepdims=True) + jnp.sum(e_n, axis=-1, keepdims=True)
            o_w = (_dot(e_c.astype(BF16), head_rows(win_ref, 1, hk, n_buf))
                   + _dot(e_n.astype(BF16), new_ref[:, new_cols(5)].astype(BF16))) / den
            gates = gt_ref[hk]
            o_ref[hk] = (gates[:, 0:1] * oc_s[hk] + gates[:, 1:2] * o_s + gates[:, 2:3] * o_w).astype(o_ref.dtype)


def _attend_sample(q_rows, kc, vc, cache_sel, page_table, kv_new, cache_win, gate_rows, slopes, *, layer, past,
                   dec_seq):
    batch, kvh, rows, hd = q_rows.shape
    grp = N_HEADS // kvh
    assert past % SEL_BLOCK == 0 and dec_seq <= SEL_BLOCK and past % PAGE_SIZE == 0
    pages_per_batch = past // PAGE_SIZE
    n_pages = _tile(pages_per_batch, 8)
    n_ch = kc.shape[2]
    n_blk = -(-(past + dec_seq) // SEL_BLOCK)
    nb_pad = -(-n_blk // LANES) * LANES
    n_layers, n_phys = cache_sel.shape[:2]
    n_buf = cache_win.shape[2]
    rpt = 2 * kvh
    sel_rows = cache_sel.reshape(n_layers, n_phys, PAGE_SIZE * rpt, hd)
    win_rows = cache_win.reshape(n_layers, batch, n_buf * rpt, hd)

    def page_spec(r):
        return pl.BlockSpec((None, None, PAGE_SIZE * rpt, hd),
                            lambda b, c, pt: (layer, pt[b * pages_per_batch + c * n_pages + r], 0, 0))

    per_batch = lambda *shape: pl.BlockSpec((None,) + shape, lambda b, c, pt: (b,) + (0,) * len(shape))
    row_spec = per_batch(kvh, rows, hd)
    tk = n_pages * PAGE_SIZE
    blk = (2 * n_pages * PAGE_SIZE * rpt * hd * 4 + 4 * kvh * n_ch * hd * 2 + 2 * n_buf * rpt * hd * 4
           + 16 * rows * max(tk, n_ch, nb_pad) * 4 + 3 * nb_pad * max(tk, n_ch) * 4)
    return pl.pallas_call(
        functools.partial(_attn_sample_kernel, n_pages=n_pages, kvh=kvh, grp=grp, dec_seq=dec_seq, past=past,
                          n_cmp=n_ch - 1, n_blk=n_blk, n_top=min(N_SEL, n_blk), nb_pad=nb_pad),
        out_shape=jax.ShapeDtypeStruct((batch, kvh, rows, hd), BF16),
        grid_spec=pltpu.PrefetchScalarGridSpec(
            num_scalar_prefetch=1,
            grid=(batch, pages_per_batch // n_pages),
            in_specs=[pl.BlockSpec(memory_space=pltpu.SMEM), row_spec, per_batch(kvh, n_ch, hd),
                      per_batch(kvh, n_ch, hd)]
            + [page_spec(r) for r in range(n_pages)]
            + [per_batch(kv_new.shape[1], kv_new.shape[2]),
               pl.BlockSpec((None, None, n_buf * rpt, hd), lambda b, c, pt: (layer, b, 0, 0)), row_spec],
            out_specs=row_spec,
            scratch_shapes=[pltpu.VMEM((kvh, rows, 1), F32), pltpu.VMEM((kvh, rows, 1), F32),
                            pltpu.VMEM((kvh, rows, hd), F32), pltpu.VMEM((kvh, rows, hd), F32),
                            pltpu.VMEM((kvh, rows, nb_pad), BF16)]),
        compiler_params=_params(("arbitrary", "arbitrary"), blk),
        name="nsa_attend_sample",
    )(page_table, slopes, q_rows, kc, vc, *([sel_rows] * n_pages), kv_new, win_rows, gate_rows)


def _nsa_layer(hp, hs, xp, xs, g1p, g1s, caches, page_table, weights, slopes, *, layer, batch, seq, dec_batch,
               dec_seq):
    (w_in, w_out, q_g, k_g, pe, w1, b1, w2, b2) = weights
    cache_cmp, cache_sel, cache_win = caches
    d = xp.shape[-1]
    kvh = N_KV_HEADS
    grp = N_HEADS // kvh
    hd = LANES
    cols = 2 * kvh * hd
    n_s = dec_batch * dec_seq
    past = page_table.shape[1] * PAGE_SIZE
    assert seq % PAGE_SIZE == 0 and past % CMP_STRIDE == 0 and dec_seq < CMP_STRIDE
    cmp_w = (w1, b1, w2, b2, pe, k_g[0])

    q, kv, gates = _nsa_project(hp, w_in, q_g, k_g)
    n_pages_p = batch * seq // PAGE_SIZE
    kc, vc = _compress(kv.reshape(n_pages_p, PAGE_SIZE, 3 * cols), jnp.arange(n_pages_p, dtype=I32), batch, *cmp_w)
    o = _attend_prompt(q, kc, vc, kv, gates, slopes, batch=batch, seq=seq)
    xp = _mm(o, w_out, col0=0, n=d, epi="res", extra=(xp.reshape(batch * seq, d), g1p), rows_per_batch=seq,
             name="nsa_out").reshape(batch, seq, d)
    kv_p = kv.reshape(batch, seq, 3, 2, kvh, hd)
    new_p = (kv_p[:, :, 0], kv_p[:, :, 1], kv_p[:, seq - min(WINDOW, seq):, 2])

    qs, kvs, gs = _nsa_project(hs, w_in, q_g, k_g)
    pt = page_table.reshape(-1)
    kcs, vcs = _compress(cache_cmp, pt, dec_batch, *cmp_w, layer=layer)
    q_rows = jnp.transpose(qs.reshape(dec_batch, dec_seq, kvh, grp, hd), (0, 2, 1, 3, 4))
    q_rows = q_rows.reshape(dec_batch, kvh, dec_seq * grp, hd)
    gate_rows = gs.reshape(dec_batch, dec_seq, kvh, LANES)[..., :3 * grp].reshape(dec_batch, dec_seq, kvh, 3, grp)
    gate_rows = jnp.transpose(gate_rows, (0, 2, 1, 4, 3)).reshape(dec_batch, kvh, dec_seq * grp, 3)
    gate_rows = jnp.pad(gate_rows, ((0, 0), (0, 0), (0, 0), (0, LANES - 3)))
    kv_new = jnp.pad(kvs.reshape(dec_batch, dec_seq, 3 * cols), ((0, 0), (0, SUBLANES - dec_seq), (0, 0)))
    n_buf = cache_win.shape[2]
    o_rows = _attend_sample(q_rows, kcs, vcs, cache_sel, pt, kv_new, cache_win, gate_rows, slopes, layer=layer,
                            past=past, dec_seq=dec_seq)
    o_s = jnp.transpose(o_rows.reshape(dec_batch, kvh, dec_seq, grp, hd), (0, 2, 1, 3, 4)).reshape(n_s, N_HEADS * hd)
    xs = _mm(o_s, w_out, col0=0, n=d, epi="res", extra=(xs.reshape(n_s, d), g1s[0]),
             name="nsa_out").reshape(1, n_s, d)
    kv_s = kvs.reshape(dec_batch, dec_seq, 3, 2, kvh, hd)
    win_s = jnp.concatenate([cache_win[layer], kv_s[:, :, 2]], axis=1)[:, -n_buf:]
    new_s = (kv_s[:, :, 0], kv_s[:, :, 1], win_s)
    return xp, xs, new_p, new_s


def kernel(x_prompt, x_sample, c_prompt, c_sample, state_conv, cache_cmp_kv, cache_sel_kv, cache_win_kv,
           page_table, w_mod, b_mod, norm_g, conv_w_in, conv_w, conv_w_out, nsa_w_in, nsa_w_out, q_norm_g,
           k_norm_g, cmp_pe, cmp_w1, cmp_b1, cmp_w2, cmp_b2, router_w, router_b, moe_w_gu, moe_b_gu,
           moe_w_down, moe_b_down):
    batch, seq, d = x_prompt.shape
    dec_batch, dec_seq, _ = x_sample.shape
    depth = w_mod.shape[0]
    n_s = dec_batch * dec_seq
    assert dec_seq >= conv_w.shape[1] - 1

    c_all = jnp.concatenate([c_prompt, c_sample], axis=0)
    c_all = jnp.pad(c_all, ((0, -c_all.shape[0] % SUBLANES), (0, 0)))
    mod = _adaln(c_all, w_mod, b_mod)
    slope2 = jnp.exp2(-8.0 * jnp.arange(1, N_HEADS + 1, dtype=F32) / N_HEADS) * LOG2_E
    slopes = jnp.concatenate([slope2] + [p.astype(F32) for p in _split3(slope2)])

    xp = x_prompt
    xs = x_sample.reshape(1, n_s, d)
    conv_p, conv_s, cmp_p, cmp_s, sel_p, sel_s, win_p, win_s = [], [], [], [], [], [], [], []
    for i in range(depth):
        j = i // 2
        sh1p, sc1p, g1p, sh2p, sc2p, g2p = [m[:, None, :] for m in jnp.split(mod[i, :batch], 6, axis=-1)]
        sh1s, sc1s, g1s, sh2s, sc2s, g2s = [jnp.repeat(m, dec_seq, axis=0)[None]
                                            for m in jnp.split(mod[i, batch:batch + dec_batch], 6, axis=-1)]
        hp = _norm_mod(xp, norm_g[i, 0], sc1p, sh1p).reshape(batch * seq, d)
        hs = _norm_mod(xs, norm_g[i, 0], sc1s, sh1s).reshape(n_s, d)
        if i % 2 == 0:
            a_p, state_p = _conv_in(hp, conv_w_in[j], conv_w[j], batch=batch, seq=seq)
            a_s, v_s = _conv_in(hs, conv_w_in[j], conv_w[j], batch=dec_batch, seq=dec_seq,
                                prev=_conv_prev_rows(state_conv[j], dec_seq))
            xp = _mm(a_p, conv_w_out[j], col0=0, n=d, epi="res", extra=(xp.reshape(batch * seq, d), g1p),
                     rows_per_batch=seq, name="conv_out").reshape(batch, seq, d)
            xs = _mm(a_s, conv_w_out[j], col0=0, n=d, epi="res", extra=(xs.reshape(n_s, d), g1s[0]),
                     name="conv_out").reshape(1, n_s, d)
            conv_p.append(state_p)
            conv_s.append(v_s.reshape(dec_batch, dec_seq, d)[:, dec_seq - 2:])
        else:
            weights = (nsa_w_in[j], nsa_w_out[j], q_norm_g[j], k_norm_g[j], cmp_pe[j], cmp_w1[j], cmp_b1[j],
                       cmp_w2[j], cmp_b2[j])
            xp, xs, new_p, new_s = _nsa_layer(
                hp, hs, xp, xs, g1p, g1s, (cache_cmp_kv, cache_sel_kv, cache_win_kv), page_table,
                weights, slopes, layer=j, batch=batch, seq=seq, dec_batch=dec_batch, dec_seq=dec_seq)
            cmp_p.append(new_p[0])
            sel_p.append(new_p[1])
            win_p.append(new_p[2])
            cmp_s.append(new_s[0])
            sel_s.append(new_s[1])
            win_s.append(new_s[2])
        xp, xs = _moe(xp, xs, norm_g[i, 1], (sc2p, sh2p, g2p), (sc2s, sh2s, g2s), router_w[i], router_b[i],
                      moe_w_gu, moe_b_gu, moe_w_down, moe_b_down, layer=i)
    return (xp, xs.reshape(dec_batch, dec_seq, d), jnp.stack(conv_p), jnp.stack(conv_s), jnp.stack(cmp_p),
            jnp.stack(cmp_s), jnp.stack(sel_p), jnp.stack(sel_s), jnp.stack(win_p), jnp.stack(win_s))
```

```python
import functools

import jax
import jax.numpy as jnp
from jax import lax
from jax.experimental import pallas as pl
from jax.experimental.pallas import tpu as pltpu

F32 = jnp.float32
BF16 = jnp.bfloat16
I32 = jnp.int32
U32 = jnp.uint32

N_HEADS = 32
N_KV_HEADS = 4
CMP_BLOCK = 32
CMP_STRIDE = 16
SEL_BLOCK = 64
N_SEL = 16
WINDOW = 512
PAGE_SIZE = 128
TOP_K = 4
SWIGLU_LIMIT = 7.0
SWIGLU_ALPHA = 1.702
EPS = 1e-6
FORCE_SCORE = 1e4
MASKED = -1e30
LOG2_E = 1.4426950408889634
MOE_SUB_ROWS = 272
MOE_SUBS_PER_SUPERBLOCK = 5

LANES = 128
SUBLANES = 8
VMEM_PHYSICAL_BYTES = 64 * 1024 * 1024
VMEM_CAP_BYTES = VMEM_PHYSICAL_BYTES - 6 * 1024 * 1024


def _vmem_limit(block_bytes):
    return int(min(VMEM_CAP_BYTES, block_bytes * 5 // 4 + (4 << 20)))


def _params(sem, block_bytes):
    return pltpu.CompilerParams(dimension_semantics=sem, vmem_limit_bytes=_vmem_limit(block_bytes))


def _tile(n, pref):
    if n <= pref:
        return n
    t = pref
    while n % t:
        t //= 2
    return t


def _dot(a, b):
    return jnp.dot(a, b, preferred_element_type=F32)


def _dot_nt(a, b):
    return lax.dot_general(a, b, (((1,), (1,)), ((), ())), preferred_element_type=F32)


def _split3(x):
    hi = x.astype(BF16)
    r = x - hi.astype(F32)
    mid = r.astype(BF16)
    lo = (r - mid.astype(F32)).astype(BF16)
    return hi, mid, lo


def _iota(shape, dim):
    return lax.broadcasted_iota(I32, shape, dim)


def _rms(a):
    return a * lax.rsqrt(jnp.mean(a * a, axis=-1, keepdims=True) + EPS)


def _adaln_kernel(c_ref, w_ref, b_ref, o_ref, *, kc):
    c = c_ref[...]
    a = (c * jax.nn.sigmoid(c)).astype(BF16)
    acc = jnp.zeros(o_ref.shape, F32)
    for k0 in range(0, a.shape[1], kc):
        acc = acc + _dot(a[:, k0:k0 + kc], w_ref[k0:k0 + kc, :].astype(BF16))
    o_ref[...] = acc + b_ref[...]


def _adaln(c_all, w_mod, b_mod):
    n_layers, d, n6 = w_mod.shape
    r = c_all.shape[0]
    tn = _tile(n6, 1024)
    kc = _tile(d, 1024)
    blk = 2 * d * tn * 4 + d * tn * 2 + r * d * 4
    return pl.pallas_call(
        functools.partial(_adaln_kernel, kc=kc),
        out_shape=jax.ShapeDtypeStruct((n_layers, r, n6), F32),
        grid=(n_layers, n6 // tn),
        in_specs=[
            pl.BlockSpec((r, d), lambda l, j: (0, 0)),
            pl.BlockSpec((None, d, tn), lambda l, j: (l, 0, j)),
            pl.BlockSpec((None, 1, tn), lambda l, j: (l, 0, j)),
        ],
        out_specs=pl.BlockSpec((None, r, tn), lambda l, j: (l, 0, j)),
        compiler_params=_params(("arbitrary", "arbitrary"), blk),
        name="adaln_mod",
    )(c_all, w_mod, b_mod.reshape(n_layers, 1, n6))


def _modulated(x_ref, g_ref, sc_ref, sh_ref):
    return _rms(x_ref[...]) * g_ref[...] * (1.0 + sc_ref[...]) + sh_ref[...]


def _norm_mod_kernel(x_ref, g_ref, sc_ref, sh_ref, h_ref):
    h_ref[...] = _modulated(x_ref, g_ref, sc_ref, sh_ref).astype(h_ref.dtype)


def _mod_spec(mod, tm, d):
    if mod.shape[1] == 1:
        return pl.BlockSpec((None, 1, d), lambda b, i: (b, 0, 0))
    return pl.BlockSpec((None, tm, d), lambda b, i: (b, i, 0))


def _norm_mod(x, g, scale, shift):
    b, t, d = x.shape
    tm = _tile(t, 512)
    blk = 2 * tm * d * (4 + 2) + 6 * d * 4 + 2 * tm * d * 4
    return pl.pallas_call(
        _norm_mod_kernel,
        out_shape=jax.ShapeDtypeStruct((b, t, d), BF16),
        grid=(b, t // tm),
        in_specs=[
            pl.BlockSpec((None, tm, d), lambda b_, i: (b_, i, 0)),
            pl.BlockSpec((1, d), lambda b_, i: (0, 0)),
            _mod_spec(scale, tm, d),
            _mod_spec(shift, tm, d),
        ],
        out_specs=pl.BlockSpec((None, tm, d), lambda b_, i: (b_, i, 0)),
        compiler_params=_params(("arbitrary", "arbitrary"), blk),
        name="norm_mod",
    )(x, g.reshape(1, d), scale, shift)


def _norm_router_kernel(x_ref, g_ref, sc_ref, sh_ref, wr_ref, br_ref, hp_ref, ti_ref, gt_ref):
    h = _modulated(x_ref, g_ref, sc_ref, sh_ref)
    tm, d = h.shape
    dh = d // 2
    lo = pltpu.bitcast(h[:, :dh].astype(BF16).astype(F32), U32)
    hi = pltpu.bitcast(h[:, dh:].astype(BF16).astype(F32), U32)
    hp_ref[...] = (lo >> 16) | hi

    h1, h2, h3 = _split3(h)
    w1, w2, w3 = _split3(wr_ref[...])
    logits = (_dot(h1, w1) + (_dot(h1, w2) + _dot(h2, w1))
              + (_dot(h2, w2) + _dot(h1, w3) + _dot(h3, w1))) + br_ref[...]
    n_exp = logits.shape[1]
    lane = _iota(logits.shape, 1).astype(F32)
    work = logits
    vals, idxs = [], []
    for _ in range(TOP_K):
        m = jnp.max(work, axis=-1, keepdims=True)
        idx = jnp.min(jnp.where(work == m, lane, float(n_exp)), axis=-1, keepdims=True)
        vals.append(m)
        idxs.append(idx)
        work = jnp.where(lane == idx, -jnp.inf, work)
    es = [jnp.exp(v - vals[0]) for v in vals]
    den = es[0]
    for e in es[1:]:
        den = den + e
    lane_o = _iota((tm, LANES), 1)
    ti = jnp.zeros((tm, LANES), F32)
    gt = jnp.zeros((tm, LANES), F32)
    for k in range(TOP_K):
        ti = jnp.where(lane_o == k, idxs[k], ti)
        gt = jnp.where(lane_o == k, es[k] / den, gt)
    ti_ref[...] = ti.astype(I32)
    gt_ref[...] = gt


def _norm_router(x, g, scale, shift, w_router, b_router):
    b, t, d = x.shape
    n_exp = w_router.shape[1]
    tm = _tile(t, 256)
    blk = 2 * tm * d * 4 + 2 * tm * d * 2 + 8 * tm * d * 4 + 2 * d * n_exp * 4
    row = lambda b_, i: (b_, i, 0)
    return pl.pallas_call(
        _norm_router_kernel,
        out_shape=(jax.ShapeDtypeStruct((b, t, d // 2), U32),
                   jax.ShapeDtypeStruct((b, t, LANES), I32),
                   jax.ShapeDtypeStruct((b, t, LANES), F32)),
        grid=(b, t // tm),
        in_specs=[
            pl.BlockSpec((None, tm, d), row),
            pl.BlockSpec((1, d), lambda b_, i: (0, 0)),
            _mod_spec(scale, tm, d),
            _mod_spec(shift, tm, d),
            pl.BlockSpec((d, n_exp), lambda b_, i: (0, 0)),
            pl.BlockSpec((1, n_exp), lambda b_, i: (0, 0)),
        ],
        out_specs=(pl.BlockSpec((None, tm, d // 2), row),
                   pl.BlockSpec((None, tm, LANES), row),
                   pl.BlockSpec((None, tm, LANES), row)),
        compiler_params=_params(("arbitrary", "arbitrary"), blk),
        name="norm_router",
    )(x, g.reshape(1, d), scale, shift, w_router, b_router.reshape(1, n_exp))


def _mm_kernel(a_ref, w_ref, *rest, epi, head_dim, scale):
    *ins, o_ref, wb = rest

    @pl.when(pl.program_id(1) == 0)
    def _():
        wb[...] = w_ref[...].astype(BF16)

    acc = _dot(a_ref[...], wb[...])
    tn = acc.shape[1]
    if epi == "res":
        x_ref, g_ref = ins
        o_ref[...] = x_ref[...] + g_ref[...] * acc
    elif epi == "qnorm":
        (gq_ref,) = ins
        for c in range(tn // head_dim):
            a = acc[:, c * head_dim:(c + 1) * head_dim]
            o_ref[:, c * head_dim:(c + 1) * head_dim] = (_rms(a) * gq_ref[...] * scale).astype(o_ref.dtype)
    elif epi == "kvnorm":
        fl_ref, gk_ref = ins
        for c in range(tn // head_dim):
            sl = slice(c * head_dim, (c + 1) * head_dim)
            a = acc[:, sl]
            o_ref[:, sl] = jnp.where(fl_ref[:, sl] > 0.5, _rms(a) * gk_ref[:, sl], a)
    elif epi == "sigmoid":
        o_ref[...] = jax.nn.sigmoid(acc)
    else:
        raise ValueError(epi)


def _mm(a, w, *, col0, n, epi, extra=(), rows_per_batch=None, out_dtype=F32, head_dim=LANES, scale=1.0,
        name="mm"):
    m, k = a.shape
    tm = _tile(m, 1024)
    tn = _tile(n, 512)
    assert col0 % tn == 0
    j0 = col0 // tn
    in_specs = [pl.BlockSpec((tm, k), lambda j, i: (i, 0)),
                pl.BlockSpec((k, tn), lambda j, i: (0, j + j0))]
    operands = [a, w]
    if epi == "res":
        x, g = extra
        in_specs.append(pl.BlockSpec((tm, tn), lambda j, i: (i, j)))
        if g.ndim == 3:
            assert rows_per_batch % tm == 0
            in_specs.append(pl.BlockSpec((None, 1, tn), lambda j, i: (i * tm // rows_per_batch, 0, j)))
        else:
            in_specs.append(pl.BlockSpec((tm, tn), lambda j, i: (i, j)))
        operands += [x, g]
    elif epi == "qnorm":
        in_specs.append(pl.BlockSpec((1, head_dim), lambda j, i: (0, 0)))
        operands += list(extra)
    elif epi == "kvnorm":
        in_specs += [pl.BlockSpec((1, tn), lambda j, i: (0, j))] * 2
        operands += list(extra)
    blk = 2 * tm * k * 2 + 2 * k * tn * 4 + k * tn * 2 + 6 * tm * tn * 4
    return pl.pallas_call(
        functools.partial(_mm_kernel, epi=epi, head_dim=head_dim, scale=scale),
        out_shape=jax.ShapeDtypeStruct((m, n), out_dtype),
        grid=(n // tn, m // tm),
        in_specs=in_specs,
        out_specs=pl.BlockSpec((tm, tn), lambda j, i: (i, j)),
        scratch_shapes=[pltpu.VMEM((k, tn), BF16)],
        compiler_params=_params(("arbitrary", "arbitrary"), blk),
        name=name,
    )(*operands)


def _conv_in_kernel(a_ref, wb_ref, wc_ref, wu_ref, cw_ref, *rest, tiles_per_batch, seq, per_token_prev):
    if per_token_prev:
        p1_ref, p2_ref, o_ref, v_ref, wbuf, ext = rest
    else:
        o_ref, st_ref, wbuf, ext = rest
    i = pl.program_id(1)

    @pl.when(i == 0)
    def _():
        wbuf[0] = wb_ref[...].astype(BF16)
        wbuf[1] = wc_ref[...].astype(BF16)
        wbuf[2] = wu_ref[...].astype(BF16)

    a = a_ref[...]
    b_gate = _dot(a, wbuf[0])
    v = _dot(a, wbuf[1]) * _dot(a, wbuf[2])
    tm, tn = v.shape

    if per_token_prev:
        ext[0:SUBLANES, :] = jnp.zeros((SUBLANES, tn), F32)
    else:
        @pl.when(i % tiles_per_batch == 0)
        def _():
            ext[0:SUBLANES, :] = jnp.zeros((SUBLANES, tn), F32)

    ext[SUBLANES:SUBLANES + tm, :] = v
    s1 = ext[SUBLANES - 1:SUBLANES - 1 + tm, :]
    s2 = ext[SUBLANES - 2:SUBLANES - 2 + tm, :]
    if per_token_prev:
        tpos = _iota((tm, 1), 0) % seq
        s1 = jnp.where(tpos >= 1, s1, p1_ref[...])
        s2 = jnp.where(tpos >= 2, s2, p2_ref[...])
    cw = cw_ref[...]
    conv = s2 * cw[0:1, :] + s1 * cw[1:2, :] + v * cw[2:3, :]
    o_ref[...] = (b_gate * conv).astype(o_ref.dtype)

    if per_token_prev:
        v_ref[...] = v
    else:
        ext[0:SUBLANES, :] = ext[tm:tm + SUBLANES, :]

        @pl.when(i % tiles_per_batch == tiles_per_batch - 1)
        def _():
            st_ref[...] = ext[tm + SUBLANES - 2:tm + SUBLANES, :]


def _conv_in(h, w_in, conv_w, *, batch, seq, prev=None):
    m, d = h.shape
    tn = _tile(d, 256)
    nd = d // tn
    per_token_prev = prev is not None
    tm = m if per_token_prev else _tile(seq, 512)
    tiles_per_batch = max(seq // tm, 1)
    in_specs = [pl.BlockSpec((tm, d), lambda j, i: (i, 0)),
                pl.BlockSpec((d, tn), lambda j, i: (0, j)),
                pl.BlockSpec((d, tn), lambda j, i: (0, j + nd)),
                pl.BlockSpec((d, tn), lambda j, i: (0, j + 2 * nd)),
                pl.BlockSpec((3, tn), lambda j, i: (0, j))]
    operands = [h, w_in, w_in, w_in, conv_w]
    tile_spec = pl.BlockSpec((tm, tn), lambda j, i: (i, j))
    if per_token_prev:
        in_specs += [tile_spec, tile_spec]
        operands += list(prev)
        out_shape = (jax.ShapeDtypeStruct((m, d), BF16), jax.ShapeDtypeStruct((m, d), F32))
        out_specs = (tile_spec, tile_spec)
    else:
        out_shape = (jax.ShapeDtypeStruct((m, d), BF16), jax.ShapeDtypeStruct((batch, 2, d), F32))
        out_specs = (tile_spec, pl.BlockSpec((None, 2, tn), lambda j, i: (i // tiles_per_batch, 0, j)))
    blk = 2 * tm * d * 2 + 6 * d * tn * 4 + 3 * d * tn * 2 + 10 * tm * tn * 4
    return pl.pallas_call(
        functools.partial(_conv_in_kernel, tiles_per_batch=tiles_per_batch, seq=seq,
                          per_token_prev=per_token_prev),
        out_shape=out_shape,
        grid=(nd, m // tm),
        in_specs=in_specs,
        out_specs=out_specs,
        scratch_shapes=[pltpu.VMEM((3, d, tn), BF16), pltpu.VMEM((tm + 2 * SUBLANES, tn), F32)],
        compiler_params=_params(("arbitrary", "arbitrary"), blk),
        name="conv_in",
    )(*operands)


def _conv_prev_rows(state, seq):
    b, _, d = state.shape
    zeros = jnp.zeros((b, seq, d), state.dtype)
    p1 = zeros.at[:, 0].set(state[:, 1])
    p2 = zeros.at[:, 0].set(state[:, 0]).at[:, 1].set(state[:, 1])
    return p1.reshape(b * seq, d), p2.reshape(b * seq, d)


def _plan_kernel(ti_ref, pos_ref, cnt_ref, carry):
    @pl.when(pl.program_id(0) == 0)
    def _():
        carry[...] = jnp.zeros(carry.shape, F32)

    ti = ti_ref[...]
    tm = ti.shape[0]
    e_iota = _iota((tm, LANES), 1)
    onehots = [jnp.where(ti[:, k:k + 1] == e_iota, 1.0, 0.0) for k in range(TOP_K)]
    hits = onehots[0]
    for oh in onehots[1:]:
        hits = hits + oh
    strictly_lower = jnp.where(_iota((tm, tm), 0) > _iota((tm, tm), 1), 1.0, 0.0).astype(BF16)
    before = _dot(strictly_lower, hits.astype(BF16)) + carry[...]
    out = jnp.zeros((tm, LANES), F32)
    for k in range(TOP_K):
        out = jnp.where(e_iota == k, jnp.sum(onehots[k] * before, axis=-1, keepdims=True), out)
    pos_ref[...] = out.astype(I32)
    carry[...] = carry[...] + jnp.sum(hits, axis=0, keepdims=True)
    cnt_ref[...] = carry[...]


def _plan(topi):
    n = topi.shape[0]
    tm = _tile(n, 256)
    return pl.pallas_call(
        _plan_kernel,
        out_shape=(jax.ShapeDtypeStruct((n, LANES), I32), jax.ShapeDtypeStruct((1, LANES), F32)),
        grid=(n // tm,),
        in_specs=[pl.BlockSpec((tm, LANES), lambda i: (i, 0))],
        out_specs=(pl.BlockSpec((tm, LANES), lambda i: (i, 0)), pl.BlockSpec((1, LANES), lambda i: (0, 0))),
        scratch_shapes=[pltpu.VMEM((1, LANES), F32)],
        compiler_params=_params(("arbitrary",), 8 * tm * LANES * 4 + tm * tm * 8),
        name="moe_plan",
    )(topi)


def _dispatch_tables(topi, n_exp, sub, rmax):
    n = topi.shape[0]
    ids = topi[:, :TOP_K]
    n_pairs = n * TOP_K
    n_pad = -(-n // 256) * 256
    pos, counts = _plan(jnp.pad(topi, ((0, n_pad - n), (0, 0)), constant_values=-1))
    counts = counts[0, :n_exp].astype(I32)
    padded = (counts + sub - 1) // sub * sub
    pad_start = jnp.cumsum(padded) - padded
    dest = pad_start[ids] + pos[:n, :TOP_K]
    n_rows = -(-(n_pairs + n_exp * (sub - 1)) // sub) * sub
    row_tok = jnp.zeros((n_rows,), I32).at[dest.reshape(-1)].set(jnp.arange(n_pairs, dtype=I32) // TOP_K)
    n_sb = (counts + rmax - 1) // rmax
    cum = jnp.cumsum(n_sb)
    total = cum[-1]
    n_sb_max = n_exp + n_pairs // rmax
    s = jnp.arange(n_sb_max, dtype=I32)
    e_of = jnp.minimum(jnp.searchsorted(cum, s, side="right"), n_exp - 1).astype(I32)
    local = s - (cum - n_sb)[e_of]
    active = s < total
    rows = jnp.where(active, jnp.minimum(counts[e_of] - local * rmax, rmax), 0).astype(I32)
    start = jnp.where(active, pad_start[e_of] + local * rmax, 0).astype(I32)
    sb_exp = jnp.where(active, e_of, e_of[jnp.maximum(total - 1, 0)]).astype(I32)
    n_used = jnp.sum(padded).astype(I32).reshape(1)
    return dest.astype(I32), row_tok, sb_exp, start, rows, n_used, n_rows, n_sb_max


def _expert_kernel(exp_ref, start_ref, rows_ref, tok_ref, used_ref,
                   h_hbm, wg_ref, wu_ref, bg_ref, bu_ref, wd_ref, bd_ref, ys_hbm,
                   xbuf, act, accg, accu, wgu, wdb, ostage, gsem, osem, *, n1, sub, tf, dt):
    sb = pl.program_id(0)
    s = pl.program_id(1)
    rows = rows_ref[sb]
    start = start_ref[sb]
    n_sub = (rows + sub - 1) // sub
    dh = xbuf.shape[1]

    @pl.when(jnp.logical_and(sb == 0, s == 0))
    def _():
        ostage[0] = jnp.zeros(ostage.shape[1:], F32)
        n_tail = (ys_hbm.shape[0] - used_ref[0]) // sub

        def tail_copy(t, col):
            row0 = pl.multiple_of(used_ref[0] + t * sub, sub)
            return pltpu.make_async_copy(ostage.at[0], ys_hbm.at[pl.ds(row0, sub), pl.ds(col * dt, dt)],
                                         osem.at[0])

        def issue(t, c):
            for col in range(ys_hbm.shape[1] // dt):
                tail_copy(t, col).start()
            return c

        def drain(t, c):
            for col in range(ys_hbm.shape[1] // dt):
                tail_copy(t, col).wait()
            return c

        lax.fori_loop(0, n_tail, issue, 0)
        lax.fori_loop(0, n_tail, drain, 0)

    def gather_copy(r):
        tok = tok_ref[start + r]
        return pltpu.make_async_copy(h_hbm.at[pl.ds(tok, 1)], xbuf.at[pl.ds(r, 1)], gsem)

    @pl.when(jnp.logical_and(s == 0, rows > 0))
    def _():
        def issue(r, c):
            gather_copy(r).start()
            return c

        def drain(r, c):
            gather_copy(r).wait()
            return c

        lax.fori_loop(0, n_sub * sub, issue, 0)
        lax.fori_loop(0, n_sub * sub, drain, 0)

    @pl.when(jnp.logical_and(sb == 0, s == 0))
    def _():
        accg[...] = jnp.zeros(accg.shape, F32)
        accu[...] = jnp.zeros(accu.shape, F32)

    @pl.when(jnp.logical_and(s < n1, rows > 0))
    def _():
        nt = s // 2
        first = s % 2 == 0
        shift = jnp.where(first, 16, 0).astype(U32)

        def half_rows(i):
            words = xbuf[pl.ds(pl.multiple_of(i * sub, sub), sub), :]
            return pltpu.bitcast((words << shift) & jnp.uint32(0xFFFF0000), F32).astype(BF16)

        def finish(i, pg, pu):
            r = pl.ds(pl.multiple_of(i * sub, sub), sub)
            g = jnp.where(first, 0.0, accg[r, :]) + pg
            u = jnp.where(first, 0.0, accu[r, :]) + pu
            accg[r, :] = g
            accu[r, :] = u
            g = jnp.minimum(g + bg_ref[...], SWIGLU_LIMIT)
            u = jnp.clip(u + bu_ref[...], -SWIGLU_LIMIT, SWIGLU_LIMIT)
            act[nt, r, :] = (g * jax.nn.sigmoid(SWIGLU_ALPHA * g) * (u + 1.0)).astype(BF16)

        xk = half_rows(0)
        pg = jnp.zeros((sub, tf), F32)
        pu = jnp.zeros((sub, tf), F32)
        kc = min(dh, 512)
        for k0 in range(0, dh, kc):
            wgc = wg_ref[k0:k0 + kc, :].astype(BF16)
            wuc = wu_ref[k0:k0 + kc, :].astype(BF16)
            wgu[0, k0:k0 + kc, :] = wgc
            wgu[1, k0:k0 + kc, :] = wuc
            pg = pg + _dot(xk[:, k0:k0 + kc], wgc)
            pu = pu + _dot(xk[:, k0:k0 + kc], wuc)
        finish(0, pg, pu)

        def sub_block(i, c):
            xk = half_rows(i)
            finish(i, _dot(xk, wgu[0]), _dot(xk, wgu[1]))
            return c

        lax.fori_loop(1, n_sub, sub_block, 0)

    @pl.when(jnp.logical_and(s >= n1, rows > 0))
    def _():
        col = pl.multiple_of((s - n1) * dt, dt)

        def out_copy(i, slot):
            row0 = pl.multiple_of(start + i * sub, sub)
            return pltpu.make_async_copy(ostage.at[slot], ys_hbm.at[pl.ds(row0, sub), pl.ds(col, dt)],
                                         osem.at[slot])

        n_f = act.shape[0]
        y = bd_ref[...]
        for f in range(n_f):
            wdc = wd_ref[f * tf:(f + 1) * tf, :].astype(BF16)
            wdb[f * tf:(f + 1) * tf, :] = wdc
            y = y + _dot(act[f, 0:sub, :], wdc)
        ostage[0] = y
        out_copy(0, 0).start()

        def sub_block(i, c):
            slot = i % 2

            @pl.when(i >= 2)
            def _():
                out_copy(i - 2, slot).wait()

            r0 = pl.multiple_of(i * sub, sub)
            y = bd_ref[...] + _dot(act[0, pl.ds(r0, sub), :], wdb[0:tf, :])
            for f in range(1, n_f):
                y = y + _dot(act[f, pl.ds(r0, sub), :], wdb[f * tf:(f + 1) * tf, :])
            ostage[slot] = y
            out_copy(i, slot).start()
            return c

        lax.fori_loop(1, n_sub, sub_block, 0)

        @pl.when(n_sub >= 2)
        def _():
            out_copy(n_sub - 2, n_sub % 2).wait()

        out_copy(n_sub - 1, (n_sub - 1) % 2).wait()


def _experts(h_packed, tables, w_gu, b_gu, w_down, b_down, *, layer, sub, rmax):
    _, row_tok, sb_exp, sb_start, sb_rows, n_used, n_rows, n_sb_max = tables
    n_layers, n_exp, d, f2 = w_gu.shape
    f = f2 // 2
    dh = d // 2
    tf = _tile(f, 512)
    dt = _tile(d, 512)
    n_f = f // tf
    n1, n2 = 2 * n_f, d // dt

    def gu_step(sb, s, rows_ref):
        return jnp.where(rows_ref[sb] > 0, jnp.minimum(s, n1 - 1), n1 - 1)

    def d_idx(sb, s, rows_ref):
        return jnp.where(rows_ref[sb] > 0, jnp.maximum(s - n1, 0), n2 - 1)

    in_specs = [
        pl.BlockSpec(memory_space=pl.ANY),
        pl.BlockSpec((None, None, dh, tf), lambda sb, s, e, st, rw, tk, us: (
            layer, e[sb], gu_step(sb, s, rw) % 2, gu_step(sb, s, rw) // 2)),
        pl.BlockSpec((None, None, dh, tf), lambda sb, s, e, st, rw, tk, us: (
            layer, e[sb], gu_step(sb, s, rw) % 2, n_f + gu_step(sb, s, rw) // 2)),
        pl.BlockSpec((None, None, 1, tf),
                     lambda sb, s, e, st, rw, tk, us: (layer, e[sb], 0, gu_step(sb, s, rw) // 2)),
        pl.BlockSpec((None, None, 1, tf),
                     lambda sb, s, e, st, rw, tk, us: (layer, e[sb], 0, n_f + gu_step(sb, s, rw) // 2)),
        pl.BlockSpec((None, None, f, dt),
                     lambda sb, s, e, st, rw, tk, us: (layer, e[sb], 0, d_idx(sb, s, rw))),
        pl.BlockSpec((None, None, 1, dt),
                     lambda sb, s, e, st, rw, tk, us: (layer, e[sb], 0, d_idx(sb, s, rw))),
    ]
    blk = (rmax * dh * 4 + rmax * f * 2 + 2 * rmax * tf * 4 + 4 * dh * tf * 4 + 2 * f * dt * 4 + 2 * dh * tf * 2
           + f * dt * 2 + 2 * sub * dt * 4 + 6 * sub * dh * 2)
    return pl.pallas_call(
        functools.partial(_expert_kernel, n1=n1, sub=sub, tf=tf, dt=dt),
        out_shape=jax.ShapeDtypeStruct((n_rows, d), F32),
        grid_spec=pltpu.PrefetchScalarGridSpec(
            num_scalar_prefetch=5,
            grid=(n_sb_max, n1 + n2),
            in_specs=in_specs,
            out_specs=pl.BlockSpec(memory_space=pl.ANY),
            scratch_shapes=[
                pltpu.VMEM((rmax, dh), U32),
                pltpu.VMEM((n_f, rmax, tf), BF16),
                pltpu.VMEM((rmax, tf), F32),
                pltpu.VMEM((rmax, tf), F32),
                pltpu.VMEM((2, dh, tf), BF16),
                pltpu.VMEM((f, dt), BF16),
                pltpu.VMEM((2, sub, dt), F32),
                pltpu.SemaphoreType.DMA(()),
                pltpu.SemaphoreType.DMA((2,)),
            ]),
        compiler_params=_params(("arbitrary", "arbitrary"), blk),
        name="moe_experts",
    )(sb_exp, sb_start, sb_rows, row_tok, n_used, h_packed, w_gu, w_gu,
      b_gu.reshape(n_layers, n_exp, 1, f2), b_gu.reshape(n_layers, n_exp, 1, f2), w_down,
      b_down.reshape(n_layers, n_exp, 1, d))


def _combine_kernel(dest_ref, ys_hbm, x_ref, g_ref, gate_ref, o_ref, buf, sem, *, tiles_per_batch):
    tm = x_ref.shape[0]
    tok0 = (pl.program_id(0) * tiles_per_batch + pl.program_id(1)) * tm

    def row_copy(r, k):
        src = dest_ref[(tok0 + r) * TOP_K + k]
        return pltpu.make_async_copy(ys_hbm.at[pl.ds(src, 1)], buf.at[k, pl.ds(r, 1)], sem)

    def issue(r, c):
        for k in range(TOP_K):
            row_copy(r, k).start()
        return c

    def drain(r, c):
        for k in range(TOP_K):
            row_copy(r, k).wait()
        return c

    lax.fori_loop(0, tm, issue, 0)
    lax.fori_loop(0, tm, drain, 0)
    gate = gate_ref[...]
    y = gate[:, 0:1] * buf[0]
    for k in range(1, TOP_K):
        y = y + gate[:, k:k + 1] * buf[k]
    o_ref[...] = x_ref[...] + g_ref[...] * y


def _combine(ys, dest, x, g, gate):
    b, t, d = x.shape
    tm = _tile(t, 128)
    tiles_per_batch = t // tm
    row = lambda b_, i, dref: (b_, i, 0)
    if g.shape[1] == 1:
        g_spec = pl.BlockSpec((None, 1, d), lambda b_, i, dref: (b_, 0, 0))
    else:
        g_spec = pl.BlockSpec((None, tm, d), row)
    blk = TOP_K * tm * d * 4 + 6 * tm * d * 4
    return pl.pallas_call(
        functools.partial(_combine_kernel, tiles_per_batch=tiles_per_batch),
        out_shape=jax.ShapeDtypeStruct((b, t, d), F32),
        grid_spec=pltpu.PrefetchScalarGridSpec(
            num_scalar_prefetch=1,
            grid=(b, tiles_per_batch),
            in_specs=[pl.BlockSpec(memory_space=pl.ANY),
                      pl.BlockSpec((None, tm, d), row),
                      g_spec,
                      pl.BlockSpec((None, tm, LANES), row)],
            out_specs=pl.BlockSpec((None, tm, d), row),
            scratch_shapes=[pltpu.VMEM((TOP_K, tm, d), F32), pltpu.SemaphoreType.DMA(())]),
        compiler_params=_params(("arbitrary", "arbitrary"), blk),
        name="moe_combine",
    )(dest.reshape(-1), ys, x, g, gate)


def _moe(xp, xs, g, mods_p, mods_s, w_router, b_router, w_gu, b_gu, w_down, b_down, *, layer):
    (sc_p, sh_p, g_p), (sc_s, sh_s, g_s) = mods_p, mods_s
    d = xp.shape[-1]
    n_exp = w_router.shape[1]
    f = w_down.shape[2]
    hp, ti_p, gt_p = _norm_router(xp, g, sc_p, sh_p, w_router, b_router)
    hs, ti_s, gt_s = _norm_router(xs, g, sc_s, sh_s, w_router, b_router)
    n_p = xp.shape[0] * xp.shape[1]
    h_all = jnp.concatenate([hp.reshape(n_p, d // 2), hs.reshape(-1, d // 2)], axis=0)
    topi = jnp.concatenate([ti_p.reshape(n_p, LANES), ti_s.reshape(-1, LANES)], axis=0)
    sub = MOE_SUB_ROWS
    rmax = MOE_SUBS_PER_SUPERBLOCK * sub
    tables = _dispatch_tables(topi, n_exp, sub, rmax)
    ys = _experts(h_all, tables, w_gu, b_gu, w_down, b_down, layer=layer, sub=sub, rmax=rmax)
    dest = tables[0]
    xp = _combine(ys, dest[:n_p], xp, g_p, gt_p)
    xs = _combine(ys, dest[n_p:], xs, g_s, gt_s)
    return xp, xs


def _nsa_project(h, w_in, q_g, k_g):
    d = h.shape[1]
    hd = LANES
    kvh = N_KV_HEADS
    grp = N_HEADS // kvh
    q_dim = N_HEADS * hd
    kv_dim = 2 * kvh * hd
    q = _mm(h, w_in, col0=0, n=q_dim, epi="qnorm", extra=(q_g.reshape(1, hd),), out_dtype=BF16,
            head_dim=hd, scale=hd ** -0.5 * LOG2_E, name="nsa_q")
    ones = jnp.ones((kvh * hd,), F32)
    zeros = jnp.zeros((kvh * hd,), F32)
    flags = jnp.concatenate([zeros, zeros, ones, zeros, ones, zeros]).reshape(1, 3 * kv_dim)
    gains = jnp.concatenate([ones, ones, jnp.tile(k_g[1], kvh), ones, jnp.tile(k_g[2], kvh), ones])
    kv = _mm(h, w_in, col0=q_dim, n=3 * kv_dim, epi="kvnorm", extra=(flags, gains.reshape(1, 3 * kv_dim)),
             head_dim=hd, name="nsa_kv")
    w_gate = w_in[:, q_dim + 3 * kv_dim:].reshape(d, 3, kvh, grp)
    w_gate = jnp.transpose(w_gate, (0, 2, 1, 3)).reshape(d, kvh, 3 * grp)
    w_gate = jnp.pad(w_gate, ((0, 0), (0, 0), (0, LANES - 3 * grp))).reshape(d, kvh * LANES)
    gates = _mm(h, w_gate, col0=0, n=kvh * LANES, epi="sigmoid", name="nsa_gates")
    return q, kv, gates


def _cmp_part_kernel(pt_ref, *refs, n_pages, kvh, row_view):
    n_refs = n_pages if row_view else n_pages * 2 * kvh
    pages = refs[:n_refs]
    w_ref, pe_ref, p_out, pe_out, wbf = refs[n_refs:]
    rpt = 2 * kvh
    chunks = PAGE_SIZE // CMP_STRIDE

    def chunk_rows(j, h, s):
        if row_view:
            return [pg[pl.ds(s * rpt + j * kvh + h, chunks, stride=CMP_STRIDE * rpt), :] for pg in pages]
        return [pg[pl.ds(s, chunks, stride=CMP_STRIDE), :] for pg in pages[j * kvh + h::rpt]]

    @pl.when(pl.program_id(0) == 0)
    def _():
        for j in range(2):
            wbf[j] = w_ref[j].astype(BF16)
            pe_out[j] = _dot(pe_ref[j].astype(BF16), wbf[j])

    rows = n_pages * PAGE_SIZE // CMP_STRIDE
    for j in range(2):
        per_head = []
        for h in range(kvh):
            pieces = [jnp.concatenate(chunk_rows(j, h, s), axis=0) for s in range(CMP_STRIDE)]
            per_head.append(jnp.concatenate(pieces, axis=1))
        x = jnp.concatenate(per_head, axis=0).astype(BF16)
        y = _dot(x, wbf[j])
        for h in range(kvh):
            p_out[j, h] = y[h * rows:(h + 1) * rows, :]


def _cmp_out_kernel(pk_ref, pv_ref, b1_ref, pe_ref, w2_ref, b2_ref, kg_ref, kc_ref, vc_ref, shifted):
    n_ch = pk_ref.shape[0]
    hid_dim = pk_ref.shape[1] // 2
    for j, (p_ref, o_ref) in enumerate(((pk_ref, kc_ref), (pv_ref, vc_ref))):
        shifted[0:n_ch, :] = p_ref[:, hid_dim:2 * hid_dim]
        shifted[n_ch:n_ch + SUBLANES, :] = jnp.zeros((SUBLANES, hid_dim), F32)
        hid = (b1_ref[j:j + 1, :] + pe_ref[j, 0:1, 0:hid_dim] + pe_ref[j, 1:2, hid_dim:2 * hid_dim]
               + p_ref[:, 0:hid_dim] + shifted[1:1 + n_ch, :])
        a = hid * jax.nn.sigmoid(hid)
        out = _dot(a.astype(BF16), w2_ref[j].astype(BF16)) + b2_ref[j:j + 1, :]
        if j == 0:
            out = _rms(out) * kg_ref[...]
        o_ref[...] = out.astype(o_ref.dtype)


def _compress(src_pages, page_table, batch, w1, b1, w2, b2, pe, kg, *, layer=None):
    kvh = N_KV_HEADS
    hd = LANES
    n_log = page_table.shape[0]
    row_view = layer is not None
    n_pages = _tile(n_log, 16 if row_view else 8)
    chunks_per_page = PAGE_SIZE // CMP_STRIDE
    n_chunks = n_log * chunks_per_page
    hid2 = w1.shape[1] * w1.shape[-1]
    kdim = CMP_STRIDE * hd
    w1cat = jnp.transpose(w1, (0, 2, 3, 1, 4)).reshape(2, kdim, hid2)
    pe_rows = jnp.pad(pe.reshape(2, -1, kdim), ((0, 0), (0, SUBLANES - pe.shape[1]), (0, 0)))

    if row_view:
        rpt = 2 * kvh
        src_pages = src_pages.reshape(src_pages.shape[0], src_pages.shape[1], PAGE_SIZE * rpt, hd)
        page_specs = [pl.BlockSpec((None, None, PAGE_SIZE * rpt, hd),
                                   lambda i, pt, r=r: (layer, pt[i * n_pages + r], 0, 0)) for r in range(n_pages)]
    else:
        page_specs = [pl.BlockSpec((None, PAGE_SIZE, hd),
                                   lambda i, pt, r=r, cb=cb: (pt[i * n_pages + r], 0, cb))
                      for r in range(n_pages) for cb in range(2 * kvh)]
    rows = n_pages * chunks_per_page
    blk = (2 * n_pages * PAGE_SIZE * 2 * kvh * hd * 4 + 3 * 2 * kdim * hid2 * 4
           + 4 * kvh * rows * (kdim + hid2) * 4)
    parts, pe_out = pl.pallas_call(
        functools.partial(_cmp_part_kernel, n_pages=n_pages, kvh=kvh, row_view=row_view),
        out_shape=(jax.ShapeDtypeStruct((2, kvh, n_chunks, hid2), F32),
                   jax.ShapeDtypeStruct((2, SUBLANES, hid2), F32)),
        grid_spec=pltpu.PrefetchScalarGridSpec(
            num_scalar_prefetch=1,
            grid=(n_log // n_pages,),
            in_specs=page_specs + [
                pl.BlockSpec((2, kdim, hid2), lambda i, pt: (0, 0, 0)),
                pl.BlockSpec((2, SUBLANES, kdim), lambda i, pt: (0, 0, 0))],
            out_specs=(pl.BlockSpec((2, kvh, rows, hid2), lambda i, pt: (0, 0, i, 0)),
                       pl.BlockSpec((2, SUBLANES, hid2), lambda i, pt: (0, 0, 0))),
            scratch_shapes=[pltpu.VMEM((2, kdim, hid2), BF16)]),
        compiler_params=_params(("arbitrary",), blk),
        name="cmp_part",
    )(page_table, *([src_pages] * len(page_specs)), w1cat, pe_rows)

    n_ch = n_chunks // batch
    hid = hid2 // 2
    part_spec = lambda j: pl.BlockSpec((None, None, n_ch, hid2), lambda b, h: (j, h, b, 0))
    full = lambda *shape: pl.BlockSpec(shape, lambda b, h: (0,) * len(shape))
    out_spec = pl.BlockSpec((None, None, n_ch, hd), lambda b, h: (b, h, 0, 0))
    return pl.pallas_call(
        _cmp_out_kernel,
        out_shape=(jax.ShapeDtypeStruct((batch, kvh, n_ch, hd), BF16),) * 2,
        grid=(batch, kvh),
        in_specs=[part_spec(0), part_spec(1), full(2, hid), full(2, SUBLANES, hid2), full(2, hid, hd),
                  full(2, hd), full(1, hd)],
        out_specs=(out_spec, out_spec),
        scratch_shapes=[pltpu.VMEM((n_ch + SUBLANES, hid), F32)],
        compiler_params=_params(("arbitrary", "arbitrary"), 12 * n_ch * hid2 * 4),
        name="cmp_out",
    )(parts, parts, b1, pe_out, w2, b2, kg.reshape(1, hd))


def _overlap(n_ch, n_blk):
    c0 = _iota((n_ch, n_blk), 0) * CMP_STRIDE
    b0 = _iota((n_ch, n_blk), 1) * SEL_BLOCK
    return jnp.where(c0 < b0 + SEL_BLOCK, jnp.where(c0 + CMP_BLOCK > b0, 1.0, 0.0), 0.0).astype(BF16)


def _cmp_scores(q, kc, slope, tpos, n_cmp):
    n_ch = kc.shape[0]
    c_idx = _iota((1, n_ch), 1)
    d_c = tpos - (c_idx * CMP_STRIDE + (CMP_BLOCK - 1))
    valid = jnp.where(c_idx < n_cmp, d_c, -1) >= 0
    s = _dot_nt(q, kc) - slope * d_c.astype(F32)
    s = jnp.where(valid, s, -jnp.inf)
    m = jnp.max(s, axis=-1, keepdims=True)
    m = jnp.where(m == -jnp.inf, 0.0, m)
    e = jnp.exp2(s - m)
    return e / jnp.maximum(jnp.sum(e, axis=-1, keepdims=True), 1e-30)


def _with_position_lanes(k, pos0):
    rows, hd = k.shape
    pos = pos0 + _iota((rows, hd), 0)
    lane = _iota((rows, hd), 1)
    ext = jnp.where(lane < 3, pos // LANES * LANES, jnp.where(lane < 6, pos % LANES, 0))
    return jnp.concatenate([k, ext.astype(F32).astype(BF16)], axis=1)


def _with_slope_lanes(q, parts):
    lane = _iota(q.shape, 1)
    ext = jnp.zeros(q.shape, F32)
    for c, part in enumerate(parts):
        ext = jnp.where(lane == c, part, jnp.where(lane == c + 3, part, ext))
    return jnp.concatenate([q, ext.astype(BF16)], axis=1)


def _with_ones(v):
    return jnp.concatenate([v, jnp.ones(v.shape, BF16)], axis=1)


def _force_and_mask(imp, tpos):
    blk = _iota((1, imp.shape[1]), 1)
    cur = tpos // SEL_BLOCK
    forced = jnp.logical_or(blk == 0, jnp.logical_or(blk == cur, blk == cur - 1))
    imp = jnp.where(forced, FORCE_SCORE, imp)
    return jnp.where(blk > cur, -jnp.inf, imp)


def _attn_prompt_kernel(slope_ref, q_ref, kc_ref, vc_ref, ks_ref, vs_ref, kw_ref, vw_ref, gt_ref, o_ref,
                        m_s, acc_s, oc_s, qa_s, *, grp, seq, tq, tk, n_cmp, n_blk, n_top, wl):
    hk = pl.program_id(1)
    t0 = pl.program_id(2) * tq
    hd = LANES
    n_heads = slope_ref.shape[0] // 4
    row_t = t0 + _iota((tq, 1), 0)
    heads = [(g, slope_ref[hk * grp + g], slice(g * hd, (g + 1) * hd)) for g in range(grp)]

    kc = kc_ref[...]
    vc = vc_ref[...]
    n_ch = kc.shape[0]
    psum = jnp.zeros((tq, n_ch), F32)
    for g, slope, cols in heads:
        q = q_ref[:, cols]
        p = _cmp_scores(q, kc, slope, row_t, n_cmp)
        psum = psum + p
        oc_s[g] = _dot(p.astype(BF16), vc)
        qa_s[g] = _with_slope_lanes(q, [slope_ref[(1 + c) * n_heads + hk * grp + g] for c in range(3)])
    c0 = _iota((n_blk, n_ch), 1) * CMP_STRIDE
    b0 = _iota((n_blk, n_ch), 0) * SEL_BLOCK
    ov_t = jnp.where(c0 < b0 + SEL_BLOCK, jnp.where(c0 + CMP_BLOCK > b0, 1.0, 0.0), 0.0).astype(BF16)
    p1, p2, p3 = _split3(psum)
    imp = _dot_nt(ov_t, p1) + _dot_nt(ov_t, p2) + _dot_nt(ov_t, p3)
    blk = _iota((n_blk, 1), 0)
    cur = (t0 + _iota((1, tq), 1)) // SEL_BLOCK
    forced = jnp.logical_or(blk == 0, jnp.logical_or(blk == cur, blk == cur - 1))
    imp = jnp.where(blk > cur, -jnp.inf, jnp.where(forced, FORCE_SCORE, imp))
    rank = jnp.zeros((n_blk, tq), F32)
    for j in range(n_blk):
        rj = imp[j:j + 1, :]
        tie = jnp.where(blk > j, 1.0, 0.0)
        rank = rank + jnp.where(rj > imp, 1.0, jnp.where(rj == imp, tie, 0.0))
    sel_t = jnp.where(rank < n_top, 1.0, 0.0)
    nb_pad = -(-n_blk // LANES) * LANES
    sel_t = jnp.concatenate([sel_t, jnp.zeros((nb_pad - n_blk, tq), F32)], axis=0)
    sel = sel_t.T.astype(BF16)

    m_s[...] = jnp.full(m_s.shape, MASKED, F32)
    acc_s[...] = jnp.zeros(acc_s.shape, F32)

    def kv_tile(kt, carry):
        k0 = pl.multiple_of(kt * tk, tk)
        ka = _with_position_lanes(ks_ref[pl.ds(k0, tk), :].astype(BF16), k0)
        va = _with_ones(vs_ref[pl.ds(k0, tk), :].astype(BF16))
        expand = jnp.where(_iota((nb_pad, tk), 0) == (k0 + _iota((nb_pad, tk), 1)) // SEL_BLOCK, 1.0, 0.0)
        picked = _dot(sel, expand.astype(BF16))
        ok = jnp.where(row_t - (k0 + _iota((1, tk), 1)) >= 0, picked, 0.0) > 0.5
        for g, _, _ in heads:
            s = jnp.where(ok, _dot_nt(qa_s[g], ka), MASKED)
            m_old = m_s[g]
            m_new = jnp.maximum(m_old, jnp.max(s, axis=-1, keepdims=True))
            p = jnp.exp2(s - m_new)
            acc_s[g] = jnp.exp2(m_old - m_new) * acc_s[g] + _dot(p.astype(BF16), va)
            m_s[g] = m_new
        return carry

    lax.fori_loop(0, (t0 + tq + tk - 1) // tk, kv_tile, 0)

    ws = pl.multiple_of(jnp.clip(t0 + tq - wl, 0, seq - wl), SUBLANES)
    kwa = _with_position_lanes(kw_ref[pl.ds(ws, wl), :].astype(BF16), ws)
    vwa = _with_ones(vw_ref[pl.ds(ws, wl), :].astype(BF16))
    dw = row_t - (ws + _iota((1, wl), 1))
    okw = jnp.where(dw >= 0, jnp.where(dw < WINDOW, 1.0, 0.0), 0.0) > 0.5
    gates = gt_ref[...]
    for g, _, cols in heads:
        s = jnp.where(okw, _dot_nt(qa_s[g], kwa), MASKED)
        e = jnp.exp2(s - jnp.max(s, axis=-1, keepdims=True))
        win = _dot(e.astype(BF16), vwa)
        o_w = win[:, :hd] / win[:, hd:hd + 1]
        acc = acc_s[g]
        o_s = acc[:, :hd] / acc[:, hd:hd + 1]
        out = (gates[:, g:g + 1] * oc_s[g] + gates[:, grp + g:grp + g + 1] * o_s
               + gates[:, 2 * grp + g:2 * grp + g + 1] * o_w)
        o_ref[:, cols] = out.astype(o_ref.dtype)


def _attend_prompt(q, kc, vc, kv, gates, slopes, *, batch, seq):
    kvh = N_KV_HEADS
    grp = N_HEADS // kvh
    hd = LANES
    tq = _tile(seq, 256)
    tk = _tile(seq, 512)
    nq = seq // tq
    n_ch = kc.shape[2]
    n_blk = -(-seq // SEL_BLOCK)
    wl = min(tq + WINDOW, seq)
    kv3 = kv.reshape(batch, seq, kv.shape[1])
    row = lambda b, h, i: (b * nq + i, h)
    kv_spec = lambda cb: pl.BlockSpec((None, seq, hd), lambda b, h, i: (b, 0, cb * kvh + h))
    cmp_spec = pl.BlockSpec((None, None, n_ch, hd), lambda b, h, i: (b, h, 0, 0))
    blk = (8 * seq * hd * 4 + 4 * tq * grp * hd * 2 + 3 * grp * tq * hd * 4 + 10 * tq * max(tk, wl) * 4
           + 4 * n_ch * hd * 2)
    return pl.pallas_call(
        functools.partial(_attn_prompt_kernel, grp=grp, seq=seq, tq=tq, tk=tk, n_cmp=n_ch - 1, n_blk=n_blk,
                          n_top=min(N_SEL, n_blk), wl=wl),
        out_shape=jax.ShapeDtypeStruct((batch * seq, N_HEADS * hd), BF16),
        grid=(batch, kvh, nq),
        in_specs=[pl.BlockSpec(memory_space=pltpu.SMEM),
                  pl.BlockSpec((tq, grp * hd), row), cmp_spec, cmp_spec,
                  kv_spec(2), kv_spec(3), kv_spec(4), kv_spec(5),
                  pl.BlockSpec((tq, LANES), row)],
        out_specs=pl.BlockSpec((tq, grp * hd), row),
        scratch_shapes=[pltpu.VMEM((grp, tq, 1), F32), pltpu.VMEM((grp, tq, 2 * hd), F32),
                        pltpu.VMEM((grp, tq, hd), F32), pltpu.VMEM((grp, tq, 2 * hd), BF16)],
        compiler_params=_params(("arbitrary", "arbitrary", "arbitrary"), blk),
        name="nsa_attend_prompt",
    )(slopes, q, kc, vc, kv3, kv3, kv3, kv3, gates)


def _attn_sample_kernel(pt_ref, slope_ref, q_ref, kc_ref, vc_ref, *rest, n_pages, kvh, grp, dec_seq, past, n_cmp,
                        n_blk, n_top, nb_pad):
    pages = rest[:n_pages]
    new_ref, win_ref, gt_ref, o_ref, m_s, l_s, acc_s, oc_s, sel_s = rest[n_pages:]
    c = pl.program_id(1)
    rows = q_ref.shape[1]
    hd = LANES
    rpt = 2 * kvh
    row = _iota((rows, 1), 0)
    t_row = row // grp
    g_row = row % grp
    tpos = past + t_row
    tk = n_pages * PAGE_SIZE
    k0 = c * tk
    expand = jnp.where(_iota((nb_pad, tk), 0) == (k0 + _iota((nb_pad, tk), 1)) // SEL_BLOCK, 1.0, 0.0).astype(BF16)
    d_past = tpos - (k0 + _iota((1, tk), 1))
    d_past_f = d_past.astype(F32)

    def head_rows(ref, j, hk, n):
        return ref[pl.ds(j * kvh + hk, n, stride=rpt), :].astype(BF16)

    for hk in range(kvh):
        q = q_ref[hk]
        slope = jnp.zeros((rows, 1), F32)
        for g in range(grp):
            slope = jnp.where(g_row == g, slope_ref[hk * grp + g], slope)

        def online_update(s, v, hk=hk):
            m_old = m_s[hk]
            m_new = jnp.maximum(m_old, jnp.max(s, axis=-1, keepdims=True))
            alpha = jnp.exp2(m_old - m_new)
            p = jnp.exp2(s - m_new)
            l_s[hk] = alpha * l_s[hk] + jnp.sum(p, axis=-1, keepdims=True)
            acc_s[hk] = alpha * acc_s[hk] + _dot(p.astype(BF16), v)
            m_s[hk] = m_new

        @pl.when(c == 0)
        def _(hk=hk, q=q, slope=slope):
            kc = kc_ref[hk]
            n_ch = kc.shape[0]
            p = _cmp_scores(q, kc, slope, tpos, n_cmp)
            oc_s[hk] = _dot(p.astype(BF16), vc_ref[hk])
            ov = _overlap(n_ch, nb_pad)
            p1, p2, p3 = _split3(p)
            per_head = _dot(p1, ov) + _dot(p2, ov) + _dot(p3, ov)
            same_tok = jnp.where(_iota((rows, rows), 0) // grp == _iota((rows, rows), 1) // grp, 1.0, 0.0)
            same_tok = same_tok.astype(BF16)
            a1, a2, a3 = _split3(per_head)
            imp = _force_and_mask(_dot(same_tok, a1) + _dot(same_tok, a2) + _dot(same_tok, a3), tpos)
            blk_f = _iota((1, nb_pad), 1).astype(F32)
            taken = jnp.where(blk_f >= n_blk, 1.0, 0.0) + jnp.zeros((rows, nb_pad), F32)
            sel = jnp.zeros((rows, nb_pad), F32)
            for _ in range(n_top):
                avail = jnp.where(taken > 0.5, -jnp.inf, imp)
                best = jnp.max(avail, axis=-1, keepdims=True)
                cand = jnp.where(taken > 0.5, 0.0, jnp.where(avail == best, 1.0, 0.0))
                idx = jnp.min(jnp.where(cand > 0.5, blk_f, float(nb_pad)), axis=-1, keepdims=True)
                pick = blk_f == idx
                taken = jnp.where(pick, 1.0, taken)
                sel = jnp.where(pick, 1.0, sel)
            sel_s[hk] = sel.astype(BF16)
            m_s[hk] = jnp.full((rows, 1), MASKED, F32)
            l_s[hk] = jnp.zeros((rows, 1), F32)
            acc_s[hk] = jnp.zeros((rows, hd), F32)

        kk = jnp.concatenate([head_rows(pg, 0, hk, PAGE_SIZE) for pg in pages], axis=0)
        vv = jnp.concatenate([head_rows(pg, 1, hk, PAGE_SIZE) for pg in pages], axis=0)
        picked = _dot(sel_s[hk], expand)
        ok = jnp.where(d_past >= 0, picked, 0.0) > 0.5
        online_update(jnp.where(ok, _dot_nt(q, kk) - slope * d_past_f, MASKED), vv)

        @pl.when(c == pl.num_programs(1) - 1)
        def _(hk=hk, q=q, slope=slope, online_update=online_update):
            new_cols = lambda cb: slice((cb * kvh + hk) * hd, (cb * kvh + hk + 1) * hd)
            t_new = _iota((1, new_ref.shape[0]), 1)
            dn = t_row - t_new
            causal_new = jnp.where(dn >= 0, jnp.where(t_new < dec_seq, 1.0, 0.0), 0.0)
            dnf = dn.astype(F32)
            online_update(jnp.where(causal_new > 0.5, _dot_nt(q, new_ref[:, new_cols(2)].astype(BF16)) - slope * dnf,
                                    MASKED), new_ref[:, new_cols(3)].astype(BF16))
            o_s = acc_s[hk] / l_s[hk]
            n_buf = win_ref.shape[0] // rpt
            dc = tpos - (past - n_buf + _iota((1, n_buf), 1))
            ok_c = jnp.where(dc >= 0, jnp.where(dc < WINDOW, 1.0, 0.0), 0.0) > 0.5
            s_c = jnp.where(ok_c, _dot_nt(q, head_rows(win_ref, 0, hk, n_buf)) - slope * dc.astype(F32), MASKED)
            ok_n = jnp.where(dn < WINDOW, causal_new, 0.0) > 0.5
            s_n = jnp.where(ok_n, _dot_nt(q, new_ref[:, new_cols(4)].astype(BF16)) - slope * dnf, MASKED)
            m = jnp.maximum(jnp.max(s_c, axis=-1, keepdims=True), jnp.max(s_n, axis=-1, keepdims=True))
            e_c = jnp.exp2(s_c - m)
            e_n = jnp.exp2(s_n - m)
            den = jnp.sum(e_c, axis=-1, keepdims=True) + jnp.sum(e_n, axis=-1, keepdims=True)
            o_w = (_dot(e_c.astype(BF16), head_rows(win_ref, 1, hk, n_buf))
                   + _dot(e_n.astype(BF16), new_ref[:, new_cols(5)].astype(BF16))) / den
            gates = gt_ref[hk]
            o_ref[hk] = (gates[:, 0:1] * oc_s[hk] + gates[:, 1:2] * o_s + gates[:, 2:3] * o_w).astype(o_ref.dtype)


def _attend_sample(q_rows, kc, vc, cache_sel, page_table, kv_new, cache_win, gate_rows, slopes, *, layer, past,
                   dec_seq):
    batch, kvh, rows, hd = q_rows.shape
    grp = N_HEADS // kvh
    assert past % SEL_BLOCK == 0 and dec_seq <= SEL_BLOCK and past % PAGE_SIZE == 0
    pages_per_batch = past // PAGE_SIZE
    n_pages = _tile(pages_per_batch, 8)
    n_ch = kc.shape[2]
    n_blk = -(-(past + dec_seq) // SEL_BLOCK)
    nb_pad = -(-n_blk // LANES) * LANES
    n_layers, n_phys = cache_sel.shape[:2]
    n_buf = cache_win.shape[2]
    rpt = 2 * kvh
    sel_rows = cache_sel.reshape(n_layers, n_phys, PAGE_SIZE * rpt, hd)
    win_rows = cache_win.reshape(n_layers, batch, n_buf * rpt, hd)

    def page_spec(r):
        return pl.BlockSpec((None, None, PAGE_SIZE * rpt, hd),
                            lambda b, c, pt: (layer, pt[b * pages_per_batch + c * n_pages + r], 0, 0))

    per_batch = lambda *shape: pl.BlockSpec((None,) + shape, lambda b, c, pt: (b,) + (0,) * len(shape))
    row_spec = per_batch(kvh, rows, hd)
    tk = n_pages * PAGE_SIZE
    blk = (2 * n_pages * PAGE_SIZE * rpt * hd * 4 + 4 * kvh * n_ch * hd * 2 + 2 * n_buf * rpt * hd * 4
           + 16 * rows * max(tk, n_ch, nb_pad) * 4 + 3 * nb_pad * max(tk, n_ch) * 4)
    return pl.pallas_call(
        functools.partial(_attn_sample_kernel, n_pages=n_pages, kvh=kvh, grp=grp, dec_seq=dec_seq, past=past,
                          n_cmp=n_ch - 1, n_blk=n_blk, n_top=min(N_SEL, n_blk), nb_pad=nb_pad),
        out_shape=jax.ShapeDtypeStruct((batch, kvh, rows, hd), BF16),
        grid_spec=pltpu.PrefetchScalarGridSpec(
            num_scalar_prefetch=1,
            grid=(batch, pages_per_batch // n_pages),
            in_specs=[pl.BlockSpec(memory_space=pltpu.SMEM), row_spec, per_batch(kvh, n_ch, hd),
                      per_batch(kvh, n_ch, hd)]
            + [page_spec(r) for r in range(n_pages)]
            + [per_batch(kv_new.shape[1], kv_new.shape[2]),
               pl.BlockSpec((None, None, n_buf * rpt, hd), lambda b, c, pt: (layer, b, 0, 0)), row_spec],
            out_specs=row_spec,
            scratch_shapes=[pltpu.VMEM((kvh, rows, 1), F32), pltpu.VMEM((kvh, rows, 1), F32),
                            pltpu.VMEM((kvh, rows, hd), F32), pltpu.VMEM((kvh, rows, hd), F32),
                            pltpu.VMEM((kvh, rows, nb_pad), BF16)]),
        compiler_params=_params(("arbitrary", "arbitrary"), blk),
        name="nsa_attend_sample",
    )(page_table, slopes, q_rows, kc, vc, *([sel_rows] * n_pages), kv_new, win_rows, gate_rows)


def _nsa_layer(hp, hs, xp, xs, g1p, g1s, caches, page_table, weights, slopes, *, layer, batch, seq, dec_batch,
               dec_seq):
    (w_in, w_out, q_g, k_g, pe, w1, b1, w2, b2) = weights
    cache_cmp, cache_sel, cache_win = caches
    d = xp.shape[-1]
    kvh = N_KV_HEADS
    grp = N_HEADS // kvh
    hd = LANES
    cols = 2 * kvh * hd
    n_s = dec_batch * dec_seq
    past = page_table.shape[1] * PAGE_SIZE
    assert seq % PAGE_SIZE == 0 and past % CMP_STRIDE == 0 and dec_seq < CMP_STRIDE
    cmp_w = (w1, b1, w2, b2, pe, k_g[0])

    q, kv, gates = _nsa_project(hp, w_in, q_g, k_g)
    n_pages_p = batch * seq // PAGE_SIZE
    kc, vc = _compress(kv.reshape(n_pages_p, PAGE_SIZE, 3 * cols), jnp.arange(n_pages_p, dtype=I32), batch, *cmp_w)
    o = _attend_prompt(q, kc, vc, kv, gates, slopes, batch=batch, seq=seq)
    xp = _mm(o, w_out, col0=0, n=d, epi="res", extra=(xp.reshape(batch * seq, d), g1p), rows_per_batch=seq,
             name="nsa_out").reshape(batch, seq, d)
    kv_p = kv.reshape(batch, seq, 3, 2, kvh, hd)
    new_p = (kv_p[:, :, 0], kv_p[:, :, 1], kv_p[:, seq - min(WINDOW, seq):, 2])

    qs, kvs, gs = _nsa_project(hs, w_in, q_g, k_g)
    pt = page_table.reshape(-1)
    kcs, vcs = _compress(cache_cmp, pt, dec_batch, *cmp_w, layer=layer)
    q_rows = jnp.transpose(qs.reshape(dec_batch, dec_seq, kvh, grp, hd), (0, 2, 1, 3, 4))
    q_rows = q_rows.reshape(dec_batch, kvh, dec_seq * grp, hd)
    gate_rows = gs.reshape(dec_batch, dec_seq, kvh, LANES)[..., :3 * grp].reshape(dec_batch, dec_seq, kvh, 3, grp)
    gate_rows = jnp.transpose(gate_rows, (0, 2, 1, 4, 3)).reshape(dec_batch, kvh, dec_seq * grp, 3)
    gate_rows = jnp.pad(gate_rows, ((0, 0), (0, 0), (0, 0), (0, LANES - 3)))
    kv_new = jnp.pad(kvs.reshape(dec_batch, dec_seq, 3 * cols), ((0, 0), (0, SUBLANES - dec_seq), (0, 0)))
    n_buf = cache_win.shape[2]
    o_rows = _attend_sample(q_rows, kcs, vcs, cache_sel, pt, kv_new, cache_win, gate_rows, slopes, layer=layer,
                            past=past, dec_seq=dec_seq)
    o_s = jnp.transpose(o_rows.reshape(dec_batch, kvh, dec_seq, grp, hd), (0, 2, 1, 3, 4)).reshape(n_s, N_HEADS * hd)
    xs = _mm(o_s, w_out, col0=0, n=d, epi="res", extra=(xs.reshape(n_s, d), g1s[0]),
             name="nsa_out").reshape(1, n_s, d)
    kv_s = kvs.reshape(dec_batch, dec_seq, 3, 2, kvh, hd)
    win_s = jnp.concatenate([cache_win[layer], kv_s[:, :, 2]], axis=1)[:, -n_buf:]
    new_s = (kv_s[:, :, 0], kv_s[:, :, 1], win_s)
    return xp, xs, new_p, new_s


def kernel(x_prompt, x_sample, c_prompt, c_sample, state_conv, cache_cmp_kv, cache_sel_kv, cache_win_kv,
           page_table, w_mod, b_mod, norm_g, conv_w_in, conv_w, conv_w_out, nsa_w_in, nsa_w_out, q_norm_g,
           k_norm_g, cmp_pe, cmp_w1, cmp_b1, cmp_w2, cmp_b2, router_w, router_b, moe_w_gu, moe_b_gu,
           moe_w_down, moe_b_down):
    batch, seq, d = x_prompt.shape
    dec_batch, dec_seq, _ = x_sample.shape
    depth = w_mod.shape[0]
    n_s = dec_batch * dec_seq
    assert dec_seq >= conv_w.shape[1] - 1

    c_all = jnp.concatenate([c_prompt, c_sample], axis=0)
    c_all = jnp.pad(c_all, ((0, -c_all.shape[0] % SUBLANES), (0, 0)))
    mod = _adaln(c_all, w_mod, b_mod)
    slope2 = jnp.exp2(-8.0 * jnp.arange(1, N_HEADS + 1, dtype=F32) / N_HEADS) * LOG2_E
    slopes = jnp.concatenate([slope2] + [p.astype(F32) for p in _split3(slope2)])

    xp = x_prompt
    xs = x_sample.reshape(1, n_s, d)
    conv_p, conv_s, cmp_p, cmp_s, sel_p, sel_s, win_p, win_s = [], [], [], [], [], [], [], []
    for i in range(depth):
        j = i // 2
        sh1p, sc1p, g1p, sh2p, sc2p, g2p = [m[:, None, :] for m in jnp.split(mod[i, :batch], 6, axis=-1)]
        sh1s, sc1s, g1s, sh2s, sc2s, g2s = [jnp.repeat(m, dec_seq, axis=0)[None]
                                            for m in jnp.split(mod[i, batch:batch + dec_batch], 6, axis=-1)]
        hp = _norm_mod(xp, norm_g[i, 0], sc1p, sh1p).reshape(batch * seq, d)
        hs = _norm_mod(xs, norm_g[i, 0], sc1s, sh1s).reshape(n_s, d)
        if i % 2 == 0:
            a_p, state_p = _conv_in(hp, conv_w_in[j], conv_w[j], batch=batch, seq=seq)
            a_s, v_s = _conv_in(hs, conv_w_in[j], conv_w[j], batch=dec_batch, seq=dec_seq,
                                prev=_conv_prev_rows(state_conv[j], dec_seq))
            xp = _mm(a_p, conv_w_out[j], col0=0, n=d, epi="res", extra=(xp.reshape(batch * seq, d), g1p),
                     rows_per_batch=seq, name="conv_out").reshape(batch, seq, d)
            xs = _mm(a_s, conv_w_out[j], col0=0, n=d, epi="res", extra=(xs.reshape(n_s, d), g1s[0]),
                     name="conv_out").reshape(1, n_s, d)
            conv_p.append(state_p)
            conv_s.append(v_s.reshape(dec_batch, dec_seq, d)[:, dec_seq - 2:])
        else:
            weights = (nsa_w_in[j], nsa_w_out[j], q_norm_g[j], k_norm_g[j], cmp_pe[j], cmp_w1[j], cmp_b1[j],
                       cmp_w2[j], cmp_b2[j])
            xp, xs, new_p, new_s = _nsa_layer(
                hp, hs, xp, xs, g1p, g1s, (cache_cmp_kv, cache_sel_kv, cache_win_kv), page_table,
                weights, slopes, layer=j, batch=batch, seq=seq, dec_batch=dec_batch, dec_seq=dec_seq)
            cmp_p.append(new_p[0])
            sel_p.append(new_p[1])
            win_p.append(new_p[2])
            cmp_s.append(new_s[0])
            sel_s.append(new_s[1])
            win_s.append(new_s[2])
        xp, xs = _moe(xp, xs, norm_g[i, 1], (sc2p, sh2p, g2p), (sc2s, sh2s, g2s), router_w[i], router_b[i],
                      moe_w_gu, moe_b_gu, moe_w_down, moe_b_down, layer=i)
    return (xp, xs.reshape(dec_batch, dec_seq, d), jnp.stack(conv_p), jnp.stack(conv_s), jnp.stack(cmp_p),
            jnp.stack(cmp_s), jnp.stack(sel_p), jnp.stack(sel_s), jnp.stack(win_p), jnp.stack(win_s))
```

```python
import functools

import jax
import jax.numpy as jnp
from jax import lax
from jax.experimental import pallas as pl
from jax.experimental.pallas import tpu as pltpu

F32 = jnp.float32
BF16 = jnp.bfloat16
I32 = jnp.int32
U32 = jnp.uint32

N_HEADS = 32
N_KV_HEADS = 4
CMP_BLOCK = 32
CMP_STRIDE = 16
SEL_BLOCK = 64
N_SEL = 16
WINDOW = 512
PAGE_SIZE = 128
TOP_K = 4
SWIGLU_LIMIT = 7.0
SWIGLU_ALPHA = 1.702
EPS = 1e-6
FORCE_SCORE = 1e4
MASKED = -1e30
LOG2_E = 1.4426950408889634
MOE_SUB_ROWS = 272
MOE_SUBS_PER_SUPERBLOCK = 4

LANES = 128
SUBLANES = 8
VMEM_PHYSICAL_BYTES = 64 * 1024 * 1024
VMEM_CAP_BYTES = VMEM_PHYSICAL_BYTES - 6 * 1024 * 1024


def _vmem_limit(block_bytes):
    return int(min(VMEM_CAP_BYTES, block_bytes * 5 // 4 + (4 << 20)))


def _params(sem, block_bytes):
    return pltpu.CompilerParams(dimension_semantics=sem, vmem_limit_bytes=_vmem_limit(block_bytes))


def _tile(n, pref):
    if n <= pref:
        return n
    t = pref
    while n % t:
        t //= 2
    return t


def _dot(a, b):
    return jnp.dot(a, b, preferred_element_type=F32)


def _dot_nt(a, b):
    return lax.dot_general(a, b, (((1,), (1,)), ((), ())), preferred_element_type=F32)


def _split3(x):
    hi = x.astype(BF16)
    r = x - hi.astype(F32)
    mid = r.astype(BF16)
    lo = (r - mid.astype(F32)).astype(BF16)
    return hi, mid, lo


def _iota(shape, dim):
    return lax.broadcasted_iota(I32, shape, dim)


def _rms(a):
    return a * lax.rsqrt(jnp.mean(a * a, axis=-1, keepdims=True) + EPS)


def _adaln_kernel(c_ref, w_ref, b_ref, o_ref, *, kc):
    c = c_ref[...]
    a = (c * jax.nn.sigmoid(c)).astype(BF16)
    acc = jnp.zeros(o_ref.shape, F32)
    for k0 in range(0, a.shape[1], kc):
        acc = acc + _dot(a[:, k0:k0 + kc], w_ref[k0:k0 + kc, :].astype(BF16))
    o_ref[...] = acc + b_ref[...]


def _adaln(c_all, w_mod, b_mod):
    n_layers, d, n6 = w_mod.shape
    r = c_all.shape[0]
    tn = _tile(n6, 1024)
    kc = _tile(d, 1024)
    blk = 2 * d * tn * 4 + d * tn * 2 + r * d * 4
    return pl.pallas_call(
        functools.partial(_adaln_kernel, kc=kc),
        out_shape=jax.ShapeDtypeStruct((n_layers, r, n6), F32),
        grid=(n_layers, n6 // tn),
        in_specs=[
            pl.BlockSpec((r, d), lambda l, j: (0, 0)),
            pl.BlockSpec((None, d, tn), lambda l, j: (l, 0, j)),
            pl.BlockSpec((None, 1, tn), lambda l, j: (l, 0, j)),
        ],
        out_specs=pl.BlockSpec((None, r, tn), lambda l, j: (l, 0, j)),
        compiler_params=_params(("arbitrary", "arbitrary"), blk),
        name="adaln_mod",
    )(c_all, w_mod, b_mod.reshape(n_layers, 1, n6))


def _modulated(x_ref, g_ref, sc_ref, sh_ref):
    return _rms(x_ref[...]) * g_ref[...] * (1.0 + sc_ref[...]) + sh_ref[...]


def _norm_mod_kernel(x_ref, g_ref, sc_ref, sh_ref, h_ref):
    h_ref[...] = _modulated(x_ref, g_ref, sc_ref, sh_ref).astype(h_ref.dtype)


def _mod_spec(mod, tm, d):
    if mod.shape[1] == 1:
        return pl.BlockSpec((None, 1, d), lambda b, i: (b, 0, 0))
    return pl.BlockSpec((None, tm, d), lambda b, i: (b, i, 0))


def _norm_mod(x, g, scale, shift):
    b, t, d = x.shape
    tm = _tile(t, 512)
    blk = 2 * tm * d * (4 + 2) + 6 * d * 4 + 2 * tm * d * 4
    return pl.pallas_call(
        _norm_mod_kernel,
        out_shape=jax.ShapeDtypeStruct((b, t, d), BF16),
        grid=(b, t // tm),
        in_specs=[
            pl.BlockSpec((None, tm, d), lambda b_, i: (b_, i, 0)),
            pl.BlockSpec((1, d), lambda b_, i: (0, 0)),
            _mod_spec(scale, tm, d),
            _mod_spec(shift, tm, d),
        ],
        out_specs=pl.BlockSpec((None, tm, d), lambda b_, i: (b_, i, 0)),
        compiler_params=_params(("arbitrary", "arbitrary"), blk),
        name="norm_mod",
    )(x, g.reshape(1, d), scale, shift)


def _norm_router_kernel(x_ref, g_ref, sc_ref, sh_ref, wr_ref, br_ref, hp_ref, ti_ref, gt_ref):
    h = _modulated(x_ref, g_ref, sc_ref, sh_ref)
    tm, d = h.shape
    dh = d // 2
    lo = pltpu.bitcast(h[:, :dh].astype(BF16).astype(F32), U32)
    hi = pltpu.bitcast(h[:, dh:].astype(BF16).astype(F32), U32)
    hp_ref[...] = (lo >> 16) | hi

    h1, h2, h3 = _split3(h)
    w1, w2, w3 = _split3(wr_ref[...])
    logits = (_dot(h1, w1) + (_dot(h1, w2) + _dot(h2, w1))
              + (_dot(h2, w2) + _dot(h1, w3) + _dot(h3, w1))) + br_ref[...]
    n_exp = logits.shape[1]
    lane = _iota(logits.shape, 1).astype(F32)
    work = logits
    vals, idxs = [], []
    for _ in range(TOP_K):
        m = jnp.max(work, axis=-1, keepdims=True)
        idx = jnp.min(jnp.where(work == m, lane, float(n_exp)), axis=-1, keepdims=True)
        vals.append(m)
        idxs.append(idx)
        work = jnp.where(lane == idx, -jnp.inf, work)
    es = [jnp.exp(v - vals[0]) for v in vals]
    den = es[0]
    for e in es[1:]:
        den = den + e
    lane_o = _iota((tm, LANES), 1)
    ti = jnp.zeros((tm, LANES), F32)
    gt = jnp.zeros((tm, LANES), F32)
    for k in range(TOP_K):
        ti = jnp.where(lane_o == k, idxs[k], ti)
        gt = jnp.where(lane_o == k, es[k] / den, gt)
    ti_ref[...] = ti.astype(I32)
    gt_ref[...] = gt


def _norm_router(x, g, scale, shift, w_router, b_router):
    b, t, d = x.shape
    n_exp = w_router.shape[1]
    tm = _tile(t, 256)
    blk = 2 * tm * d * 4 + 2 * tm * d * 2 + 8 * tm * d * 4 + 2 * d * n_exp * 4
    row = lambda b_, i: (b_, i, 0)
    return pl.pallas_call(
        _norm_router_kernel,
        out_shape=(jax.ShapeDtypeStruct((b, t, d // 2), U32),
                   jax.ShapeDtypeStruct((b, t, LANES), I32),
                   jax.ShapeDtypeStruct((b, t, LANES), F32)),
        grid=(b, t // tm),
        in_specs=[
            pl.BlockSpec((None, tm, d), row),
            pl.BlockSpec((1, d), lambda b_, i: (0, 0)),
            _mod_spec(scale, tm, d),
            _mod_spec(shift, tm, d),
            pl.BlockSpec((d, n_exp), lambda b_, i: (0, 0)),
            pl.BlockSpec((1, n_exp), lambda b_, i: (0, 0)),
        ],
        out_specs=(pl.BlockSpec((None, tm, d // 2), row),
                   pl.BlockSpec((None, tm, LANES), row),
                   pl.BlockSpec((None, tm, LANES), row)),
        compiler_params=_params(("arbitrary", "arbitrary"), blk),
        name="norm_router",
    )(x, g.reshape(1, d), scale, shift, w_router, b_router.reshape(1, n_exp))


def _mm_kernel(a_ref, w_ref, *rest, epi, head_dim, scale):
    *ins, o_ref, wb = rest

    @pl.when(pl.program_id(1) == 0)
    def _():
        wb[...] = w_ref[...].astype(BF16)

    acc = _dot(a_ref[...], wb[...])
    tn = acc.shape[1]
    if epi == "res":
        x_ref, g_ref = ins
        o_ref[...] = x_ref[...] + g_ref[...] * acc
    elif epi == "qnorm":
        (gq_ref,) = ins
        for c in range(tn // head_dim):
            a = acc[:, c * head_dim:(c + 1) * head_dim]
            o_ref[:, c * head_dim:(c + 1) * head_dim] = (_rms(a) * gq_ref[...] * scale).astype(o_ref.dtype)
    elif epi == "kvnorm":
        fl_ref, gk_ref = ins
        for c in range(tn // head_dim):
            sl = slice(c * head_dim, (c + 1) * head_dim)
            a = acc[:, sl]
            o_ref[:, sl] = jnp.where(fl_ref[:, sl] > 0.5, _rms(a) * gk_ref[:, sl], a)
    elif epi == "sigmoid":
        o_ref[...] = jax.nn.sigmoid(acc)
    else:
        raise ValueError(epi)


def _mm(a, w, *, col0, n, epi, extra=(), rows_per_batch=None, out_dtype=F32, head_dim=LANES, scale=1.0,
        name="mm"):
    m, k = a.shape
    tm = _tile(m, 1024)
    tn = _tile(n, 512)
    assert col0 % tn == 0
    j0 = col0 // tn
    in_specs = [pl.BlockSpec((tm, k), lambda j, i: (i, 0)),
                pl.BlockSpec((k, tn), lambda j, i: (0, j + j0))]
    operands = [a, w]
    if epi == "res":
        x, g = extra
        in_specs.append(pl.BlockSpec((tm, tn), lambda j, i: (i, j)))
        if g.ndim == 3:
            assert rows_per_batch % tm == 0
            in_specs.append(pl.BlockSpec((None, 1, tn), lambda j, i: (i * tm // rows_per_batch, 0, j)))
        else:
            in_specs.append(pl.BlockSpec((tm, tn), lambda j, i: (i, j)))
        operands += [x, g]
    elif epi == "qnorm":
        in_specs.append(pl.BlockSpec((1, head_dim), lambda j, i: (0, 0)))
        operands += list(extra)
    elif epi == "kvnorm":
        in_specs += [pl.BlockSpec((1, tn), lambda j, i: (0, j))] * 2
        operands += list(extra)
    blk = 2 * tm * k * 2 + 2 * k * tn * 4 + k * tn * 2 + 6 * tm * tn * 4
    return pl.pallas_call(
        functools.partial(_mm_kernel, epi=epi, head_dim=head_dim, scale=scale),
        out_shape=jax.ShapeDtypeStruct((m, n), out_dtype),
        grid=(n // tn, m // tm),
        in_specs=in_specs,
        out_specs=pl.BlockSpec((tm, tn), lambda j, i: (i, j)),
        scratch_shapes=[pltpu.VMEM((k, tn), BF16)],
        compiler_params=_params(("arbitrary", "arbitrary"), blk),
        name=name,
    )(*operands)


def _conv_in_kernel(a_ref, wb_ref, wc_ref, wu_ref, cw_ref, *rest, tiles_per_batch, seq, per_token_prev):
    if per_token_prev:
        p1_ref, p2_ref, o_ref, v_ref, wbuf, ext = rest
    else:
        o_ref, st_ref, wbuf, ext = rest
    i = pl.program_id(1)

    @pl.when(i == 0)
    def _():
        wbuf[0] = wb_ref[...].astype(BF16)
        wbuf[1] = wc_ref[...].astype(BF16)
        wbuf[2] = wu_ref[...].astype(BF16)

    a = a_ref[...]
    b_gate = _dot(a, wbuf[0])
    v = _dot(a, wbuf[1]) * _dot(a, wbuf[2])
    tm, tn = v.shape

    if per_token_prev:
        ext[0:SUBLANES, :] = jnp.zeros((SUBLANES, tn), F32)
    else:
        @pl.when(i % tiles_per_batch == 0)
        def _():
            ext[0:SUBLANES, :] = jnp.zeros((SUBLANES, tn), F32)

    ext[SUBLANES:SUBLANES + tm, :] = v
    s1 = ext[SUBLANES - 1:SUBLANES - 1 + tm, :]
    s2 = ext[SUBLANES - 2:SUBLANES - 2 + tm, :]
    if per_token_prev:
        tpos = _iota((tm, 1), 0) % seq
        s1 = jnp.where(tpos >= 1, s1, p1_ref[...])
        s2 = jnp.where(tpos >= 2, s2, p2_ref[...])
    cw = cw_ref[...]
    conv = s2 * cw[0:1, :] + s1 * cw[1:2, :] + v * cw[2:3, :]
    o_ref[...] = (b_gate * conv).astype(o_ref.dtype)

    if per_token_prev:
        v_ref[...] = v
    else:
        ext[0:SUBLANES, :] = ext[tm:tm + SUBLANES, :]

        @pl.when(i % tiles_per_batch == tiles_per_batch - 1)
        def _():
            st_ref[...] = ext[tm + SUBLANES - 2:tm + SUBLANES, :]


def _conv_in(h, w_in, conv_w, *, batch, seq, prev=None):
    m, d = h.shape
    tn = _tile(d, 256)
    nd = d // tn
    per_token_prev = prev is not None
    tm = m if per_token_prev else _tile(seq, 512)
    tiles_per_batch = max(seq // tm, 1)
    in_specs = [pl.BlockSpec((tm, d), lambda j, i: (i, 0)),
                pl.BlockSpec((d, tn), lambda j, i: (0, j)),
                pl.BlockSpec((d, tn), lambda j, i: (0, j + nd)),
                pl.BlockSpec((d, tn), lambda j, i: (0, j + 2 * nd)),
                pl.BlockSpec((3, tn), lambda j, i: (0, j))]
    operands = [h, w_in, w_in, w_in, conv_w]
    tile_spec = pl.BlockSpec((tm, tn), lambda j, i: (i, j))
    if per_token_prev:
        in_specs += [tile_spec, tile_spec]
        operands += list(prev)
        out_shape = (jax.ShapeDtypeStruct((m, d), BF16), jax.ShapeDtypeStruct((m, d), F32))
        out_specs = (tile_spec, tile_spec)
    else:
        out_shape = (jax.ShapeDtypeStruct((m, d), BF16), jax.ShapeDtypeStruct((batch, 2, d), F32))
        out_specs = (tile_spec, pl.BlockSpec((None, 2, tn), lambda j, i: (i // tiles_per_batch, 0, j)))
    blk = 2 * tm * d * 2 + 6 * d * tn * 4 + 3 * d * tn * 2 + 10 * tm * tn * 4
    return pl.pallas_call(
        functools.partial(_conv_in_kernel, tiles_per_batch=tiles_per_batch, seq=seq,
                          per_token_prev=per_token_prev),
        out_shape=out_shape,
        grid=(nd, m // tm),
        in_specs=in_specs,
        out_specs=out_specs,
        scratch_shapes=[pltpu.VMEM((3, d, tn), BF16), pltpu.VMEM((tm + 2 * SUBLANES, tn), F32)],
        compiler_params=_params(("arbitrary", "arbitrary"), blk),
        name="conv_in",
    )(*operands)


def _conv_prev_rows(state, seq):
    b, _, d = state.shape
    zeros = jnp.zeros((b, seq, d), state.dtype)
    p1 = zeros.at[:, 0].set(state[:, 1])
    p2 = zeros.at[:, 0].set(state[:, 0]).at[:, 1].set(state[:, 1])
    return p1.reshape(b * seq, d), p2.reshape(b * seq, d)


def _plan_kernel(ti_ref, pos_ref, cnt_ref, carry):
    @pl.when(pl.program_id(0) == 0)
    def _():
        carry[...] = jnp.zeros(carry.shape, F32)

    ti = ti_ref[...]
    tm = ti.shape[0]
    e_iota = _iota((tm, LANES), 1)
    onehots = [jnp.where(ti[:, k:k + 1] == e_iota, 1.0, 0.0) for k in range(TOP_K)]
    hits = onehots[0]
    for oh in onehots[1:]:
        hits = hits + oh
    strictly_lower = jnp.where(_iota((tm, tm), 0) > _iota((tm, tm), 1), 1.0, 0.0).astype(BF16)
    before = _dot(strictly_lower, hits.astype(BF16)) + carry[...]
    out = jnp.zeros((tm, LANES), F32)
    for k in range(TOP_K):
        out = jnp.where(e_iota == k, jnp.sum(onehots[k] * before, axis=-1, keepdims=True), out)
    pos_ref[...] = out.astype(I32)
    carry[...] = carry[...] + jnp.sum(hits, axis=0, keepdims=True)
    cnt_ref[...] = carry[...]


def _plan(topi):
    n = topi.shape[0]
    tm = _tile(n, 256)
    return pl.pallas_call(
        _plan_kernel,
        out_shape=(jax.ShapeDtypeStruct((n, LANES), I32), jax.ShapeDtypeStruct((1, LANES), F32)),
        grid=(n // tm,),
        in_specs=[pl.BlockSpec((tm, LANES), lambda i: (i, 0))],
        out_specs=(pl.BlockSpec((tm, LANES), lambda i: (i, 0)), pl.BlockSpec((1, LANES), lambda i: (0, 0))),
        scratch_shapes=[pltpu.VMEM((1, LANES), F32)],
        compiler_params=_params(("arbitrary",), 8 * tm * LANES * 4 + tm * tm * 8),
        name="moe_plan",
    )(topi)


def _dispatch_tables(topi, n_exp, sub, rmax):
    n = topi.shape[0]
    ids = topi[:, :TOP_K]
    n_pairs = n * TOP_K
    n_pad = -(-n // 256) * 256
    pos, counts = _plan(jnp.pad(topi, ((0, n_pad - n), (0, 0)), constant_values=-1))
    counts = counts[0, :n_exp].astype(I32)
    padded = (counts + sub - 1) // sub * sub
    pad_start = jnp.cumsum(padded) - padded
    dest = pad_start[ids] + pos[:n, :TOP_K]
    n_rows = -(-(n_pairs + n_exp * (sub - 1)) // sub) * sub
    row_tok = jnp.zeros((n_rows,), I32).at[dest.reshape(-1)].set(jnp.arange(n_pairs, dtype=I32) // TOP_K)
    n_sb = (counts + rmax - 1) // rmax
    cum = jnp.cumsum(n_sb)
    total = cum[-1]
    n_sb_max = n_exp + n_pairs // rmax
    s = jnp.arange(n_sb_max, dtype=I32)
    e_of = jnp.minimum(jnp.searchsorted(cum, s, side="right"), n_exp - 1).astype(I32)
    local = s - (cum - n_sb)[e_of]
    active = s < total
    rows = jnp.where(active, jnp.minimum(counts[e_of] - local * rmax, rmax), 0).astype(I32)
    start = jnp.where(active, pad_start[e_of] + local * rmax, 0).astype(I32)
    sb_exp = jnp.where(active, e_of, e_of[jnp.maximum(total - 1, 0)]).astype(I32)
    n_used = jnp.sum(padded).astype(I32).reshape(1)
    return dest.astype(I32), row_tok, sb_exp, start, rows, n_used, n_rows, n_sb_max


def _expert_kernel(exp_ref, start_ref, rows_ref, tok_ref, used_ref,
                   h_hbm, wg_ref, wu_ref, bg_ref, bu_ref, wd_ref, bd_ref, ys_hbm,
                   xbuf, act, accg, accu, ostage, gsem, osem, *, n1, sub, tf, dt):
    sb = pl.program_id(0)
    s = pl.program_id(1)
    rows = rows_ref[sb]
    start = start_ref[sb]
    n_sub = (rows + sub - 1) // sub
    dh = xbuf.shape[1]

    @pl.when(jnp.logical_and(sb == 0, s == 0))
    def _():
        ostage[0:sub, :] = jnp.zeros((sub, dt), F32)
        n_tail = (ys_hbm.shape[0] - used_ref[0]) // sub

        def tail_copy(t, col):
            row0 = pl.multiple_of(used_ref[0] + t * sub, sub)
            return pltpu.make_async_copy(ostage.at[pl.ds(0, sub)],
                                         ys_hbm.at[pl.ds(row0, sub), pl.ds(col * dt, dt)], osem.at[0])

        def issue(t, c):
            for col in range(ys_hbm.shape[1] // dt):
                tail_copy(t, col).start()
            return c

        def drain(t, c):
            for col in range(ys_hbm.shape[1] // dt):
                tail_copy(t, col).wait()
            return c

        lax.fori_loop(0, n_tail, issue, 0)
        lax.fori_loop(0, n_tail, drain, 0)

    rmax = xbuf.shape[0]
    last_tok = tok_ref.shape[0] - 1

    def gather_copy(r):
        tok = tok_ref[jnp.minimum(start + r, last_tok)]
        return pltpu.make_async_copy(h_hbm.at[pl.ds(tok, 1)], xbuf.at[pl.ds(r, 1)], gsem)

    @pl.when(jnp.logical_and(s == 0, rows > 0))
    def _():
        def issue(r, c):
            gather_copy(r).start()
            return c

        def drain(r, c):
            gather_copy(r).wait()
            return c

        lax.fori_loop(0, rmax, issue, 0)
        lax.fori_loop(0, rmax, drain, 0)

    @pl.when(jnp.logical_and(sb == 0, s == 0))
    def _():
        accg[...] = jnp.zeros(accg.shape, F32)
        accu[...] = jnp.zeros(accu.shape, F32)

    kc = min(dh, 512)

    @pl.when(jnp.logical_and(s < n1, rows > 0))
    def _():
        nt = s // 2
        first = s % 2 == 0
        shift = jnp.where(first, 16, 0).astype(U32)
        xk = pltpu.bitcast((xbuf[...] << shift) & jnp.uint32(0xFFFF0000), F32).astype(BF16)
        g = jnp.where(first, 0.0, accg[...])
        u = jnp.where(first, 0.0, accu[...])
        for k0 in range(0, dh, kc):
            g = g + _dot(xk[:, k0:k0 + kc], wg_ref[k0:k0 + kc, :].astype(BF16))
            u = u + _dot(xk[:, k0:k0 + kc], wu_ref[k0:k0 + kc, :].astype(BF16))
        accg[...] = g
        accu[...] = u
        g = jnp.minimum(g + bg_ref[...], SWIGLU_LIMIT)
        u = jnp.clip(u + bu_ref[...], -SWIGLU_LIMIT, SWIGLU_LIMIT)
        act[nt] = (g * jax.nn.sigmoid(SWIGLU_ALPHA * g) * (u + 1.0)).astype(BF16)

    @pl.when(jnp.logical_and(s >= n1, rows > 0))
    def _():
        def out_copy(i, step):
            row0 = pl.multiple_of(start + i * sub, sub)
            col = pl.multiple_of((step - n1) * dt, dt)
            return pltpu.make_async_copy(ostage.at[pl.ds(pl.multiple_of(i * sub, sub), sub)],
                                         ys_hbm.at[pl.ds(row0, sub), pl.ds(col, dt)], osem.at[0])

        def drain_step(step):
            def drain(i, c):
                out_copy(i, step).wait()
                return c

            lax.fori_loop(0, n_sub, drain, 0)

        y = bd_ref[...]
        for f in range(act.shape[0]):
            a = act[f]
            for k0 in range(0, tf, kc):
                y = y + _dot(a[:, k0:k0 + kc], wd_ref[f * tf + k0:f * tf + k0 + kc, :].astype(BF16))

        @pl.when(s > n1)
        def _():
            drain_step(s - 1)

        ostage[...] = y

        def issue(i, c):
            out_copy(i, s).start()
            return c

        lax.fori_loop(0, n_sub, issue, 0)

        @pl.when(s == pl.num_programs(1) - 1)
        def _():
            drain_step(s)


def _experts(h_packed, tables, w_gu, b_gu, w_down, b_down, *, layer, sub, rmax):
    _, row_tok, sb_exp, sb_start, sb_rows, n_used, n_rows, n_sb_max = tables
    n_layers, n_exp, d, f2 = w_gu.shape
    f = f2 // 2
    dh = d // 2
    tf = _tile(f, 512)
    dt = _tile(d, 512)
    n_f = f // tf
    n1, n2 = 2 * n_f, d // dt

    def gu_step(sb, s, rows_ref):
        return jnp.where(rows_ref[sb] > 0, jnp.minimum(s, n1 - 1), n1 - 1)

    def d_idx(sb, s, rows_ref):
        return jnp.where(rows_ref[sb] > 0, jnp.maximum(s - n1, 0), n2 - 1)

    in_specs = [
        pl.BlockSpec(memory_space=pl.ANY),
        pl.BlockSpec((None, None, dh, tf), lambda sb, s, e, st, rw, tk, us: (
            layer, e[sb], gu_step(sb, s, rw) % 2, gu_step(sb, s, rw) // 2)),
        pl.BlockSpec((None, None, dh, tf), lambda sb, s, e, st, rw, tk, us: (
            layer, e[sb], gu_step(sb, s, rw) % 2, n_f + gu_step(sb, s, rw) // 2)),
        pl.BlockSpec((None, None, 1, tf),
                     lambda sb, s, e, st, rw, tk, us: (layer, e[sb], 0, gu_step(sb, s, rw) // 2)),
        pl.BlockSpec((None, None, 1, tf),
                     lambda sb, s, e, st, rw, tk, us: (layer, e[sb], 0, n_f + gu_step(sb, s, rw) // 2)),
        pl.BlockSpec((None, None, f, dt),
                     lambda sb, s, e, st, rw, tk, us: (layer, e[sb], 0, d_idx(sb, s, rw))),
        pl.BlockSpec((None, None, 1, dt),
                     lambda sb, s, e, st, rw, tk, us: (layer, e[sb], 0, d_idx(sb, s, rw))),
    ]
    blk = (rmax * dh * 4 + rmax * f * 2 + 2 * rmax * tf * 4 + 4 * dh * tf * 4 + 2 * f * dt * 4 + rmax * dt * 4
           + rmax * dh * 2 + 4 * rmax * tf * 4)
    return pl.pallas_call(
        functools.partial(_expert_kernel, n1=n1, sub=sub, tf=tf, dt=dt),
        out_shape=jax.ShapeDtypeStruct((n_rows, d), F32),
        grid_spec=pltpu.PrefetchScalarGridSpec(
            num_scalar_prefetch=5,
            grid=(n_sb_max, n1 + n2),
            in_specs=in_specs,
            out_specs=pl.BlockSpec(memory_space=pl.ANY),
            scratch_shapes=[
                pltpu.VMEM((rmax, dh), U32),
                pltpu.VMEM((n_f, rmax, tf), BF16),
                pltpu.VMEM((rmax, tf), F32),
                pltpu.VMEM((rmax, tf), F32),
                pltpu.VMEM((rmax, dt), F32),
                pltpu.SemaphoreType.DMA(()),
                pltpu.SemaphoreType.DMA((2,)),
            ]),
        compiler_params=_params(("arbitrary", "arbitrary"), blk),
        name="moe_experts",
    )(sb_exp, sb_start, sb_rows, row_tok, n_used, h_packed, w_gu, w_gu,
      b_gu.reshape(n_layers, n_exp, 1, f2), b_gu.reshape(n_layers, n_exp, 1, f2), w_down,
      b_down.reshape(n_layers, n_exp, 1, d))


def _combine_kernel(dest_ref, ys_hbm, x_ref, g_ref, gate_ref, o_ref, buf, sem, *, tiles_per_batch):
    tm = x_ref.shape[0]
    tok0 = (pl.program_id(0) * tiles_per_batch + pl.program_id(1)) * tm

    def row_copy(r, k):
        src = dest_ref[(tok0 + r) * TOP_K + k]
        return pltpu.make_async_copy(ys_hbm.at[pl.ds(src, 1)], buf.at[k, pl.ds(r, 1)], sem)

    def issue(r, c):
        for k in range(TOP_K):
            row_copy(r, k).start()
        return c

    def drain(r, c):
        for k in range(TOP_K):
            row_copy(r, k).wait()
        return c

    lax.fori_loop(0, tm, issue, 0)
    lax.fori_loop(0, tm, drain, 0)
    gate = gate_ref[...]
    y = gate[:, 0:1] * buf[0]
    for k in range(1, TOP_K):
        y = y + gate[:, k:k + 1] * buf[k]
    o_ref[...] = x_ref[...] + g_ref[...] * y


def _combine(ys, dest, x, g, gate):
    b, t, d = x.shape
    tm = _tile(t, 128)
    tiles_per_batch = t // tm
    row = lambda b_, i, dref: (b_, i, 0)
    if g.shape[1] == 1:
        g_spec = pl.BlockSpec((None, 1, d), lambda b_, i, dref: (b_, 0, 0))
    else:
        g_spec = pl.BlockSpec((None, tm, d), row)
    blk = TOP_K * tm * d * 4 + 6 * tm * d * 4
    return pl.pallas_call(
        functools.partial(_combine_kernel, tiles_per_batch=tiles_per_batch),
        out_shape=jax.ShapeDtypeStruct((b, t, d), F32),
        grid_spec=pltpu.PrefetchScalarGridSpec(
            num_scalar_prefetch=1,
            grid=(b, tiles_per_batch),
            in_specs=[pl.BlockSpec(memory_space=pl.ANY),
                      pl.BlockSpec((None, tm, d), row),
                      g_spec,
                      pl.BlockSpec((None, tm, LANES), row)],
            out_specs=pl.BlockSpec((None, tm, d), row),
            scratch_shapes=[pltpu.VMEM((TOP_K, tm, d), F32), pltpu.SemaphoreType.DMA(())]),
        compiler_params=_params(("arbitrary", "arbitrary"), blk),
        name="moe_combine",
    )(dest.reshape(-1), ys, x, g, gate)


def _moe(xp, xs, g, mods_p, mods_s, w_router, b_router, w_gu, b_gu, w_down, b_down, *, layer):
    (sc_p, sh_p, g_p), (sc_s, sh_s, g_s) = mods_p, mods_s
    d = xp.shape[-1]
    n_exp = w_router.shape[1]
    f = w_down.shape[2]
    hp, ti_p, gt_p = _norm_router(xp, g, sc_p, sh_p, w_router, b_router)
    hs, ti_s, gt_s = _norm_router(xs, g, sc_s, sh_s, w_router, b_router)
    n_p = xp.shape[0] * xp.shape[1]
    h_all = jnp.concatenate([hp.reshape(n_p, d // 2), hs.reshape(-1, d // 2)], axis=0)
    topi = jnp.concatenate([ti_p.reshape(n_p, LANES), ti_s.reshape(-1, LANES)], axis=0)
    sub = MOE_SUB_ROWS
    rmax = MOE_SUBS_PER_SUPERBLOCK * sub
    tables = _dispatch_tables(topi, n_exp, sub, rmax)
    ys = _experts(h_all, tables, w_gu, b_gu, w_down, b_down, layer=layer, sub=sub, rmax=rmax)
    dest = tables[0]
    xp = _combine(ys, dest[:n_p], xp, g_p, gt_p)
    xs = _combine(ys, dest[n_p:], xs, g_s, gt_s)
    return xp, xs


def _nsa_project(h, w_in, q_g, k_g):
    d = h.shape[1]
    hd = LANES
    kvh = N_KV_HEADS
    grp = N_HEADS // kvh
    q_dim = N_HEADS * hd
    kv_dim = 2 * kvh * hd
    q = _mm(h, w_in, col0=0, n=q_dim, epi="qnorm", extra=(q_g.reshape(1, hd),), out_dtype=BF16,
            head_dim=hd, scale=hd ** -0.5 * LOG2_E, name="nsa_q")
    ones = jnp.ones((kvh * hd,), F32)
    zeros = jnp.zeros((kvh * hd,), F32)
    flags = jnp.concatenate([zeros, zeros, ones, zeros, ones, zeros]).reshape(1, 3 * kv_dim)
    gains = jnp.concatenate([ones, ones, jnp.tile(k_g[1], kvh), ones, jnp.tile(k_g[2], kvh), ones])
    kv = _mm(h, w_in, col0=q_dim, n=3 * kv_dim, epi="kvnorm", extra=(flags, gains.reshape(1, 3 * kv_dim)),
             head_dim=hd, name="nsa_kv")
    w_gate = w_in[:, q_dim + 3 * kv_dim:].reshape(d, 3, kvh, grp)
    w_gate = jnp.transpose(w_gate, (0, 2, 1, 3)).reshape(d, kvh, 3 * grp)
    w_gate = jnp.pad(w_gate, ((0, 0), (0, 0), (0, LANES - 3 * grp))).reshape(d, kvh * LANES)
    gates = _mm(h, w_gate, col0=0, n=kvh * LANES, epi="sigmoid", name="nsa_gates")
    return q, kv, gates


def _cmp_part_kernel(pt_ref, *refs, n_pages, kvh, row_view):
    n_refs = n_pages if row_view else n_pages * 2 * kvh
    pages = refs[:n_refs]
    w_ref, pe_ref, p_out, pe_out, wbf = refs[n_refs:]
    rpt = 2 * kvh
    chunks = PAGE_SIZE // CMP_STRIDE

    def chunk_rows(j, h, s):
        if row_view:
            return [pg[pl.ds(s * rpt + j * kvh + h, chunks, stride=CMP_STRIDE * rpt), :] for pg in pages]
        return [pg[pl.ds(s, chunks, stride=CMP_STRIDE), :] for pg in pages[j * kvh + h::rpt]]

    @pl.when(pl.program_id(0) == 0)
    def _():
        for j in range(2):
            wbf[j] = w_ref[j].astype(BF16)
            pe_out[j] = _dot(pe_ref[j].astype(BF16), wbf[j])

    rows = n_pages * PAGE_SIZE // CMP_STRIDE
    for j in range(2):
        per_head = []
        for h in range(kvh):
            pieces = [jnp.concatenate(chunk_rows(j, h, s), axis=0) for s in range(CMP_STRIDE)]
            per_head.append(jnp.concatenate(pieces, axis=1))
        x = jnp.concatenate(per_head, axis=0).astype(BF16)
        y = _dot(x, wbf[j])
        for h in range(kvh):
            p_out[j, h] = y[h * rows:(h + 1) * rows, :]


def _cmp_out_kernel(pk_ref, pv_ref, b1_ref, pe_ref, w2_ref, b2_ref, kg_ref, kc_ref, vc_ref, shifted):
    n_ch = pk_ref.shape[0]
    hid_dim = pk_ref.shape[1] // 2
    for j, (p_ref, o_ref) in enumerate(((pk_ref, kc_ref), (pv_ref, vc_ref))):
        shifted[0:n_ch, :] = p_ref[:, hid_dim:2 * hid_dim]
        shifted[n_ch:n_ch + SUBLANES, :] = jnp.zeros((SUBLANES, hid_dim), F32)
        hid = (b1_ref[j:j + 1, :] + pe_ref[j, 0:1, 0:hid_dim] + pe_ref[j, 1:2, hid_dim:2 * hid_dim]
               + p_ref[:, 0:hid_dim] + shifted[1:1 + n_ch, :])
        a = hid * jax.nn.sigmoid(hid)
        out = _dot(a.astype(BF16), w2_ref[j].astype(BF16)) + b2_ref[j:j + 1, :]
        if j == 0:
            out = _rms(out) * kg_ref[...]
        o_ref[...] = out.astype(o_ref.dtype)


def _compress(src_pages, page_table, batch, w1, b1, w2, b2, pe, kg, *, layer=None):
    kvh = N_KV_HEADS
    hd = LANES
    n_log = page_table.shape[0]
    row_view = layer is not None
    n_pages = _tile(n_log, 16 if row_view else 8)
    chunks_per_page = PAGE_SIZE // CMP_STRIDE
    n_chunks = n_log * chunks_per_page
    hid2 = w1.shape[1] * w1.shape[-1]
    kdim = CMP_STRIDE * hd
    w1cat = jnp.transpose(w1, (0, 2, 3, 1, 4)).reshape(2, kdim, hid2)
    pe_rows = jnp.pad(pe.reshape(2, -1, kdim), ((0, 0), (0, SUBLANES - pe.shape[1]), (0, 0)))

    if row_view:
        rpt = 2 * kvh
        src_pages = src_pages.reshape(src_pages.shape[0], src_pages.shape[1], PAGE_SIZE * rpt, hd)
        page_specs = [pl.BlockSpec((None, None, PAGE_SIZE * rpt, hd),
                                   lambda i, pt, r=r: (layer, pt[i * n_pages + r], 0, 0)) for r in range(n_pages)]
    else:
        page_specs = [pl.BlockSpec((None, PAGE_SIZE, hd),
                                   lambda i, pt, r=r, cb=cb: (pt[i * n_pages + r], 0, cb))
                      for r in range(n_pages) for cb in range(2 * kvh)]
    rows = n_pages * chunks_per_page
    blk = (2 * n_pages * PAGE_SIZE * 2 * kvh * hd * 4 + 3 * 2 * kdim * hid2 * 4
           + 4 * kvh * rows * (kdim + hid2) * 4)
    parts, pe_out = pl.pallas_call(
        functools.partial(_cmp_part_kernel, n_pages=n_pages, kvh=kvh, row_view=row_view),
        out_shape=(jax.ShapeDtypeStruct((2, kvh, n_chunks, hid2), F32),
                   jax.ShapeDtypeStruct((2, SUBLANES, hid2), F32)),
        grid_spec=pltpu.PrefetchScalarGridSpec(
            num_scalar_prefetch=1,
            grid=(n_log // n_pages,),
            in_specs=page_specs + [
                pl.BlockSpec((2, kdim, hid2), lambda i, pt: (0, 0, 0)),
                pl.BlockSpec((2, SUBLANES, kdim), lambda i, pt: (0, 0, 0))],
            out_specs=(pl.BlockSpec((2, kvh, rows, hid2), lambda i, pt: (0, 0, i, 0)),
                       pl.BlockSpec((2, SUBLANES, hid2), lambda i, pt: (0, 0, 0))),
            scratch_shapes=[pltpu.VMEM((2, kdim, hid2), BF16)]),
        compiler_params=_params(("arbitrary",), blk),
        name="cmp_part",
    )(page_table, *([src_pages] * len(page_specs)), w1cat, pe_rows)

    n_ch = n_chunks // batch
    hid = hid2 // 2
    part_spec = lambda j: pl.BlockSpec((None, None, n_ch, hid2), lambda b, h: (j, h, b, 0))
    full = lambda *shape: pl.BlockSpec(shape, lambda b, h: (0,) * len(shape))
    out_spec = pl.BlockSpec((None, None, n_ch, hd), lambda b, h: (b, h, 0, 0))
    return pl.pallas_call(
        _cmp_out_kernel,
        out_shape=(jax.ShapeDtypeStruct((batch, kvh, n_ch, hd), BF16),) * 2,
        grid=(batch, kvh),
        in_specs=[part_spec(0), part_spec(1), full(2, hid), full(2, SUBLANES, hid2), full(2, hid, hd),
                  full(2, hd), full(1, hd)],
        out_specs=(out_spec, out_spec),
        scratch_shapes=[pltpu.VMEM((n_ch + SUBLANES, hid), F32)],
        compiler_params=_params(("arbitrary", "arbitrary"), 12 * n_ch * hid2 * 4),
        name="cmp_out",
    )(parts, parts, b1, pe_out, w2, b2, kg.reshape(1, hd))


def _overlap(n_ch, n_blk):
    c0 = _iota((n_ch, n_blk), 0) * CMP_STRIDE
    b0 = _iota((n_ch, n_blk), 1) * SEL_BLOCK
    return jnp.where(c0 < b0 + SEL_BLOCK, jnp.where(c0 + CMP_BLOCK > b0, 1.0, 0.0), 0.0).astype(BF16)


def _cmp_scores(q, kc, slope, tpos, n_cmp):
    n_ch = kc.shape[0]
    c_idx = _iota((1, n_ch), 1)
    d_c = tpos - (c_idx * CMP_STRIDE + (CMP_BLOCK - 1))
    valid = jnp.where(c_idx < n_cmp, d_c, -1) >= 0
    s = _dot_nt(q, kc) - slope * d_c.astype(F32)
    s = jnp.where(valid, s, -jnp.inf)
    m = jnp.max(s, axis=-1, keepdims=True)
    m = jnp.where(m == -jnp.inf, 0.0, m)
    e = jnp.exp2(s - m)
    return e / jnp.maximum(jnp.sum(e, axis=-1, keepdims=True), 1e-30)


def _with_position_lanes(k, pos0):
    rows, hd = k.shape
    pos = pos0 + _iota((rows, hd), 0)
    lane = _iota((rows, hd), 1)
    ext = jnp.where(lane < 3, pos // LANES * LANES, jnp.where(lane < 6, pos % LANES, 0))
    return jnp.concatenate([k, ext.astype(F32).astype(BF16)], axis=1)


def _with_slope_lanes(q, parts):
    lane = _iota(q.shape, 1)
    ext = jnp.zeros(q.shape, F32)
    for c, part in enumerate(parts):
        ext = jnp.where(lane == c, part, jnp.where(lane == c + 3, part, ext))
    return jnp.concatenate([q, ext.astype(BF16)], axis=1)


def _with_ones(v):
    return jnp.concatenate([v, jnp.ones(v.shape, BF16)], axis=1)


def _force_and_mask(imp, tpos):
    blk = _iota((1, imp.shape[1]), 1)
    cur = tpos // SEL_BLOCK
    forced = jnp.logical_or(blk == 0, jnp.logical_or(blk == cur, blk == cur - 1))
    imp = jnp.where(forced, FORCE_SCORE, imp)
    return jnp.where(blk > cur, -jnp.inf, imp)


def _attn_prompt_kernel(slope_ref, q_ref, kc_ref, vc_ref, ks_ref, vs_ref, kw_ref, vw_ref, gt_ref, o_ref,
                        m_s, acc_s, oc_s, qa_s, *, grp, seq, tq, tk, n_cmp, n_blk, n_top, wl):
    hk = pl.program_id(1)
    t0 = pl.program_id(2) * tq
    hd = LANES
    n_heads = slope_ref.shape[0] // 4
    row_t = t0 + _iota((tq, 1), 0)
    heads = [(g, slope_ref[hk * grp + g], slice(g * hd, (g + 1) * hd)) for g in range(grp)]

    kc = kc_ref[...]
    vc = vc_ref[...]
    n_ch = kc.shape[0]
    psum = jnp.zeros((tq, n_ch), F32)
    for g, slope, cols in heads:
        q = q_ref[:, cols]
        p = _cmp_scores(q, kc, slope, row_t, n_cmp)
        psum = psum + p
        oc_s[g] = _dot(p.astype(BF16), vc)
        qa_s[g] = _with_slope_lanes(q, [slope_ref[(1 + c) * n_heads + hk * grp + g] for c in range(3)])
    c0 = _iota((n_blk, n_ch), 1) * CMP_STRIDE
    b0 = _iota((n_blk, n_ch), 0) * SEL_BLOCK
    ov_t = jnp.where(c0 < b0 + SEL_BLOCK, jnp.where(c0 + CMP_BLOCK > b0, 1.0, 0.0), 0.0).astype(BF16)
    p1, p2, p3 = _split3(psum)
    imp = _dot_nt(ov_t, p1) + _dot_nt(ov_t, p2) + _dot_nt(ov_t, p3)
    blk = _iota((n_blk, 1), 0)
    cur = (t0 + _iota((1, tq), 1)) // SEL_BLOCK
    forced = jnp.logical_or(blk == 0, jnp.logical_or(blk == cur, blk == cur - 1))
    imp = jnp.where(blk > cur, -jnp.inf, jnp.where(forced, FORCE_SCORE, imp))
    rank = jnp.zeros((n_blk, tq), F32)
    for j in range(n_blk):
        rj = imp[j:j + 1, :]
        tie = jnp.where(blk > j, 1.0, 0.0)
        rank = rank + jnp.where(rj > imp, 1.0, jnp.where(rj == imp, tie, 0.0))
    sel_t = jnp.where(rank < n_top, 1.0, 0.0)
    nb_pad = -(-n_blk // LANES) * LANES
    sel_t = jnp.concatenate([sel_t, jnp.zeros((nb_pad - n_blk, tq), F32)], axis=0)
    sel = sel_t.T.astype(BF16)

    m_s[...] = jnp.full(m_s.shape, MASKED, F32)
    acc_s[...] = jnp.zeros(acc_s.shape, F32)

    def kv_tile(kt, carry):
        k0 = pl.multiple_of(kt * tk, tk)
        ka = _with_position_lanes(ks_ref[pl.ds(k0, tk), :].astype(BF16), k0)
        va = _with_ones(vs_ref[pl.ds(k0, tk), :].astype(BF16))
        expand = jnp.where(_iota((nb_pad, tk), 0) == (k0 + _iota((nb_pad, tk), 1)) // SEL_BLOCK, 1.0, 0.0)
        picked = _dot(sel, expand.astype(BF16))
        ok = jnp.where(row_t - (k0 + _iota((1, tk), 1)) >= 0, picked, 0.0) > 0.5
        for g, _, _ in heads:
            s = jnp.where(ok, _dot_nt(qa_s[g], ka), MASKED)
            m_old = m_s[g]
            m_new = jnp.maximum(m_old, jnp.max(s, axis=-1, keepdims=True))
            p = jnp.exp2(s - m_new)
            acc_s[g] = jnp.exp2(m_old - m_new) * acc_s[g] + _dot(p.astype(BF16), va)
            m_s[g] = m_new
        return carry

    lax.fori_loop(0, (t0 + tq + tk - 1) // tk, kv_tile, 0)

    ws = pl.multiple_of(jnp.clip(t0 + tq - wl, 0, seq - wl), SUBLANES)
    kwa = _with_position_lanes(kw_ref[pl.ds(ws, wl), :].astype(BF16), ws)
    vwa = _with_ones(vw_ref[pl.ds(ws, wl), :].astype(BF16))
    dw = row_t - (ws + _iota((1, wl), 1))
    okw = jnp.where(dw >= 0, jnp.where(dw < WINDOW, 1.0, 0.0), 0.0) > 0.5
    gates = gt_ref[...]
    for g, _, cols in heads:
        s = jnp.where(okw, _dot_nt(qa_s[g], kwa), MASKED)
        e = jnp.exp2(s - jnp.max(s, axis=-1, keepdims=True))
        win = _dot(e.astype(BF16), vwa)
        o_w = win[:, :hd] / win[:, hd:hd + 1]
        acc = acc_s[g]
        o_s = acc[:, :hd] / acc[:, hd:hd + 1]
        out = (gates[:, g:g + 1] * oc_s[g] + gates[:, grp + g:grp + g + 1] * o_s
               + gates[:, 2 * grp + g:2 * grp + g + 1] * o_w)
        o_ref[:, cols] = out.astype(o_ref.dtype)


def _attend_prompt(q, kc, vc, kv, gates, slopes, *, batch, seq):
    kvh = N_KV_HEADS
    grp = N_HEADS // kvh
    hd = LANES
    tq = _tile(seq, 256)
    tk = _tile(seq, 512)
    nq = seq // tq
    n_ch = kc.shape[2]
    n_blk = -(-seq // SEL_BLOCK)
    wl = min(tq + WINDOW, seq)
    kv3 = kv.reshape(batch, seq, kv.shape[1])
    row = lambda b, h, i: (b * nq + i, h)
    kv_spec = lambda cb: pl.BlockSpec((None, seq, hd), lambda b, h, i: (b, 0, cb * kvh + h))
    cmp_spec = pl.BlockSpec((None, None, n_ch, hd), lambda b, h, i: (b, h, 0, 0))
    blk = (8 * seq * hd * 4 + 4 * tq * grp * hd * 2 + 3 * grp * tq * hd * 4 + 10 * tq * max(tk, wl) * 4
           + 4 * n_ch * hd * 2)
    return pl.pallas_call(
        functools.partial(_attn_prompt_kernel, grp=grp, seq=seq, tq=tq, tk=tk, n_cmp=n_ch - 1, n_blk=n_blk,
                          n_top=min(N_SEL, n_blk), wl=wl),
        out_shape=jax.ShapeDtypeStruct((batch * seq, N_HEADS * hd), BF16),
        grid=(batch, kvh, nq),
        in_specs=[pl.BlockSpec(memory_space=pltpu.SMEM),
                  pl.BlockSpec((tq, grp * hd), row), cmp_spec, cmp_spec,
                  kv_spec(2), kv_spec(3), kv_spec(4), kv_spec(5),
                  pl.BlockSpec((tq, LANES), row)],
        out_specs=pl.BlockSpec((tq, grp * hd), row),
        scratch_shapes=[pltpu.VMEM((grp, tq, 1), F32), pltpu.VMEM((grp, tq, 2 * hd), F32),
                        pltpu.VMEM((grp, tq, hd), F32), pltpu.VMEM((grp, tq, 2 * hd), BF16)],
        compiler_params=_params(("arbitrary", "arbitrary", "arbitrary"), blk),
        name="nsa_attend_prompt",
    )(slopes, q, kc, vc, kv3, kv3, kv3, kv3, gates)


def _attn_sample_kernel(pt_ref, slope_ref, q_ref, kc_ref, vc_ref, *rest, n_pages, kvh, grp, dec_seq, past, n_cmp,
                        n_blk, n_top, nb_pad):
    pages = rest[:n_pages]
    new_ref, win_ref, gt_ref, o_ref, m_s, l_s, acc_s, oc_s, sel_s = rest[n_pages:]
    c = pl.program_id(1)
    rows = q_ref.shape[1]
    hd = LANES
    rpt = 2 * kvh
    row = _iota((rows, 1), 0)
    t_row = row // grp
    g_row = row % grp
    tpos = past + t_row
    tk = n_pages * PAGE_SIZE
    k0 = c * tk
    expand = jnp.where(_iota((nb_pad, tk), 0) == (k0 + _iota((nb_pad, tk), 1)) // SEL_BLOCK, 1.0, 0.0).astype(BF16)
    d_past = tpos - (k0 + _iota((1, tk), 1))
    d_past_f = d_past.astype(F32)

    def head_rows(ref, j, hk, n):
        return ref[pl.ds(j * kvh + hk, n, stride=rpt), :].astype(BF16)

    for hk in range(kvh):
        q = q_ref[hk]
        slope = jnp.zeros((rows, 1), F32)
        for g in range(grp):
            slope = jnp.where(g_row == g, slope_ref[hk * grp + g], slope)

        def online_update(s, v, hk=hk):
            m_old = m_s[hk]
            m_new = jnp.maximum(m_old, jnp.max(s, axis=-1, keepdims=True))
            alpha = jnp.exp2(m_old - m_new)
            p = jnp.exp2(s - m_new)
            l_s[hk] = alpha * l_s[hk] + jnp.sum(p, axis=-1, keepdims=True)
            acc_s[hk] = alpha * acc_s[hk] + _dot(p.astype(BF16), v)
            m_s[hk] = m_new

        @pl.when(c == 0)
        def _(hk=hk, q=q, slope=slope):
            kc = kc_ref[hk]
            n_ch = kc.shape[0]
            p = _cmp_scores(q, kc, slope, tpos, n_cmp)
            oc_s[hk] = _dot(p.astype(BF16), vc_ref[hk])
            ov = _overlap(n_ch, nb_pad)
            p1, p2, p3 = _split3(p)
            per_head = _dot(p1, ov) + _dot(p2, ov) + _dot(p3, ov)
            same_tok = jnp.where(_iota((rows, rows), 0) // grp == _iota((rows, rows), 1) // grp, 1.0, 0.0)
            same_tok = same_tok.astype(BF16)
            a1, a2, a3 = _split3(per_head)
            imp = _force_and_mask(_dot(same_tok, a1) + _dot(same_tok, a2) + _dot(same_tok, a3), tpos)
            blk_f = _iota((1, nb_pad), 1).astype(F32)
            taken = jnp.where(blk_f >= n_blk, 1.0, 0.0) + jnp.zeros((rows, nb_pad), F32)
            sel = jnp.zeros((rows, nb_pad), F32)
            for _ in range(n_top):
                avail = jnp.where(taken > 0.5, -jnp.inf, imp)
                best = jnp.max(avail, axis=-1, keepdims=True)
                cand = jnp.where(taken > 0.5, 0.0, jnp.where(avail == best, 1.0, 0.0))
                idx = jnp.min(jnp.where(cand > 0.5, blk_f, float(nb_pad)), axis=-1, keepdims=True)
                pick = blk_f == idx
                taken = jnp.where(pick, 1.0, taken)
                sel = jnp.where(pick, 1.0, sel)
            sel_s[hk] = sel.astype(BF16)
            m_s[hk] = jnp.full((rows, 1), MASKED, F32)
            l_s[hk] = jnp.zeros((rows, 1), F32)
            acc_s[hk] = jnp.zeros((rows, hd), F32)

        kk = jnp.concatenate([head_rows(pg, 0, hk, PAGE_SIZE) for pg in pages], axis=0)
        vv = jnp.concatenate([head_rows(pg, 1, hk, PAGE_SIZE) for pg in pages], axis=0)
        picked = _dot(sel_s[hk], expand)
        ok = jnp.where(d_past >= 0, picked, 0.0) > 0.5
        online_update(jnp.where(ok, _dot_nt(q, kk) - slope * d_past_f, MASKED), vv)

        @pl.when(c == pl.num_programs(1) - 1)
        def _(hk=hk, q=q, slope=slope, online_update=online_update):
            new_cols = lambda cb: slice((cb * kvh + hk) * hd, (cb * kvh + hk + 1) * hd)
            t_new = _iota((1, new_ref.shape[0]), 1)
            dn = t_row - t_new
            causal_new = jnp.where(dn >= 0, jnp.where(t_new < dec_seq, 1.0, 0.0), 0.0)
            dnf = dn.astype(F32)
            online_update(jnp.where(causal_new > 0.5, _dot_nt(q, new_ref[:, new_cols(2)].astype(BF16)) - slope * dnf,
                                    MASKED), new_ref[:, new_cols(3)].astype(BF16))
            o_s = acc_s[hk] / l_s[hk]
            n_buf = win_ref.shape[0] // rpt
            dc = tpos - (past - n_buf + _iota((1, n_buf), 1))
            ok_c = jnp.where(dc >= 0, jnp.where(dc < WINDOW, 1.0, 0.0), 0.0) > 0.5
            s_c = jnp.where(ok_c, _dot_nt(q, head_rows(win_ref, 0, hk, n_buf)) - slope * dc.astype(F32), MASKED)
            ok_n = jnp.where(dn < WINDOW, causal_new, 0.0) > 0.5
            s_n = jnp.where(ok_n, _dot_nt(q, new_ref[:, new_cols(4)].astype(BF16)) - slope * dnf, MASKED)
            m = jnp.maximum(jnp.max(s_c, axis=-1, keepdims=True), jnp.max(s_n, axis=-1, keepdims=True))
            e_c = jnp.exp2(s_c - m)
            e_n = jnp.exp2(s_n - m)
            den = jnp.sum(e_c, axis=-1, keepdims=True) + jnp.sum(e_n, axis=-1, keepdims=True)
            o_w = (_dot(e_c.astype(BF16), head_rows(win_ref, 1, hk, n_buf))
                   + _dot(e_n.astype(BF16), new_ref[:, new_cols(5)].astype(BF16))) / den
            gates = gt_ref[hk]
            o_ref[hk] = (gates[:, 0:1] * oc_s[hk] + gates[:, 1:2] * o_s + gates[:, 2:3] * o_w).astype(o_ref.dtype)


def _attend_sample(q_rows, kc, vc, cache_sel, page_table, kv_new, cache_win, gate_rows, slopes, *, layer, past,
                   dec_seq):
    batch, kvh, rows, hd = q_rows.shape
    grp = N_HEADS // kvh
    assert past % SEL_BLOCK == 0 and dec_seq <= SEL_BLOCK and past % PAGE_SIZE == 0
    pages_per_batch = past // PAGE_SIZE
    n_pages = _tile(pages_per_batch, 8)
    n_ch = kc.shape[2]
    n_blk = -(-(past + dec_seq) // SEL_BLOCK)
    nb_pad = -(-n_blk // LANES) * LANES
    n_layers, n_phys = cache_sel.shape[:2]
    n_buf = cache_win.shape[2]
    rpt = 2 * kvh
    sel_rows = cache_sel.reshape(n_layers, n_phys, PAGE_SIZE * rpt, hd)
    win_rows = cache_win.reshape(n_layers, batch, n_buf * rpt, hd)

    def page_spec(r):
        return pl.BlockSpec((None, None, PAGE_SIZE * rpt, hd),
                            lambda b, c, pt: (layer, pt[b * pages_per_batch + c * n_pages + r], 0, 0))

    per_batch = lambda *shape: pl.BlockSpec((None,) + shape, lambda b, c, pt: (b,) + (0,) * len(shape))
    row_spec = per_batch(kvh, rows, hd)
    tk = n_pages * PAGE_SIZE
    blk = (2 * n_pages * PAGE_SIZE * rpt * hd * 4 + 4 * kvh * n_ch * hd * 2 + 2 * n_buf * rpt * hd * 4
           + 16 * rows * max(tk, n_ch, nb_pad) * 4 + 3 * nb_pad * max(tk, n_ch) * 4)
    return pl.pallas_call(
        functools.partial(_attn_sample_kernel, n_pages=n_pages, kvh=kvh, grp=grp, dec_seq=dec_seq, past=past,
                          n_cmp=n_ch - 1, n_blk=n_blk, n_top=min(N_SEL, n_blk), nb_pad=nb_pad),
        out_shape=jax.ShapeDtypeStruct((batch, kvh, rows, hd), BF16),
        grid_spec=pltpu.PrefetchScalarGridSpec(
            num_scalar_prefetch=1,
            grid=(batch, pages_per_batch // n_pages),
            in_specs=[pl.BlockSpec(memory_space=pltpu.SMEM), row_spec, per_batch(kvh, n_ch, hd),
                      per_batch(kvh, n_ch, hd)]
            + [page_spec(r) for r in range(n_pages)]
            + [per_batch(kv_new.shape[1], kv_new.shape[2]),
               pl.BlockSpec((None, None, n_buf * rpt, hd), lambda b, c, pt: (layer, b, 0, 0)), row_spec],
            out_specs=row_spec,
            scratch_shapes=[pltpu.VMEM((kvh, rows, 1), F32), pltpu.VMEM((kvh, rows, 1), F32),
                            pltpu.VMEM((kvh, rows, hd), F32), pltpu.VMEM((kvh, rows, hd), F32),
                            pltpu.VMEM((kvh, rows, nb_pad), BF16)]),
        compiler_params=_params(("arbitrary", "arbitrary"), blk),
        name="nsa_attend_sample",
    )(page_table, slopes, q_rows, kc, vc, *([sel_rows] * n_pages), kv_new, win_rows, gate_rows)


def _nsa_layer(hp, hs, xp, xs, g1p, g1s, caches, page_table, weights, slopes, *, layer, batch, seq, dec_batch,
               dec_seq):
    (w_in, w_out, q_g, k_g, pe, w1, b1, w2, b2) = weights
    cache_cmp, cache_sel, cache_win = caches
    d = xp.shape[-1]
    kvh = N_KV_HEADS
    grp = N_HEADS // kvh
    hd = LANES
    cols = 2 * kvh * hd
    n_s = dec_batch * dec_seq
    past = page_table.shape[1] * PAGE_SIZE
    assert seq % PAGE_SIZE == 0 and past % CMP_STRIDE == 0 and dec_seq < CMP_STRIDE
    cmp_w = (w1, b1, w2, b2, pe, k_g[0])

    q, kv, gates = _nsa_project(hp, w_in, q_g, k_g)
    n_pages_p = batch * seq // PAGE_SIZE
    kc, vc = _compress(kv.reshape(n_pages_p, PAGE_SIZE, 3 * cols), jnp.arange(n_pages_p, dtype=I32), batch, *cmp_w)
    o = _attend_prompt(q, kc, vc, kv, gates, slopes, batch=batch, seq=seq)
    xp = _mm(o, w_out, col0=0, n=d, epi="res", extra=(xp.reshape(batch * seq, d), g1p), rows_per_batch=seq,
             name="nsa_out").reshape(batch, seq, d)
    kv_p = kv.reshape(batch, seq, 3, 2, kvh, hd)
    new_p = (kv_p[:, :, 0], kv_p[:, :, 1], kv_p[:, seq - min(WINDOW, seq):, 2])

    qs, kvs, gs = _nsa_project(hs, w_in, q_g, k_g)
    pt = page_table.reshape(-1)
    kcs, vcs = _compress(cache_cmp, pt, dec_batch, *cmp_w, layer=layer)
    q_rows = jnp.transpose(qs.reshape(dec_batch, dec_seq, kvh, grp, hd), (0, 2, 1, 3, 4))
    q_rows = q_rows.reshape(dec_batch, kvh, dec_seq * grp, hd)
    gate_rows = gs.reshape(dec_batch, dec_seq, kvh, LANES)[..., :3 * grp].reshape(dec_batch, dec_seq, kvh, 3, grp)
    gate_rows = jnp.transpose(gate_rows, (0, 2, 1, 4, 3)).reshape(dec_batch, kvh, dec_seq * grp, 3)
    gate_rows = jnp.pad(gate_rows, ((0, 0), (0, 0), (0, 0), (0, LANES - 3)))
    kv_new = jnp.pad(kvs.reshape(dec_batch, dec_seq, 3 * cols), ((0, 0), (0, SUBLANES - dec_seq), (0, 0)))
    n_buf = cache_win.shape[2]
    o_rows = _attend_sample(q_rows, kcs, vcs, cache_sel, pt, kv_new, cache_win, gate_rows, slopes, layer=layer,
                            past=past, dec_seq=dec_seq)
    o_s = jnp.transpose(o_rows.reshape(dec_batch, kvh, dec_seq, grp, hd), (0, 2, 1, 3, 4)).reshape(n_s, N_HEADS * hd)
    xs = _mm(o_s, w_out, col0=0, n=d, epi="res", extra=(xs.reshape(n_s, d), g1s[0]),
             name="nsa_out").reshape(1, n_s, d)
    kv_s = kvs.reshape(dec_batch, dec_seq, 3, 2, kvh, hd)
    win_s = jnp.concatenate([cache_win[layer], kv_s[:, :, 2]], axis=1)[:, -n_buf:]
    new_s = (kv_s[:, :, 0], kv_s[:, :, 1], win_s)
    return xp, xs, new_p, new_s


def kernel(x_prompt, x_sample, c_prompt, c_sample, state_conv, cache_cmp_kv, cache_sel_kv, cache_win_kv,
           page_table, w_mod, b_mod, norm_g, conv_w_in, conv_w, conv_w_out, nsa_w_in, nsa_w_out, q_norm_g,
           k_norm_g, cmp_pe, cmp_w1, cmp_b1, cmp_w2, cmp_b2, router_w, router_b, moe_w_gu, moe_b_gu,
           moe_w_down, moe_b_down):
    batch, seq, d = x_prompt.shape
    dec_batch, dec_seq, _ = x_sample.shape
    depth = w_mod.shape[0]
    n_s = dec_batch * dec_seq
    assert dec_seq >= conv_w.shape[1] - 1

    c_all = jnp.concatenate([c_prompt, c_sample], axis=0)
    c_all = jnp.pad(c_all, ((0, -c_all.shape[0] % SUBLANES), (0, 0)))
    mod = _adaln(c_all, w_mod, b_mod)
    slope2 = jnp.exp2(-8.0 * jnp.arange(1, N_HEADS + 1, dtype=F32) / N_HEADS) * LOG2_E
    slopes = jnp.concatenate([slope2] + [p.astype(F32) for p in _split3(slope2)])

    xp = x_prompt
    xs = x_sample.reshape(1, n_s, d)
    conv_p, conv_s, cmp_p, cmp_s, sel_p, sel_s, win_p, win_s = [], [], [], [], [], [], [], []
    for i in range(depth):
        j = i // 2
        sh1p, sc1p, g1p, sh2p, sc2p, g2p = [m[:, None, :] for m in jnp.split(mod[i, :batch], 6, axis=-1)]
        sh1s, sc1s, g1s, sh2s, sc2s, g2s = [jnp.repeat(m, dec_seq, axis=0)[None]
                                            for m in jnp.split(mod[i, batch:batch + dec_batch], 6, axis=-1)]
        hp = _norm_mod(xp, norm_g[i, 0], sc1p, sh1p).reshape(batch * seq, d)
        hs = _norm_mod(xs, norm_g[i, 0], sc1s, sh1s).reshape(n_s, d)
        if i % 2 == 0:
            a_p, state_p = _conv_in(hp, conv_w_in[j], conv_w[j], batch=batch, seq=seq)
            a_s, v_s = _conv_in(hs, conv_w_in[j], conv_w[j], batch=dec_batch, seq=dec_seq,
                                prev=_conv_prev_rows(state_conv[j], dec_seq))
            xp = _mm(a_p, conv_w_out[j], col0=0, n=d, epi="res", extra=(xp.reshape(batch * seq, d), g1p),
                     rows_per_batch=seq, name="conv_out").reshape(batch, seq, d)
            xs = _mm(a_s, conv_w_out[j], col0=0, n=d, epi="res", extra=(xs.reshape(n_s, d), g1s[0]),
                     name="conv_out").reshape(1, n_s, d)
            conv_p.append(state_p)
            conv_s.append(v_s.reshape(dec_batch, dec_seq, d)[:, dec_seq - 2:])
        else:
            weights = (nsa_w_in[j], nsa_w_out[j], q_norm_g[j], k_norm_g[j], cmp_pe[j], cmp_w1[j], cmp_b1[j],
                       cmp_w2[j], cmp_b2[j])
            xp, xs, new_p, new_s = _nsa_layer(
                hp, hs, xp, xs, g1p, g1s, (cache_cmp_kv, cache_sel_kv, cache_win_kv), page_table,
                weights, slopes, layer=j, batch=batch, seq=seq, dec_batch=dec_batch, dec_seq=dec_seq)
            cmp_p.append(new_p[0])
            sel_p.append(new_p[1])
            win_p.append(new_p[2])
            cmp_s.append(new_s[0])
            sel_s.append(new_s[1])
            win_s.append(new_s[2])
        xp, xs = _moe(xp, xs, norm_g[i, 1], (sc2p, sh2p, g2p), (sc2s, sh2s, g2s), router_w[i], router_b[i],
                      moe_w_gu, moe_b_gu, moe_w_down, moe_b_down, layer=i)
    return (xp, xs.reshape(dec_batch, dec_seq, d), jnp.stack(conv_p), jnp.stack(conv_s), jnp.stack(cmp_p),
            jnp.stack(cmp_s), jnp.stack(sel_p), jnp.stack(sel_s), jnp.stack(win_p), jnp.stack(win_s))
```

```python
import functools

import jax
import jax.numpy as jnp
from jax import lax
from jax.experimental import pallas as pl
from jax.experimental.pallas import tpu as pltpu

F32 = jnp.float32
BF16 = jnp.bfloat16
I32 = jnp.int32
U32 = jnp.uint32

N_HEADS = 32
N_KV_HEADS = 4
CMP_BLOCK = 32
CMP_STRIDE = 16
SEL_BLOCK = 64
N_SEL = 16
WINDOW = 512
PAGE_SIZE = 128
TOP_K = 4
SWIGLU_LIMIT = 7.0
SWIGLU_ALPHA = 1.702
EPS = 1e-6
FORCE_SCORE = 1e4
MASKED = -1e30
LOG2_E = 1.4426950408889634
MOE_SUB_ROWS = 272
MOE_SUBS_PER_SUPERBLOCK = 4

LANES = 128
SUBLANES = 8
VMEM_PHYSICAL_BYTES = 64 * 1024 * 1024
VMEM_CAP_BYTES = VMEM_PHYSICAL_BYTES - 6 * 1024 * 1024


def _vmem_limit(block_bytes):
    return int(min(VMEM_CAP_BYTES, block_bytes * 5 // 4 + (4 << 20)))


def _params(sem, block_bytes):
    return pltpu.CompilerParams(dimension_semantics=sem, vmem_limit_bytes=_vmem_limit(block_bytes))


def _tile(n, pref):
    if n <= pref:
        return n
    t = pref
    while n % t:
        t //= 2
    return t


def _dot(a, b):
    return jnp.dot(a, b, preferred_element_type=F32)


def _dot_nt(a, b):
    return lax.dot_general(a, b, (((1,), (1,)), ((), ())), preferred_element_type=F32)


def _split3(x):
    hi = x.astype(BF16)
    r = x - hi.astype(F32)
    mid = r.astype(BF16)
    lo = (r - mid.astype(F32)).astype(BF16)
    return hi, mid, lo


def _iota(shape, dim):
    return lax.broadcasted_iota(I32, shape, dim)


def _rms(a):
    return a * lax.rsqrt(jnp.mean(a * a, axis=-1, keepdims=True) + EPS)


def _adaln_kernel(c_ref, w_ref, b_ref, o_ref, *, kc):
    c = c_ref[...]
    a = (c * jax.nn.sigmoid(c)).astype(BF16)
    acc = jnp.zeros(o_ref.shape, F32)
    for k0 in range(0, a.shape[1], kc):
        acc = acc + _dot(a[:, k0:k0 + kc], w_ref[k0:k0 + kc, :].astype(BF16))
    o_ref[...] = acc + b_ref[...]


def _adaln(c_all, w_mod, b_mod):
    n_layers, d, n6 = w_mod.shape
    r = c_all.shape[0]
    tn = _tile(n6, 1024)
    kc = _tile(d, 1024)
    blk = 2 * d * tn * 4 + d * tn * 2 + r * d * 4
    return pl.pallas_call(
        functools.partial(_adaln_kernel, kc=kc),
        out_shape=jax.ShapeDtypeStruct((n_layers, r, n6), F32),
        grid=(n_layers, n6 // tn),
        in_specs=[
            pl.BlockSpec((r, d), lambda l, j: (0, 0)),
            pl.BlockSpec((None, d, tn), lambda l, j: (l, 0, j)),
            pl.BlockSpec((None, 1, tn), lambda l, j: (l, 0, j)),
        ],
        out_specs=pl.BlockSpec((None, r, tn), lambda l, j: (l, 0, j)),
        compiler_params=_params(("arbitrary", "arbitrary"), blk),
        name="adaln_mod",
    )(c_all, w_mod, b_mod.reshape(n_layers, 1, n6))


def _modulated(x_ref, g_ref, sc_ref, sh_ref):
    return _rms(x_ref[...]) * g_ref[...] * (1.0 + sc_ref[...]) + sh_ref[...]


def _norm_mod_kernel(x_ref, g_ref, sc_ref, sh_ref, h_ref):
    h_ref[...] = _modulated(x_ref, g_ref, sc_ref, sh_ref).astype(h_ref.dtype)


def _mod_spec(mod, tm, d):
    if mod.shape[1] == 1:
        return pl.BlockSpec((None, 1, d), lambda b, i: (b, 0, 0))
    return pl.BlockSpec((None, tm, d), lambda b, i: (b, i, 0))


def _norm_mod(x, g, scale, shift):
    b, t, d = x.shape
    tm = _tile(t, 512)
    blk = 2 * tm * d * (4 + 2) + 6 * d * 4 + 2 * tm * d * 4
    return pl.pallas_call(
        _norm_mod_kernel,
        out_shape=jax.ShapeDtypeStruct((b, t, d), BF16),
        grid=(b, t // tm),
        in_specs=[
            pl.BlockSpec((None, tm, d), lambda b_, i: (b_, i, 0)),
            pl.BlockSpec((1, d), lambda b_, i: (0, 0)),
            _mod_spec(scale, tm, d),
            _mod_spec(shift, tm, d),
        ],
        out_specs=pl.BlockSpec((None, tm, d), lambda b_, i: (b_, i, 0)),
        compiler_params=_params(("arbitrary", "arbitrary"), blk),
        name="norm_mod",
    )(x, g.reshape(1, d), scale, shift)


def _norm_router_kernel(x_ref, g_ref, sc_ref, sh_ref, wr_ref, br_ref, hp_ref, ti_ref, gt_ref):
    h = _modulated(x_ref, g_ref, sc_ref, sh_ref)
    tm, d = h.shape
    dh = d // 2
    lo = pltpu.bitcast(h[:, :dh].astype(BF16).astype(F32), U32)
    hi = pltpu.bitcast(h[:, dh:].astype(BF16).astype(F32), U32)
    hp_ref[...] = (lo >> 16) | hi

    h1, h2, h3 = _split3(h)
    w1, w2, w3 = _split3(wr_ref[...])
    logits = (_dot(h1, w1) + (_dot(h1, w2) + _dot(h2, w1))
              + (_dot(h2, w2) + _dot(h1, w3) + _dot(h3, w1))) + br_ref[...]
    n_exp = logits.shape[1]
    lane = _iota(logits.shape, 1).astype(F32)
    work = logits
    vals, idxs = [], []
    for _ in range(TOP_K):
        m = jnp.max(work, axis=-1, keepdims=True)
        idx = jnp.min(jnp.where(work == m, lane, float(n_exp)), axis=-1, keepdims=True)
        vals.append(m)
        idxs.append(idx)
        work = jnp.where(lane == idx, -jnp.inf, work)
    es = [jnp.exp(v - vals[0]) for v in vals]
    den = es[0]
    for e in es[1:]:
        den = den + e
    lane_o = _iota((tm, LANES), 1)
    ti = jnp.zeros((tm, LANES), F32)
    gt = jnp.zeros((tm, LANES), F32)
    for k in range(TOP_K):
        ti = jnp.where(lane_o == k, idxs[k], ti)
        gt = jnp.where(lane_o == k, es[k] / den, gt)
    ti_ref[...] = ti.astype(I32)
    gt_ref[...] = gt


def _norm_router(x, g, scale, shift, w_router, b_router):
    b, t, d = x.shape
    n_exp = w_router.shape[1]
    tm = _tile(t, 256)
    blk = 2 * tm * d * 4 + 2 * tm * d * 2 + 8 * tm * d * 4 + 2 * d * n_exp * 4
    row = lambda b_, i: (b_, i, 0)
    return pl.pallas_call(
        _norm_router_kernel,
        out_shape=(jax.ShapeDtypeStruct((b, t, d // 2), U32),
                   jax.ShapeDtypeStruct((b, t, LANES), I32),
                   jax.ShapeDtypeStruct((b, t, LANES), F32)),
        grid=(b, t // tm),
        in_specs=[
            pl.BlockSpec((None, tm, d), row),
            pl.BlockSpec((1, d), lambda b_, i: (0, 0)),
            _mod_spec(scale, tm, d),
            _mod_spec(shift, tm, d),
            pl.BlockSpec((d, n_exp), lambda b_, i: (0, 0)),
            pl.BlockSpec((1, n_exp), lambda b_, i: (0, 0)),
        ],
        out_specs=(pl.BlockSpec((None, tm, d // 2), row),
                   pl.BlockSpec((None, tm, LANES), row),
                   pl.BlockSpec((None, tm, LANES), row)),
        compiler_params=_params(("arbitrary", "arbitrary"), blk),
        name="norm_router",
    )(x, g.reshape(1, d), scale, shift, w_router, b_router.reshape(1, n_exp))


def _mm_kernel(a_ref, w_ref, *rest, epi, head_dim, scale):
    *ins, o_ref, wb = rest

    @pl.when(pl.program_id(1) == 0)
    def _():
        wb[...] = w_ref[...].astype(BF16)

    acc = _dot(a_ref[...], wb[...])
    tn = acc.shape[1]
    if epi == "res":
        x_ref, g_ref = ins
        o_ref[...] = x_ref[...] + g_ref[...] * acc
    elif epi == "qnorm":
        (gq_ref,) = ins
        for c in range(tn // head_dim):
            a = acc[:, c * head_dim:(c + 1) * head_dim]
            o_ref[:, c * head_dim:(c + 1) * head_dim] = (_rms(a) * gq_ref[...] * scale).astype(o_ref.dtype)
    elif epi == "kvnorm":
        fl_ref, gk_ref = ins
        for c in range(tn // head_dim):
            sl = slice(c * head_dim, (c + 1) * head_dim)
            a = acc[:, sl]
            o_ref[:, sl] = jnp.where(fl_ref[:, sl] > 0.5, _rms(a) * gk_ref[:, sl], a)
    elif epi == "sigmoid":
        o_ref[...] = jax.nn.sigmoid(acc)
    else:
        raise ValueError(epi)


def _mm(a, w, *, col0, n, epi, extra=(), rows_per_batch=None, out_dtype=F32, head_dim=LANES, scale=1.0,
        name="mm"):
    m, k = a.shape
    tm = _tile(m, 1024)
    tn = _tile(n, 512)
    assert col0 % tn == 0
    j0 = col0 // tn
    in_specs = [pl.BlockSpec((tm, k), lambda j, i: (i, 0)),
                pl.BlockSpec((k, tn), lambda j, i: (0, j + j0))]
    operands = [a, w]
    if epi == "res":
        x, g = extra
        in_specs.append(pl.BlockSpec((tm, tn), lambda j, i: (i, j)))
        if g.ndim == 3:
            assert rows_per_batch % tm == 0
            in_specs.append(pl.BlockSpec((None, 1, tn), lambda j, i: (i * tm // rows_per_batch, 0, j)))
        else:
            in_specs.append(pl.BlockSpec((tm, tn), lambda j, i: (i, j)))
        operands += [x, g]
    elif epi == "qnorm":
        in_specs.append(pl.BlockSpec((1, head_dim), lambda j, i: (0, 0)))
        operands += list(extra)
    elif epi == "kvnorm":
        in_specs += [pl.BlockSpec((1, tn), lambda j, i: (0, j))] * 2
        operands += list(extra)
    blk = 2 * tm * k * 2 + 2 * k * tn * 4 + k * tn * 2 + 6 * tm * tn * 4
    return pl.pallas_call(
        functools.partial(_mm_kernel, epi=epi, head_dim=head_dim, scale=scale),
        out_shape=jax.ShapeDtypeStruct((m, n), out_dtype),
        grid=(n // tn, m // tm),
        in_specs=in_specs,
        out_specs=pl.BlockSpec((tm, tn), lambda j, i: (i, j)),
        scratch_shapes=[pltpu.VMEM((k, tn), BF16)],
        compiler_params=_params(("arbitrary", "arbitrary"), blk),
        name=name,
    )(*operands)


def _conv_in_kernel(a_ref, wb_ref, wc_ref, wu_ref, cw_ref, *rest, tiles_per_batch, seq, per_token_prev):
    if per_token_prev:
        p1_ref, p2_ref, o_ref, v_ref, wbuf, ext = rest
    else:
        o_ref, st_ref, wbuf, ext = rest
    i = pl.program_id(1)

    @pl.when(i == 0)
    def _():
        wbuf[0] = wb_ref[...].astype(BF16)
        wbuf[1] = wc_ref[...].astype(BF16)
        wbuf[2] = wu_ref[...].astype(BF16)

    a = a_ref[...]
    b_gate = _dot(a, wbuf[0])
    v = _dot(a, wbuf[1]) * _dot(a, wbuf[2])
    tm, tn = v.shape

    if per_token_prev:
        ext[0:SUBLANES, :] = jnp.zeros((SUBLANES, tn), F32)
    else:
        @pl.when(i % tiles_per_batch == 0)
        def _():
            ext[0:SUBLANES, :] = jnp.zeros((SUBLANES, tn), F32)

    ext[SUBLANES:SUBLANES + tm, :] = v
    s1 = ext[SUBLANES - 1:SUBLANES - 1 + tm, :]
    s2 = ext[SUBLANES - 2:SUBLANES - 2 + tm, :]
    if per_token_prev:
        tpos = _iota((tm, 1), 0) % seq
        s1 = jnp.where(tpos >= 1, s1, p1_ref[...])
        s2 = jnp.where(tpos >= 2, s2, p2_ref[...])
    cw = cw_ref[...]
    conv = s2 * cw[0:1, :] + s1 * cw[1:2, :] + v * cw[2:3, :]
    o_ref[...] = (b_gate * conv).astype(o_ref.dtype)

    if per_token_prev:
        v_ref[...] = v
    else:
        ext[0:SUBLANES, :] = ext[tm:tm + SUBLANES, :]

        @pl.when(i % tiles_per_batch == tiles_per_batch - 1)
        def _():
            st_ref[...] = ext[tm + SUBLANES - 2:tm + SUBLANES, :]


def _conv_in(h, w_in, conv_w, *, batch, seq, prev=None):
    m, d = h.shape
    tn = _tile(d, 256)
    nd = d // tn
    per_token_prev = prev is not None
    tm = m if per_token_prev else _tile(seq, 512)
    tiles_per_batch = max(seq // tm, 1)
    in_specs = [pl.BlockSpec((tm, d), lambda j, i: (i, 0)),
                pl.BlockSpec((d, tn), lambda j, i: (0, j)),
                pl.BlockSpec((d, tn), lambda j, i: (0, j + nd)),
                pl.BlockSpec((d, tn), lambda j, i: (0, j + 2 * nd)),
                pl.BlockSpec((3, tn), lambda j, i: (0, j))]
    operands = [h, w_in, w_in, w_in, conv_w]
    tile_spec = pl.BlockSpec((tm, tn), lambda j, i: (i, j))
    if per_token_prev:
        in_specs += [tile_spec, tile_spec]
        operands += list(prev)
        out_shape = (jax.ShapeDtypeStruct((m, d), BF16), jax.ShapeDtypeStruct((m, d), F32))
        out_specs = (tile_spec, tile_spec)
    else:
        out_shape = (jax.ShapeDtypeStruct((m, d), BF16), jax.ShapeDtypeStruct((batch, 2, d), F32))
        out_specs = (tile_spec, pl.BlockSpec((None, 2, tn), lambda j, i: (i // tiles_per_batch, 0, j)))
    blk = 2 * tm * d * 2 + 6 * d * tn * 4 + 3 * d * tn * 2 + 10 * tm * tn * 4
    return pl.pallas_call(
        functools.partial(_conv_in_kernel, tiles_per_batch=tiles_per_batch, seq=seq,
                          per_token_prev=per_token_prev),
        out_shape=out_shape,
        grid=(nd, m // tm),
        in_specs=in_specs,
        out_specs=out_specs,
        scratch_shapes=[pltpu.VMEM((3, d, tn), BF16), pltpu.VMEM((tm + 2 * SUBLANES, tn), F32)],
        compiler_params=_params(("arbitrary", "arbitrary"), blk),
        name="conv_in",
    )(*operands)


def _conv_prev_rows(state, seq):
    b, _, d = state.shape
    zeros = jnp.zeros((b, seq, d), state.dtype)
    p1 = zeros.at[:, 0].set(state[:, 1])
    p2 = zeros.at[:, 0].set(state[:, 0]).at[:, 1].set(state[:, 1])
    return p1.reshape(b * seq, d), p2.reshape(b * seq, d)


def _plan_kernel(ti_ref, pos_ref, cnt_ref, carry):
    @pl.when(pl.program_id(0) == 0)
    def _():
        carry[...] = jnp.zeros(carry.shape, F32)

    ti = ti_ref[...]
    tm = ti.shape[0]
    e_iota = _iota((tm, LANES), 1)
    onehots = [jnp.where(ti[:, k:k + 1] == e_iota, 1.0, 0.0) for k in range(TOP_K)]
    hits = onehots[0]
    for oh in onehots[1:]:
        hits = hits + oh
    strictly_lower = jnp.where(_iota((tm, tm), 0) > _iota((tm, tm), 1), 1.0, 0.0).astype(BF16)
    before = _dot(strictly_lower, hits.astype(BF16)) + carry[...]
    out = jnp.zeros((tm, LANES), F32)
    for k in range(TOP_K):
        out = jnp.where(e_iota == k, jnp.sum(onehots[k] * before, axis=-1, keepdims=True), out)
    pos_ref[...] = out.astype(I32)
    carry[...] = carry[...] + jnp.sum(hits, axis=0, keepdims=True)
    cnt_ref[...] = carry[...]


def _plan(topi):
    n = topi.shape[0]
    tm = _tile(n, 256)
    return pl.pallas_call(
        _plan_kernel,
        out_shape=(jax.ShapeDtypeStruct((n, LANES), I32), jax.ShapeDtypeStruct((1, LANES), F32)),
        grid=(n // tm,),
        in_specs=[pl.BlockSpec((tm, LANES), lambda i: (i, 0))],
        out_specs=(pl.BlockSpec((tm, LANES), lambda i: (i, 0)), pl.BlockSpec((1, LANES), lambda i: (0, 0))),
        scratch_shapes=[pltpu.VMEM((1, LANES), F32)],
        compiler_params=_params(("arbitrary",), 8 * tm * LANES * 4 + tm * tm * 8),
        name="moe_plan",
    )(topi)


def _dispatch_tables(topi, n_exp, sub, rmax):
    n = topi.shape[0]
    ids = topi[:, :TOP_K]
    n_pairs = n * TOP_K
    n_pad = -(-n // 256) * 256
    pos, counts = _plan(jnp.pad(topi, ((0, n_pad - n), (0, 0)), constant_values=-1))
    counts = counts[0, :n_exp].astype(I32)
    padded = (counts + sub - 1) // sub * sub
    pad_start = jnp.cumsum(padded) - padded
    dest = pad_start[ids] + pos[:n, :TOP_K]
    n_rows = -(-(n_pairs + n_exp * (sub - 1)) // sub) * sub
    row_tok = jnp.zeros((n_rows,), I32).at[dest.reshape(-1)].set(jnp.arange(n_pairs, dtype=I32) // TOP_K)
    n_sb = (counts + rmax - 1) // rmax
    cum = jnp.cumsum(n_sb)
    total = cum[-1]
    n_sb_max = n_exp + n_pairs // rmax + 1
    s = jnp.arange(n_sb_max, dtype=I32)
    e_of = jnp.minimum(jnp.searchsorted(cum, s, side="right"), n_exp - 1).astype(I32)
    local = s - (cum - n_sb)[e_of]
    active = s < total
    rows = jnp.where(active, jnp.minimum(counts[e_of] - local * rmax, rmax), 0).astype(I32)
    start = jnp.where(active, pad_start[e_of] + local * rmax, 0).astype(I32)
    sb_exp = jnp.where(active, e_of, e_of[jnp.maximum(total - 1, 0)]).astype(I32)
    n_used = jnp.sum(padded).astype(I32).reshape(1)
    return dest.astype(I32), row_tok, sb_exp, start, rows, n_used, n_rows, n_sb_max


def _expert_kernel(exp_ref, start_ref, rows_ref, tok_ref, used_ref,
                   h_hbm, wg_ref, wu_ref, bg_ref, bu_ref, wd_ref, bd_ref, ys_hbm,
                   xbuf, act, wgu, wdb, ostage, gsem, osem, *, n1, sub, tf, dt):
    sb = pl.program_id(0)
    s = pl.program_id(1)
    rows = rows_ref[sb]
    start = start_ref[sb]
    n_sub = (rows + sub - 1) // sub
    dh = xbuf.shape[2]

    @pl.when(jnp.logical_and(sb == 0, s == 0))
    def _():
        ostage[0] = jnp.zeros(ostage.shape[1:], F32)
        n_tail = (ys_hbm.shape[0] - used_ref[0]) // sub

        def tail_copy(t, col):
            row0 = pl.multiple_of(used_ref[0] + t * sub, sub)
            return pltpu.make_async_copy(ostage.at[0], ys_hbm.at[pl.ds(row0, sub), pl.ds(col * dt, dt)],
                                         osem.at[0])

        def issue(t, c):
            for col in range(ys_hbm.shape[1] // dt):
                tail_copy(t, col).start()
            return c

        def drain(t, c):
            for col in range(ys_hbm.shape[1] // dt):
                tail_copy(t, col).wait()
            return c

        lax.fori_loop(0, n_tail, issue, 0)
        lax.fori_loop(0, n_tail, drain, 0)

    rmax = xbuf.shape[1]
    slot = sb % 2
    last_tok = tok_ref.shape[0] - 1

    def gather_copy(b, r):
        tok = tok_ref[jnp.minimum(start_ref[b] + r, last_tok)]
        return pltpu.make_async_copy(h_hbm.at[pl.ds(tok, 1)], xbuf.at[b % 2, pl.ds(r, 1)], gsem)

    def drain_gather(b):
        def drain(r, c):
            gather_copy(b, r).wait()
            return c

        lax.fori_loop(0, rmax, drain, 0)

    @pl.when(jnp.logical_and(s == 0, sb == 0))
    def _():
        def issue(r, c):
            gather_copy(0, r).start()
            return c

        lax.fori_loop(0, rmax, issue, 0)
        drain_gather(0)

    @pl.when(jnp.logical_and(s == 0, sb > 0))
    def _():
        @pl.when(rows_ref[jnp.maximum(sb - 1, 0)] > 0)
        def _():
            drain_gather(sb)

    per_step = rmax // (n1 + ys_hbm.shape[1] // dt)
    nxt = sb + 1

    def prefetch_slice():
        for j in range(per_step):
            gather_copy(nxt, s * per_step + j).start(priority=j % 2)

    def unpack(i):
        words = xbuf[slot, pl.ds(pl.multiple_of(i * sub, sub), sub), :]
        lo = pltpu.bitcast(words << 16, F32).astype(BF16)
        hi = pltpu.bitcast(words & jnp.uint32(0xFFFF0000), F32).astype(BF16)
        return lo, hi

    def swiglu(g, u):
        g = jnp.minimum(g, SWIGLU_LIMIT)
        u = jnp.clip(u, -SWIGLU_LIMIT, SWIGLU_LIMIT)
        return (g * jax.nn.sigmoid(SWIGLU_ALPHA * g) * (u + 1.0)).astype(BF16)

    @pl.when(jnp.logical_and(s < n1, rows > 0))
    def _():
        prefetch_slice()
        lo, hi = unpack(0)
        g = bg_ref[...]
        u = bu_ref[...]
        kc = min(dh, 512)
        for k0 in range(0, 2 * dh, kc):
            xk = lo[:, k0:k0 + kc] if k0 < dh else hi[:, k0 - dh:k0 - dh + kc]
            wgc = wg_ref[k0:k0 + kc, :].astype(BF16)
            wuc = wu_ref[k0:k0 + kc, :].astype(BF16)
            wgu[0, k0:k0 + kc, :] = wgc
            wgu[1, k0:k0 + kc, :] = wuc
            g = g + _dot(xk, wgc)
            u = u + _dot(xk, wuc)
        act[s, 0:sub, :] = swiglu(g, u)

        def sub_block(i, c):
            lo, hi = unpack(i)
            g = _dot(lo, wgu[0, 0:dh, :]) + _dot(hi, wgu[0, dh:2 * dh, :]) + bg_ref[...]
            u = _dot(lo, wgu[1, 0:dh, :]) + _dot(hi, wgu[1, dh:2 * dh, :]) + bu_ref[...]
            act[s, pl.ds(pl.multiple_of(i * sub, sub), sub), :] = swiglu(g, u)
            return c

        lax.fori_loop(1, n_sub, sub_block, 0)

    @pl.when(jnp.logical_and(s >= n1, rows > 0))
    def _():
        col = pl.multiple_of((s - n1) * dt, dt)

        def out_copy(i, slot):
            row0 = pl.multiple_of(start + i * sub, sub)
            return pltpu.make_async_copy(ostage.at[slot], ys_hbm.at[pl.ds(row0, sub), pl.ds(col, dt)],
                                         osem.at[slot])

        prefetch_slice()
        y = bd_ref[...]
        for f in range(n1):
            wdc = wd_ref[f * tf:(f + 1) * tf, :].astype(BF16)
            wdb[f * tf:(f + 1) * tf, :] = wdc
            y = y + _dot(act[f, 0:sub, :], wdc)
        ostage[0] = y
        out_copy(0, 0).start()

        def sub_block(i, c):
            slot = i % 2

            @pl.when(i >= 2)
            def _():
                out_copy(i - 2, slot).wait()

            r0 = pl.multiple_of(i * sub, sub)
            y = bd_ref[...] + _dot(act[0, pl.ds(r0, sub), :], wdb[0:tf, :])
            for f in range(1, n1):
                y = y + _dot(act[f, pl.ds(r0, sub), :], wdb[f * tf:(f + 1) * tf, :])
            ostage[slot] = y
            out_copy(i, slot).start()
            return c

        lax.fori_loop(1, n_sub, sub_block, 0)

        @pl.when(n_sub >= 2)
        def _():
            out_copy(n_sub - 2, n_sub % 2).wait()

        out_copy(n_sub - 1, (n_sub - 1) % 2).wait()


def _experts(h_packed, tables, w_gu, b_gu, w_down, b_down, *, layer, sub, rmax):
    _, row_tok, sb_exp, sb_start, sb_rows, n_used, n_rows, n_sb_max = tables
    n_layers, n_exp, d, f2 = w_gu.shape
    f = f2 // 2
    tf = _tile(f, 256)
    dt = _tile(d, 512)
    n1, n2 = f // tf, d // dt
    assert rmax % (n1 + n2) == 0

    def f_idx(sb, s, rows_ref):
        return jnp.where(rows_ref[sb] > 0, jnp.minimum(s, n1 - 1), n1 - 1)

    def d_idx(sb, s, rows_ref):
        return jnp.where(rows_ref[sb] > 0, jnp.maximum(s - n1, 0), n2 - 1)

    in_specs = [
        pl.BlockSpec(memory_space=pl.ANY),
        pl.BlockSpec((None, None, d, tf),
                     lambda sb, s, e, st, rw, tk, us: (layer, e[sb], 0, f_idx(sb, s, rw))),
        pl.BlockSpec((None, None, d, tf),
                     lambda sb, s, e, st, rw, tk, us: (layer, e[sb], 0, n1 + f_idx(sb, s, rw))),
        pl.BlockSpec((None, None, 1, tf),
                     lambda sb, s, e, st, rw, tk, us: (layer, e[sb], 0, f_idx(sb, s, rw))),
        pl.BlockSpec((None, None, 1, tf),
                     lambda sb, s, e, st, rw, tk, us: (layer, e[sb], 0, n1 + f_idx(sb, s, rw))),
        pl.BlockSpec((None, None, f, dt),
                     lambda sb, s, e, st, rw, tk, us: (layer, e[sb], 0, d_idx(sb, s, rw))),
        pl.BlockSpec((None, None, 1, dt),
                     lambda sb, s, e, st, rw, tk, us: (layer, e[sb], 0, d_idx(sb, s, rw))),
    ]
    blk = (2 * rmax * (d // 2) * 4 + rmax * f * 2 + 4 * d * tf * 4 + 2 * f * dt * 4 + 2 * d * tf * 2
           + f * dt * 2 + 2 * sub * dt * 4 + 2 * sub * d * 2)
    return pl.pallas_call(
        functools.partial(_expert_kernel, n1=n1, sub=sub, tf=tf, dt=dt),
        out_shape=jax.ShapeDtypeStruct((n_rows, d), F32),
        grid_spec=pltpu.PrefetchScalarGridSpec(
            num_scalar_prefetch=5,
            grid=(n_sb_max, n1 + n2),
            in_specs=in_specs,
            out_specs=pl.BlockSpec(memory_space=pl.ANY),
            scratch_shapes=[
                pltpu.VMEM((2, rmax, d // 2), U32),
                pltpu.VMEM((n1, rmax, tf), BF16),
                pltpu.VMEM((2, d, tf), BF16),
                pltpu.VMEM((f, dt), BF16),
                pltpu.VMEM((2, sub, dt), F32),
                pltpu.SemaphoreType.DMA(()),
                pltpu.SemaphoreType.DMA((2,)),
            ]),
        compiler_params=_params(("arbitrary", "arbitrary"), blk),
        name="moe_experts",
    )(sb_exp, sb_start, sb_rows, row_tok, n_used, h_packed, w_gu, w_gu,
      b_gu.reshape(n_layers, n_exp, 1, f2), b_gu.reshape(n_layers, n_exp, 1, f2), w_down,
      b_down.reshape(n_layers, n_exp, 1, d))


def _combine_kernel(dest_ref, ys_hbm, x_ref, g_ref, gate_ref, o_ref, buf, sem, *, tiles_per_batch):
    tm = x_ref.shape[0]
    tok0 = (pl.program_id(0) * tiles_per_batch + pl.program_id(1)) * tm

    def row_copy(r, k):
        src = dest_ref[(tok0 + r) * TOP_K + k]
        return pltpu.make_async_copy(ys_hbm.at[pl.ds(src, 1)], buf.at[k, pl.ds(r, 1)], sem)

    def issue(r, c):
        for k in range(TOP_K):
            row_copy(r, k).start(priority=k % 2)
        return c

    def drain(r, c):
        for k in range(TOP_K):
            row_copy(r, k).wait()
        return c

    lax.fori_loop(0, tm, issue, 0)
    lax.fori_loop(0, tm, drain, 0)
    gate = gate_ref[...]
    y = gate[:, 0:1] * buf[0]
    for k in range(1, TOP_K):
        y = y + gate[:, k:k + 1] * buf[k]
    o_ref[...] = x_ref[...] + g_ref[...] * y


def _combine(ys, dest, x, g, gate):
    b, t, d = x.shape
    tm = _tile(t, 128)
    tiles_per_batch = t // tm
    row = lambda b_, i, dref: (b_, i, 0)
    if g.shape[1] == 1:
        g_spec = pl.BlockSpec((None, 1, d), lambda b_, i, dref: (b_, 0, 0))
    else:
        g_spec = pl.BlockSpec((None, tm, d), row)
    blk = TOP_K * tm * d * 4 + 6 * tm * d * 4
    return pl.pallas_call(
        functools.partial(_combine_kernel, tiles_per_batch=tiles_per_batch),
        out_shape=jax.ShapeDtypeStruct((b, t, d), F32),
        grid_spec=pltpu.PrefetchScalarGridSpec(
            num_scalar_prefetch=1,
            grid=(b, tiles_per_batch),
            in_specs=[pl.BlockSpec(memory_space=pl.ANY),
                      pl.BlockSpec((None, tm, d), row),
                      g_spec,
                      pl.BlockSpec((None, tm, LANES), row)],
            out_specs=pl.BlockSpec((None, tm, d), row),
            scratch_shapes=[pltpu.VMEM((TOP_K, tm, d), F32), pltpu.SemaphoreType.DMA(())]),
        compiler_params=_params(("arbitrary", "arbitrary"), blk),
        name="moe_combine",
    )(dest.reshape(-1), ys, x, g, gate)


def _moe(xp, xs, g, mods_p, mods_s, w_router, b_router, w_gu, b_gu, w_down, b_down, *, layer):
    (sc_p, sh_p, g_p), (sc_s, sh_s, g_s) = mods_p, mods_s
    d = xp.shape[-1]
    n_exp = w_router.shape[1]
    f = w_down.shape[2]
    hp, ti_p, gt_p = _norm_router(xp, g, sc_p, sh_p, w_router, b_router)
    hs, ti_s, gt_s = _norm_router(xs, g, sc_s, sh_s, w_router, b_router)
    n_p = xp.shape[0] * xp.shape[1]
    h_all = jnp.concatenate([hp.reshape(n_p, d // 2), hs.reshape(-1, d // 2)], axis=0)
    topi = jnp.concatenate([ti_p.reshape(n_p, LANES), ti_s.reshape(-1, LANES)], axis=0)
    sub = MOE_SUB_ROWS
    rmax = MOE_SUBS_PER_SUPERBLOCK * sub
    tables = _dispatch_tables(topi, n_exp, sub, rmax)
    ys = _experts(h_all, tables, w_gu, b_gu, w_down, b_down, layer=layer, sub=sub, rmax=rmax)
    dest = tables[0]
    xp = _combine(ys, dest[:n_p], xp, g_p, gt_p)
    xs = _combine(ys, dest[n_p:], xs, g_s, gt_s)
    return xp, xs


def _nsa_project(h, w_in, q_g, k_g):
    d = h.shape[1]
    hd = LANES
    kvh = N_KV_HEADS
    grp = N_HEADS // kvh
    q_dim = N_HEADS * hd
    kv_dim = 2 * kvh * hd
    q = _mm(h, w_in, col0=0, n=q_dim, epi="qnorm", extra=(q_g.reshape(1, hd),), out_dtype=BF16,
            head_dim=hd, scale=hd ** -0.5 * LOG2_E, name="nsa_q")
    ones = jnp.ones((kvh * hd,), F32)
    zeros = jnp.zeros((kvh * hd,), F32)
    flags = jnp.concatenate([zeros, zeros, ones, zeros, ones, zeros]).reshape(1, 3 * kv_dim)
    gains = jnp.concatenate([ones, ones, jnp.tile(k_g[1], kvh), ones, jnp.tile(k_g[2], kvh), ones])
    kv = _mm(h, w_in, col0=q_dim, n=3 * kv_dim, epi="kvnorm", extra=(flags, gains.reshape(1, 3 * kv_dim)),
             head_dim=hd, name="nsa_kv")
    w_gate = w_in[:, q_dim + 3 * kv_dim:].reshape(d, 3, kvh, grp)
    w_gate = jnp.transpose(w_gate, (0, 2, 1, 3)).reshape(d, kvh, 3 * grp)
    w_gate = jnp.pad(w_gate, ((0, 0), (0, 0), (0, LANES - 3 * grp))).reshape(d, kvh * LANES)
    gates = _mm(h, w_gate, col0=0, n=kvh * LANES, epi="sigmoid", name="nsa_gates")
    return q, kv, gates


def _cmp_part_kernel(pt_ref, *refs, n_pages, kvh, row_view):
    n_refs = n_pages if row_view else n_pages * 2 * kvh
    pages = refs[:n_refs]
    w_ref, pe_ref, p_out, pe_out, wbf = refs[n_refs:]
    rpt = 2 * kvh
    chunks = PAGE_SIZE // CMP_STRIDE

    def chunk_rows(j, h, s):
        if row_view:
            return [pg[pl.ds(s * rpt + j * kvh + h, chunks, stride=CMP_STRIDE * rpt), :] for pg in pages]
        return [pg[pl.ds(s, chunks, stride=CMP_STRIDE), :] for pg in pages[j * kvh + h::rpt]]

    @pl.when(pl.program_id(0) == 0)
    def _():
        for j in range(2):
            wbf[j] = w_ref[j].astype(BF16)
            pe_out[j] = _dot(pe_ref[j].astype(BF16), wbf[j])

    rows = n_pages * PAGE_SIZE // CMP_STRIDE
    for j in range(2):
        per_head = []
        for h in range(kvh):
            pieces = [jnp.concatenate(chunk_rows(j, h, s), axis=0) for s in range(CMP_STRIDE)]
            per_head.append(jnp.concatenate(pieces, axis=1))
        x = jnp.concatenate(per_head, axis=0).astype(BF16)
        y = _dot(x, wbf[j])
        for h in range(kvh):
            p_out[j, h] = y[h * rows:(h + 1) * rows, :]


def _cmp_out_kernel(pk_ref, pv_ref, b1_ref, pe_ref, w2_ref, b2_ref, kg_ref, kc_ref, vc_ref, shifted):
    n_ch = pk_ref.shape[0]
    hid_dim = pk_ref.shape[1] // 2
    for j, (p_ref, o_ref) in enumerate(((pk_ref, kc_ref), (pv_ref, vc_ref))):
        shifted[0:n_ch, :] = p_ref[:, hid_dim:2 * hid_dim]
        shifted[n_ch:n_ch + SUBLANES, :] = jnp.zeros((SUBLANES, hid_dim), F32)
        hid = (b1_ref[j:j + 1, :] + pe_ref[j, 0:1, 0:hid_dim] + pe_ref[j, 1:2, hid_dim:2 * hid_dim]
               + p_ref[:, 0:hid_dim] + shifted[1:1 + n_ch, :])
        a = hid * jax.nn.sigmoid(hid)
        out = _dot(a.astype(BF16), w2_ref[j].astype(BF16)) + b2_ref[j:j + 1, :]
        if j == 0:
            out = _rms(out) * kg_ref[...]
        o_ref[...] = out.astype(o_ref.dtype)


def _compress(src_pages, page_table, batch, w1, b1, w2, b2, pe, kg, *, layer=None):
    kvh = N_KV_HEADS
    hd = LANES
    n_log = page_table.shape[0]
    row_view = layer is not None
    n_pages = _tile(n_log, 16 if row_view else 8)
    chunks_per_page = PAGE_SIZE // CMP_STRIDE
    n_chunks = n_log * chunks_per_page
    hid2 = w1.shape[1] * w1.shape[-1]
    kdim = CMP_STRIDE * hd
    w1cat = jnp.transpose(w1, (0, 2, 3, 1, 4)).reshape(2, kdim, hid2)
    pe_rows = jnp.pad(pe.reshape(2, -1, kdim), ((0, 0), (0, SUBLANES - pe.shape[1]), (0, 0)))

    if row_view:
        rpt = 2 * kvh
        src_pages = src_pages.reshape(src_pages.shape[0], src_pages.shape[1], PAGE_SIZE * rpt, hd)
        page_specs = [pl.BlockSpec((None, None, PAGE_SIZE * rpt, hd),
                                   lambda i, pt, r=r: (layer, pt[i * n_pages + r], 0, 0)) for r in range(n_pages)]
    else:
        page_specs = [pl.BlockSpec((None, PAGE_SIZE, hd),
                                   lambda i, pt, r=r, cb=cb: (pt[i * n_pages + r], 0, cb))
                      for r in range(n_pages) for cb in range(2 * kvh)]
    rows = n_pages * chunks_per_page
    blk = (2 * n_pages * PAGE_SIZE * 2 * kvh * hd * 4 + 3 * 2 * kdim * hid2 * 4
           + 4 * kvh * rows * (kdim + hid2) * 4)
    parts, pe_out = pl.pallas_call(
        functools.partial(_cmp_part_kernel, n_pages=n_pages, kvh=kvh, row_view=row_view),
        out_shape=(jax.ShapeDtypeStruct((2, kvh, n_chunks, hid2), F32),
                   jax.ShapeDtypeStruct((2, SUBLANES, hid2), F32)),
        grid_spec=pltpu.PrefetchScalarGridSpec(
            num_scalar_prefetch=1,
            grid=(n_log // n_pages,),
            in_specs=page_specs + [
                pl.BlockSpec((2, kdim, hid2), lambda i, pt: (0, 0, 0)),
                pl.BlockSpec((2, SUBLANES, kdim), lambda i, pt: (0, 0, 0))],
            out_specs=(pl.BlockSpec((2, kvh, rows, hid2), lambda i, pt: (0, 0, i, 0)),
                       pl.BlockSpec((2, SUBLANES, hid2), lambda i, pt: (0, 0, 0))),
            scratch_shapes=[pltpu.VMEM((2, kdim, hid2), BF16)]),
        compiler_params=_params(("arbitrary",), blk),
        name="cmp_part",
    )(page_table, *([src_pages] * len(page_specs)), w1cat, pe_rows)

    n_ch = n_chunks // batch
    hid = hid2 // 2
    part_spec = lambda j: pl.BlockSpec((None, None, n_ch, hid2), lambda b, h: (j, h, b, 0))
    full = lambda *shape: pl.BlockSpec(shape, lambda b, h: (0,) * len(shape))
    out_spec = pl.BlockSpec((None, None, n_ch, hd), lambda b, h: (b, h, 0, 0))
    return pl.pallas_call(
        _cmp_out_kernel,
        out_shape=(jax.ShapeDtypeStruct((batch, kvh, n_ch, hd), BF16),) * 2,
        grid=(batch, kvh),
        in_specs=[part_spec(0), part_spec(1), full(2, hid), full(2, SUBLANES, hid2), full(2, hid, hd),
                  full(2, hd), full(1, hd)],
        out_specs=(out_spec, out_spec),
        scratch_shapes=[pltpu.VMEM((n_ch + SUBLANES, hid), F32)],
        compiler_params=_params(("arbitrary", "arbitrary"), 12 * n_ch * hid2 * 4),
        name="cmp_out",
    )(parts, parts, b1, pe_out, w2, b2, kg.reshape(1, hd))


def _overlap(n_ch, n_blk):
    c0 = _iota((n_ch, n_blk), 0) * CMP_STRIDE
    b0 = _iota((n_ch, n_blk), 1) * SEL_BLOCK
    return jnp.where(c0 < b0 + SEL_BLOCK, jnp.where(c0 + CMP_BLOCK > b0, 1.0, 0.0), 0.0).astype(BF16)


def _cmp_scores(q, kc, slope, tpos, n_cmp):
    n_ch = kc.shape[0]
    c_idx = _iota((1, n_ch), 1)
    d_c = tpos - (c_idx * CMP_STRIDE + (CMP_BLOCK - 1))
    valid = jnp.where(c_idx < n_cmp, d_c, -1) >= 0
    s = _dot_nt(q, kc) - slope * d_c.astype(F32)
    s = jnp.where(valid, s, -jnp.inf)
    m = jnp.max(s, axis=-1, keepdims=True)
    m = jnp.where(m == -jnp.inf, 0.0, m)
    e = jnp.exp2(s - m)
    return e / jnp.maximum(jnp.sum(e, axis=-1, keepdims=True), 1e-30)


def _with_position_lanes(k, pos0):
    rows, hd = k.shape
    pos = pos0 + _iota((rows, hd), 0)
    lane = _iota((rows, hd), 1)
    ext = jnp.where(lane < 3, pos // LANES * LANES, jnp.where(lane < 6, pos % LANES, 0))
    return jnp.concatenate([k, ext.astype(F32).astype(BF16)], axis=1)


def _with_slope_lanes(q, parts):
    lane = _iota(q.shape, 1)
    ext = jnp.zeros(q.shape, F32)
    for c, part in enumerate(parts):
        ext = jnp.where(lane == c, part, jnp.where(lane == c + 3, part, ext))
    return jnp.concatenate([q, ext.astype(BF16)], axis=1)


def _with_ones(v):
    return jnp.concatenate([v, jnp.ones(v.shape, BF16)], axis=1)


def _force_and_mask(imp, tpos):
    blk = _iota((1, imp.shape[1]), 1)
    cur = tpos // SEL_BLOCK
    forced = jnp.logical_or(blk == 0, jnp.logical_or(blk == cur, blk == cur - 1))
    imp = jnp.where(forced, FORCE_SCORE, imp)
    return jnp.where(blk > cur, -jnp.inf, imp)


def _attn_prompt_kernel(slope_ref, q_ref, kc_ref, vc_ref, ks_ref, vs_ref, kw_ref, vw_ref, gt_ref, o_ref,
                        m_s, acc_s, oc_s, qa_s, *, grp, seq, tq, tk, n_cmp, n_blk, n_top, wl):
    hk = pl.program_id(1)
    t0 = pl.program_id(2) * tq
    hd = LANES
    n_heads = slope_ref.shape[0] // 4
    row_t = t0 + _iota((tq, 1), 0)
    heads = [(g, slope_ref[hk * grp + g], slice(g * hd, (g + 1) * hd)) for g in range(grp)]

    kc = kc_ref[...]
    vc = vc_ref[...]
    n_ch = kc.shape[0]
    psum = jnp.zeros((tq, n_ch), F32)
    for g, slope, cols in heads:
        q = q_ref[:, cols]
        p = _cmp_scores(q, kc, slope, row_t, n_cmp)
        psum = psum + p
        oc_s[g] = _dot(p.astype(BF16), vc)
        qa_s[g] = _with_slope_lanes(q, [slope_ref[(1 + c) * n_heads + hk * grp + g] for c in range(3)])
    c0 = _iota((n_blk, n_ch), 1) * CMP_STRIDE
    b0 = _iota((n_blk, n_ch), 0) * SEL_BLOCK
    ov_t = jnp.where(c0 < b0 + SEL_BLOCK, jnp.where(c0 + CMP_BLOCK > b0, 1.0, 0.0), 0.0).astype(BF16)
    p1, p2, p3 = _split3(psum)
    imp = _dot_nt(ov_t, p1) + _dot_nt(ov_t, p2) + _dot_nt(ov_t, p3)
    blk = _iota((n_blk, 1), 0)
    cur = (t0 + _iota((1, tq), 1)) // SEL_BLOCK
    forced = jnp.logical_or(blk == 0, jnp.logical_or(blk == cur, blk == cur - 1))
    imp = jnp.where(blk > cur, -jnp.inf, jnp.where(forced, FORCE_SCORE, imp))
    rank = jnp.zeros((n_blk, tq), F32)
    for j in range(n_blk):
        rj = imp[j:j + 1, :]
        tie = jnp.where(blk > j, 1.0, 0.0)
        rank = rank + jnp.where(rj > imp, 1.0, jnp.where(rj == imp, tie, 0.0))
    sel_t = jnp.where(rank < n_top, 1.0, 0.0)
    nb_pad = -(-n_blk // LANES) * LANES
    sel_t = jnp.concatenate([sel_t, jnp.zeros((nb_pad - n_blk, tq), F32)], axis=0)
    sel = sel_t.T.astype(BF16)

    m_s[...] = jnp.full(m_s.shape, MASKED, F32)
    acc_s[...] = jnp.zeros(acc_s.shape, F32)

    def kv_tile(kt, carry):
        k0 = pl.multiple_of(kt * tk, tk)
        ka = _with_position_lanes(ks_ref[pl.ds(k0, tk), :].astype(BF16), k0)
        va = _with_ones(vs_ref[pl.ds(k0, tk), :].astype(BF16))
        expand = jnp.where(_iota((nb_pad, tk), 0) == (k0 + _iota((nb_pad, tk), 1)) // SEL_BLOCK, 1.0, 0.0)
        picked = _dot(sel, expand.astype(BF16))
        ok = jnp.where(row_t - (k0 + _iota((1, tk), 1)) >= 0, picked, 0.0) > 0.5
        for g, _, _ in heads:
            s = jnp.where(ok, _dot_nt(qa_s[g], ka), MASKED)
            m_old = m_s[g]
            m_new = jnp.maximum(m_old, jnp.max(s, axis=-1, keepdims=True))
            p = jnp.exp2(s - m_new)
            acc_s[g] = jnp.exp2(m_old - m_new) * acc_s[g] + _dot(p.astype(BF16), va)
            m_s[g] = m_new
        return carry

    lax.fori_loop(0, (t0 + tq + tk - 1) // tk, kv_tile, 0)

    ws = pl.multiple_of(jnp.clip(t0 + tq - wl, 0, seq - wl), SUBLANES)
    kwa = _with_position_lanes(kw_ref[pl.ds(ws, wl), :].astype(BF16), ws)
    vwa = _with_ones(vw_ref[pl.ds(ws, wl), :].astype(BF16))
    dw = row_t - (ws + _iota((1, wl), 1))
    okw = jnp.where(dw >= 0, jnp.where(dw < WINDOW, 1.0, 0.0), 0.0) > 0.5
    gates = gt_ref[...]
    for g, _, cols in heads:
        s = jnp.where(okw, _dot_nt(qa_s[g], kwa), MASKED)
        e = jnp.exp2(s - jnp.max(s, axis=-1, keepdims=True))
        win = _dot(e.astype(BF16), vwa)
        o_w = win[:, :hd] / win[:, hd:hd + 1]
        acc = acc_s[g]
        o_s = acc[:, :hd] / acc[:, hd:hd + 1]
        out = (gates[:, g:g + 1] * oc_s[g] + gates[:, grp + g:grp + g + 1] * o_s
               + gates[:, 2 * grp + g:2 * grp + g + 1] * o_w)
        o_ref[:, cols] = out.astype(o_ref.dtype)


def _attend_prompt(q, kc, vc, kv, gates, slopes, *, batch, seq):
    kvh = N_KV_HEADS
    grp = N_HEADS // kvh
    hd = LANES
    tq = _tile(seq, 256)
    tk = _tile(seq, 512)
    nq = seq // tq
    n_ch = kc.shape[2]
    n_blk = -(-seq // SEL_BLOCK)
    wl = min(tq + WINDOW, seq)
    kv3 = kv.reshape(batch, seq, kv.shape[1])
    row = lambda b, h, i: (b * nq + i, h)
    kv_spec = lambda cb: pl.BlockSpec((None, seq, hd), lambda b, h, i: (b, 0, cb * kvh + h))
    cmp_spec = pl.BlockSpec((None, None, n_ch, hd), lambda b, h, i: (b, h, 0, 0))
    blk = (8 * seq * hd * 4 + 4 * tq * grp * hd * 2 + 3 * grp * tq * hd * 4 + 10 * tq * max(tk, wl) * 4
           + 4 * n_ch * hd * 2)
    return pl.pallas_call(
        functools.partial(_attn_prompt_kernel, grp=grp, seq=seq, tq=tq, tk=tk, n_cmp=n_ch - 1, n_blk=n_blk,
                          n_top=min(N_SEL, n_blk), wl=wl),
        out_shape=jax.ShapeDtypeStruct((batch * seq, N_HEADS * hd), BF16),
        grid=(batch, kvh, nq),
        in_specs=[pl.BlockSpec(memory_space=pltpu.SMEM),
                  pl.BlockSpec((tq, grp * hd), row), cmp_spec, cmp_spec,
                  kv_spec(2), kv_spec(3), kv_spec(4), kv_spec(5),
                  pl.BlockSpec((tq, LANES), row)],
        out_specs=pl.BlockSpec((tq, grp * hd), row),
        scratch_shapes=[pltpu.VMEM((grp, tq, 1), F32), pltpu.VMEM((grp, tq, 2 * hd), F32),
                        pltpu.VMEM((grp, tq, hd), F32), pltpu.VMEM((grp, tq, 2 * hd), BF16)],
        compiler_params=_params(("arbitrary", "arbitrary", "arbitrary"), blk),
        name="nsa_attend_prompt",
    )(slopes, q, kc, vc, kv3, kv3, kv3, kv3, gates)


def _attn_sample_kernel(pt_ref, slope_ref, q_ref, kc_ref, vc_ref, *rest, n_pages, kvh, grp, dec_seq, past, n_cmp,
                        n_blk, n_top, nb_pad):
    pages = rest[:n_pages]
    new_ref, win_ref, gt_ref, o_ref, m_s, l_s, acc_s, oc_s, sel_s = rest[n_pages:]
    c = pl.program_id(1)
    rows = q_ref.shape[1]
    hd = LANES
    rpt = 2 * kvh
    row = _iota((rows, 1), 0)
    t_row = row // grp
    g_row = row % grp
    tpos = past + t_row
    tk = n_pages * PAGE_SIZE
    k0 = c * tk
    expand = jnp.where(_iota((nb_pad, tk), 0) == (k0 + _iota((nb_pad, tk), 1)) // SEL_BLOCK, 1.0, 0.0).astype(BF16)
    d_past = tpos - (k0 + _iota((1, tk), 1))
    d_past_f = d_past.astype(F32)

    def head_rows(ref, j, hk, n):
        return ref[pl.ds(j * kvh + hk, n, stride=rpt), :].astype(BF16)

    for hk in range(kvh):
        q = q_ref[hk]
        slope = jnp.zeros((rows, 1), F32)
        for g in range(grp):
            slope = jnp.where(g_row == g, slope_ref[hk * grp + g], slope)

        def online_update(s, v, hk=hk):
            m_old = m_s[hk]
            m_new = jnp.maximum(m_old, jnp.max(s, axis=-1, keepdims=True))
            alpha = jnp.exp2(m_old - m_new)
            p = jnp.exp2(s - m_new)
            l_s[hk] = alpha * l_s[hk] + jnp.sum(p, axis=-1, keepdims=True)
            acc_s[hk] = alpha * acc_s[hk] + _dot(p.astype(BF16), v)
            m_s[hk] = m_new

        @pl.when(c == 0)
        def _(hk=hk, q=q, slope=slope):
            kc = kc_ref[hk]
            n_ch = kc.shape[0]
            p = _cmp_scores(q, kc, slope, tpos, n_cmp)
            oc_s[hk] = _dot(p.astype(BF16), vc_ref[hk])
            ov = _overlap(n_ch, nb_pad)
            p1, p2, p3 = _split3(p)
            per_head = _dot(p1, ov) + _dot(p2, ov) + _dot(p3, ov)
            same_tok = jnp.where(_iota((rows, rows), 0) // grp == _iota((rows, rows), 1) // grp, 1.0, 0.0)
            same_tok = same_tok.astype(BF16)
            a1, a2, a3 = _split3(per_head)
            imp = _force_and_mask(_dot(same_tok, a1) + _dot(same_tok, a2) + _dot(same_tok, a3), tpos)
            blk_f = _iota((1, nb_pad), 1).astype(F32)
            taken = jnp.where(blk_f >= n_blk, 1.0, 0.0) + jnp.zeros((rows, nb_pad), F32)
            sel = jnp.zeros((rows, nb_pad), F32)
            for _ in range(n_top):
                avail = jnp.where(taken > 0.5, -jnp.inf, imp)
                best = jnp.max(avail, axis=-1, keepdims=True)
                cand = jnp.where(taken > 0.5, 0.0, jnp.where(avail == best, 1.0, 0.0))
                idx = jnp.min(jnp.where(cand > 0.5, blk_f, float(nb_pad)), axis=-1, keepdims=True)
                pick = blk_f == idx
                taken = jnp.where(pick, 1.0, taken)
                sel = jnp.where(pick, 1.0, sel)
            sel_s[hk] = sel.astype(BF16)
            m_s[hk] = jnp.full((rows, 1), MASKED, F32)
            l_s[hk] = jnp.zeros((rows, 1), F32)
            acc_s[hk] = jnp.zeros((rows, hd), F32)

        kk = jnp.concatenate([head_rows(pg, 0, hk, PAGE_SIZE) for pg in pages], axis=0)
        vv = jnp.concatenate([head_rows(pg, 1, hk, PAGE_SIZE) for pg in pages], axis=0)
        picked = _dot(sel_s[hk], expand)
        ok = jnp.where(d_past >= 0, picked, 0.0) > 0.5
        online_update(jnp.where(ok, _dot_nt(q, kk) - slope * d_past_f, MASKED), vv)

        @pl.when(c == pl.num_programs(1) - 1)
        def _(hk=hk, q=q, slope=slope, online_update=online_update):
            new_cols = lambda cb: slice((cb * kvh + hk) * hd, (cb * kvh + hk + 1) * hd)
            t_new = _iota((1, new_ref.shape[0]), 1)
            dn = t_row - t_new
            causal_new = jnp.where(dn >= 0, jnp.where(t_new < dec_seq, 1.0, 0.0), 0.0)
            dnf = dn.astype(F32)
            online_update(jnp.where(causal_new > 0.5, _dot_nt(q, new_ref[:, new_cols(2)].astype(BF16)) - slope * dnf,
                                    MASKED), new_ref[:, new_cols(3)].astype(BF16))
            o_s = acc_s[hk] / l_s[hk]
            n_buf = win_ref.shape[0] // rpt
            dc = tpos - (past - n_buf + _iota((1, n_buf), 1))
            ok_c = jnp.where(dc >= 0, jnp.where(dc < WINDOW, 1.0, 0.0), 0.0) > 0.5
            s_c = jnp.where(ok_c, _dot_nt(q, head_rows(win_ref, 0, hk, n_buf)) - slope * dc.astype(F32), MASKED)
            ok_n = jnp.where(dn < WINDOW, causal_new, 0.0) > 0.5
            s_n = jnp.where(ok_n, _dot_nt(q, new_ref[:, new_cols(4)].astype(BF16)) - slope * dnf, MASKED)
            m = jnp.maximum(jnp.max(s_c, axis=-1, keepdims=True), jnp.max(s_n, axis=-1, keepdims=True))
            e_c = jnp.exp2(s_c - m)
            e_n = jnp.exp2(s_n - m)
            den = jnp.sum(e_c, axis=-1, keepdims=True) + jnp.sum(e_n, axis=-1, keepdims=True)
            o_w = (_dot(e_c.astype(BF16), head_rows(win_ref, 1, hk, n_buf))
                   + _dot(e_n.astype(BF16), new_ref[:, new_cols(5)].astype(BF16))) / den
            gates = gt_ref[hk]
            o_ref[hk] = (gates[:, 0:1] * oc_s[hk] + gates[:, 1:2] * o_s + gates[:, 2:3] * o_w).astype(o_ref.dtype)


def _attend_sample(q_rows, kc, vc, cache_sel, page_table, kv_new, cache_win, gate_rows, slopes, *, layer, past,
                   dec_seq):
    batch, kvh, rows, hd = q_rows.shape
    grp = N_HEADS // kvh
    assert past % SEL_BLOCK == 0 and dec_seq <= SEL_BLOCK and past % PAGE_SIZE == 0
    pages_per_batch = past // PAGE_SIZE
    n_pages = _tile(pages_per_batch, 8)
    n_ch = kc.shape[2]
    n_blk = -(-(past + dec_seq) // SEL_BLOCK)
    nb_pad = -(-n_blk // LANES) * LANES
    n_layers, n_phys = cache_sel.shape[:2]
    n_buf = cache_win.shape[2]
    rpt = 2 * kvh
    sel_rows = cache_sel.reshape(n_layers, n_phys, PAGE_SIZE * rpt, hd)
    win_rows = cache_win.reshape(n_layers, batch, n_buf * rpt, hd)

    def page_spec(r):
        return pl.BlockSpec((None, None, PAGE_SIZE * rpt, hd),
                            lambda b, c, pt: (layer, pt[b * pages_per_batch + c * n_pages + r], 0, 0))

    per_batch = lambda *shape: pl.BlockSpec((None,) + shape, lambda b, c, pt: (b,) + (0,) * len(shape))
    row_spec = per_batch(kvh, rows, hd)
    tk = n_pages * PAGE_SIZE
    blk = (2 * n_pages * PAGE_SIZE * rpt * hd * 4 + 4 * kvh * n_ch * hd * 2 + 2 * n_buf * rpt * hd * 4
           + 16 * rows * max(tk, n_ch, nb_pad) * 4 + 3 * nb_pad * max(tk, n_ch) * 4)
    return pl.pallas_call(
        functools.partial(_attn_sample_kernel, n_pages=n_pages, kvh=kvh, grp=grp, dec_seq=dec_seq, past=past,
                          n_cmp=n_ch - 1, n_blk=n_blk, n_top=min(N_SEL, n_blk), nb_pad=nb_pad),
        out_shape=jax.ShapeDtypeStruct((batch, kvh, rows, hd), BF16),
        grid_spec=pltpu.PrefetchScalarGridSpec(
            num_scalar_prefetch=1,
            grid=(batch, pages_per_batch // n_pages),
            in_specs=[pl.BlockSpec(memory_space=pltpu.SMEM), row_spec, per_batch(kvh, n_ch, hd),
                      per_batch(kvh, n_ch, hd)]
            + [page_spec(r) for r in range(n_pages)]
            + [per_batch(kv_new.shape[1], kv_new.shape[2]),
               pl.BlockSpec((None, None, n_buf * rpt, hd), lambda b, c, pt: (layer, b, 0, 0)), row_spec],
            out_specs=row_spec,
            scratch_shapes=[pltpu.VMEM((kvh, rows, 1), F32), pltpu.VMEM((kvh, rows, 1), F32),
                            pltpu.VMEM((kvh, rows, hd), F32), pltpu.VMEM((kvh, rows, hd), F32),
                            pltpu.VMEM((kvh, rows, nb_pad), BF16)]),
        compiler_params=_params(("arbitrary", "arbitrary"), blk),
        name="nsa_attend_sample",
    )(page_table, slopes, q_rows, kc, vc, *([sel_rows] * n_pages), kv_new, win_rows, gate_rows)


def _nsa_layer(hp, hs, xp, xs, g1p, g1s, caches, page_table, weights, slopes, *, layer, batch, seq, dec_batch,
               dec_seq):
    (w_in, w_out, q_g, k_g, pe, w1, b1, w2, b2) = weights
    cache_cmp, cache_sel, cache_win = caches
    d = xp.shape[-1]
    kvh = N_KV_HEADS
    grp = N_HEADS // kvh
    hd = LANES
    cols = 2 * kvh * hd
    n_s = dec_batch * dec_seq
    past = page_table.shape[1] * PAGE_SIZE
    assert seq % PAGE_SIZE == 0 and past % CMP_STRIDE == 0 and dec_seq < CMP_STRIDE
    cmp_w = (w1, b1, w2, b2, pe, k_g[0])

    q, kv, gates = _nsa_project(hp, w_in, q_g, k_g)
    n_pages_p = batch * seq // PAGE_SIZE
    kc, vc = _compress(kv.reshape(n_pages_p, PAGE_SIZE, 3 * cols), jnp.arange(n_pages_p, dtype=I32), batch, *cmp_w)
    o = _attend_prompt(q, kc, vc, kv, gates, slopes, batch=batch, seq=seq)
    xp = _mm(o, w_out, col0=0, n=d, epi="res", extra=(xp.reshape(batch * seq, d), g1p), rows_per_batch=seq,
             name="nsa_out").reshape(batch, seq, d)
    kv_p = kv.reshape(batch, seq, 3, 2, kvh, hd)
    new_p = (kv_p[:, :, 0], kv_p[:, :, 1], kv_p[:, seq - min(WINDOW, seq):, 2])

    qs, kvs, gs = _nsa_project(hs, w_in, q_g, k_g)
    pt = page_table.reshape(-1)
    kcs, vcs = _compress(cache_cmp, pt, dec_batch, *cmp_w, layer=layer)
    q_rows = jnp.transpose(qs.reshape(dec_batch, dec_seq, kvh, grp, hd), (0, 2, 1, 3, 4))
    q_rows = q_rows.reshape(dec_batch, kvh, dec_seq * grp, hd)
    gate_rows = gs.reshape(dec_batch, dec_seq, kvh, LANES)[..., :3 * grp].reshape(dec_batch, dec_seq, kvh, 3, grp)
    gate_rows = jnp.transpose(gate_rows, (0, 2, 1, 4, 3)).reshape(dec_batch, kvh, dec_seq * grp, 3)
    gate_rows = jnp.pad(gate_rows, ((0, 0), (0, 0), (0, 0), (0, LANES - 3)))
    kv_new = jnp.pad(kvs.reshape(dec_batch, dec_seq, 3 * cols), ((0, 0), (0, SUBLANES - dec_seq), (0, 0)))
    n_buf = cache_win.shape[2]
    o_rows = _attend_sample(q_rows, kcs, vcs, cache_sel, pt, kv_new, cache_win, gate_rows, slopes, layer=layer,
                            past=past, dec_seq=dec_seq)
    o_s = jnp.transpose(o_rows.reshape(dec_batch, kvh, dec_seq, grp, hd), (0, 2, 1, 3, 4)).reshape(n_s, N_HEADS * hd)
    xs = _mm(o_s, w_out, col0=0, n=d, epi="res", extra=(xs.reshape(n_s, d), g1s[0]),
             name="nsa_out").reshape(1, n_s, d)
    kv_s = kvs.reshape(dec_batch, dec_seq, 3, 2, kvh, hd)
    win_s = jnp.concatenate([cache_win[layer], kv_s[:, :, 2]], axis=1)[:, -n_buf:]
    new_s = (kv_s[:, :, 0], kv_s[:, :, 1], win_s)
    return xp, xs, new_p, new_s


def kernel(x_prompt, x_sample, c_prompt, c_sample, state_conv, cache_cmp_kv, cache_sel_kv, cache_win_kv,
           page_table, w_mod, b_mod, norm_g, conv_w_in, conv_w, conv_w_out, nsa_w_in, nsa_w_out, q_norm_g,
           k_norm_g, cmp_pe, cmp_w1, cmp_b1, cmp_w2, cmp_b2, router_w, router_b, moe_w_gu, moe_b_gu,
           moe_w_down, moe_b_down):
    batch, seq, d = x_prompt.shape
    dec_batch, dec_seq, _ = x_sample.shape
    depth = w_mod.shape[0]
    n_s = dec_batch * dec_seq
    assert dec_seq >= conv_w.shape[1] - 1

    c_all = jnp.concatenate([c_prompt, c_sample], axis=0)
    c_all = jnp.pad(c_all, ((0, -c_all.shape[0] % SUBLANES), (0, 0)))
    mod = _adaln(c_all, w_mod, b_mod)
    slope2 = jnp.exp2(-8.0 * jnp.arange(1, N_HEADS + 1, dtype=F32) / N_HEADS) * LOG2_E
    slopes = jnp.concatenate([slope2] + [p.astype(F32) for p in _split3(slope2)])

    xp = x_prompt
    xs = x_sample.reshape(1, n_s, d)
    conv_p, conv_s, cmp_p, cmp_s, sel_p, sel_s, win_p, win_s = [], [], [], [], [], [], [], []
    for i in range(depth):
        j = i // 2
        sh1p, sc1p, g1p, sh2p, sc2p, g2p = [m[:, None, :] for m in jnp.split(mod[i, :batch], 6, axis=-1)]
        sh1s, sc1s, g1s, sh2s, sc2s, g2s = [jnp.repeat(m, dec_seq, axis=0)[None]
                                            for m in jnp.split(mod[i, batch:batch + dec_batch], 6, axis=-1)]
        hp = _norm_mod(xp, norm_g[i, 0], sc1p, sh1p).reshape(batch * seq, d)
        hs = _norm_mod(xs, norm_g[i, 0], sc1s, sh1s).reshape(n_s, d)
        if i % 2 == 0:
            a_p, state_p = _conv_in(hp, conv_w_in[j], conv_w[j], batch=batch, seq=seq)
            a_s, v_s = _conv_in(hs, conv_w_in[j], conv_w[j], batch=dec_batch, seq=dec_seq,
                                prev=_conv_prev_rows(state_conv[j], dec_seq))
            xp = _mm(a_p, conv_w_out[j], col0=0, n=d, epi="res", extra=(xp.reshape(batch * seq, d), g1p),
                     rows_per_batch=seq, name="conv_out").reshape(batch, seq, d)
            xs = _mm(a_s, conv_w_out[j], col0=0, n=d, epi="res", extra=(xs.reshape(n_s, d), g1s[0]),
                     name="conv_out").reshape(1, n_s, d)
            conv_p.append(state_p)
            conv_s.append(v_s.reshape(dec_batch, dec_seq, d)[:, dec_seq - 2:])
        else:
            weights = (nsa_w_in[j], nsa_w_out[j], q_norm_g[j], k_norm_g[j], cmp_pe[j], cmp_w1[j], cmp_b1[j],
                       cmp_w2[j], cmp_b2[j])
            xp, xs, new_p, new_s = _nsa_layer(
                hp, hs, xp, xs, g1p, g1s, (cache_cmp_kv, cache_sel_kv, cache_win_kv), page_table,
                weights, slopes, layer=j, batch=batch, seq=seq, dec_batch=dec_batch, dec_seq=dec_seq)
            cmp_p.append(new_p[0])
            sel_p.append(new_p[1])
            win_p.append(new_p[2])
            cmp_s.append(new_s[0])
            sel_s.append(new_s[1])
            win_s.append(new_s[2])
        xp, xs = _moe(xp, xs, norm_g[i, 1], (sc2p, sh2p, g2p), (sc2s, sh2s, g2s), router_w[i], router_b[i],
                      moe_w_gu, moe_b_gu, moe_w_down, moe_b_down, layer=i)
    return (xp, xs.reshape(dec_batch, dec_seq, d), jnp.stack(conv_p), jnp.stack(conv_s), jnp.stack(cmp_p),
            jnp.stack(cmp_s), jnp.stack(sel_p), jnp.stack(sel_s), jnp.stack(win_p), jnp.stack(win_s))
```
